```python
import math
import jax
import jax.numpy as jnp
from jax import lax
import numpy as np

D_MODEL = 1024
BATCH = 16
SEQ = 4096
DEPTH = 1

N_HEADS_A = 8
HEAD_DIM_A = 64
ROPE_DIM_A = HEAD_DIM_A // 4
NOPE_DIM_A = HEAD_DIM_A - ROPE_DIM_A
KV_RANK = 128
IDX_HEADS = 8
IDX_DIM = 32
IDX_ROPE_DIM = IDX_DIM // 4
TOPK_MAX = 256
N_HEADS_B = 8
HEAD_DIM_B = 64
WIDTH_A = N_HEADS_A * HEAD_DIM_A
WIDTH_B = N_HEADS_B * HEAD_DIM_B
MIX_WIDTH = WIDTH_A + WIDTH_B
Q_BLOCK = 128
ROPE_THETA = 500000.0
PEER_HEADS = 8
N_KEYS = 128
N_EXPERTS = N_KEYS * N_KEYS
PEER_KEY_DIM = 128
PEER_TOPK = 16
PEER_CHUNK = 128
EPS = 1e-6
IN_SIZES = (WIDTH_A, KV_RANK, ROPE_DIM_A, IDX_HEADS * IDX_DIM, IDX_DIM, IDX_HEADS,
            WIDTH_B, WIDTH_B, WIDTH_B, N_HEADS_B)
IN_COLS = sum(IN_SIZES)

kernel_name = 'hymba_dsa_fox_peer_block'


def rms_norm(x, g):
    xf = x.astype(jnp.float32)
    y = xf * lax.rsqrt(jnp.mean(xf * xf, axis=-1, keepdims=True) + EPS)
    return (y * g.astype(jnp.float32)).astype(x.dtype)


def rope_tables(pos, rot_dim):
    inv = ROPE_THETA ** (-jnp.arange(0, rot_dim, 2, dtype=jnp.float32) / rot_dim)
    ang = pos.astype(jnp.float32)[..., None] * inv
    return jnp.cos(ang), jnp.sin(ang)


def apply_partial_rope(x, cos, sin):
    half = cos.shape[-1]
    cos = cos.astype(x.dtype)
    sin = sin.astype(x.dtype)
    x1 = x[..., :half]
    x2 = x[..., half:2 * half]
    return jnp.concatenate([x1 * cos - x2 * sin, x2 * cos + x1 * sin, x[..., 2 * half:]], axis=-1)


def to_blocks(a):
    b, s = a.shape[:2]
    return a.reshape(b, s // Q_BLOCK, Q_BLOCK, *a.shape[2:]).swapaxes(0, 1)


def from_blocks(a):
    nb, b, q = a.shape[:3]
    return a.swapaxes(0, 1).reshape(b, nb * q, *a.shape[3:])


def dsa_attention(q, latent, k_rope, iq, ik, iw, cos_a, sin_a, cos_i, sin_i, w_uk, w_uv):
    b, s = q.shape[:2]
    topk = min(TOPK_MAX, s // 4)
    q = apply_partial_rope(q, cos_a[:, :, None], sin_a[:, :, None])
    q_rope, q_nope = q[..., :ROPE_DIM_A], q[..., ROPE_DIM_A:]
    k_rope = apply_partial_rope(k_rope, cos_a, sin_a)
    q_lat = jnp.einsum('bshn,rhn->bshr', q_nope, w_uk)
    iq = apply_partial_rope(iq, cos_i[:, :, None], sin_i[:, :, None])
    ik = apply_partial_rope(ik, cos_i, sin_i)
    key_pos = jnp.arange(s)
    t_blocks = key_pos.reshape(s // Q_BLOCK, Q_BLOCK)
    gather_rows = jax.vmap(lambda table, ids: table[ids])
    scale = HEAD_DIM_A ** -0.5

    def block(args):
        q_lat_b, q_rope_b, iq_b, iw_b, t_b = args
        dots = jnp.einsum('bqhd,bsd->bqhs', iq_b, ik,
                          preferred_element_type=jnp.float32) * (IDX_DIM ** -0.5)
        w_h = iw_b.astype(jnp.float32) * (IDX_HEADS ** -0.5)
        score = jnp.einsum('bqhs,bqh->bqs', jax.nn.relu(dots), w_h)
        causal = key_pos[None, :] <= t_b[:, None]
        score = jnp.where(causal[None], score, -jnp.inf)
        _, sel = lax.top_k(score, topk)
        valid = sel <= t_b[None, :, None]
        lat_s = gather_rows(latent, sel)
        kr_s = gather_rows(k_rope, sel)
        logits = (jnp.einsum('bqhr,bqkr->bqhk', q_lat_b, lat_s, preferred_element_type=jnp.float32)
                  + jnp.einsum('bqhd,bqkd->bqhk', q_rope_b, kr_s, preferred_element_type=jnp.float32)) * scale
        logits = jnp.where(valid[:, :, None, :], logits, -jnp.inf)
        p = jax.nn.softmax(logits, axis=-1).astype(lat_s.dtype)
        return jnp.einsum('bqhk,bqkr->bqhr', p, lat_s)

    o = lax.map(block, (to_blocks(q_lat), to_blocks(q_rope), to_blocks(iq), to_blocks(iw), t_blocks))
    o = from_blocks(o)
    return jnp.einsum('bshr,rhd->bshd', o, w_uv).reshape(b, s, WIDTH_A)


def forgetting_attention(q, k, v, f_logit):
    b, s = q.shape[:2]
    cum = jnp.cumsum(jax.nn.log_sigmoid(f_logit.astype(jnp.float32)), axis=1)
    cum_k = cum.transpose(0, 2, 1)
    key_pos = jnp.arange(s)
    t_blocks = key_pos.reshape(s // Q_BLOCK, Q_BLOCK)
    scale = HEAD_DIM_B ** -0.5

    def block(args):
        q_b, cum_b, t_b = args
        logits = jnp.einsum('bqhd,bshd->bhqs', q_b, k, preferred_element_type=jnp.float32) * scale
        logits = logits + cum_b.transpose(0, 2, 1)[..., None] - cum_k[:, :, None, :]
        causal = key_pos[None, :] <= t_b[:, None]
        logits = jnp.where(causal, logits, -jnp.inf)
        p = jax.nn.softmax(logits, axis=-1).astype(v.dtype)
        return jnp.einsum('bhqs,bshd->bqhd', p, v)

    o = from_blocks(lax.map(block, (to_blocks(q), to_blocks(cum), t_blocks)))
    return o.reshape(b, s, WIDTH_B)


def peer_ffn(h, w_q, keys1, keys2, u, v):
    b, s, d = h.shape
    half = PEER_KEY_DIM // 2
    k1 = keys1.astype(jnp.float32)
    k2 = keys2.astype(jnp.float32)

    def chunk(hc):
        q = (hc @ w_q).reshape(-1, PEER_HEADS, PEER_KEY_DIM).astype(jnp.float32)
        s1 = jnp.einsum('chd,nd->chn', q[..., :half], k1)
        s2 = jnp.einsum('chd,nd->chn', q[..., half:], k2)
        v1, i1 = lax.top_k(s1, PEER_TOPK)
        v2, i2 = lax.top_k(s2, PEER_TOPK)
        cand_s = (v1[..., :, None] + v2[..., None, :]).reshape(*v1.shape[:-1], PEER_TOPK * PEER_TOPK)
        cand_i = (i1[..., :, None] * N_KEYS + i2[..., None, :]).reshape(*i1.shape[:-1], PEER_TOPK * PEER_TOPK)
        best_s, best_pos = lax.top_k(cand_s, PEER_TOPK)
        expert = jnp.take_along_axis(cand_i, best_pos, axis=-1)
        g = jax.nn.softmax(best_s, axis=-1)
        act = jax.nn.gelu(jnp.einsum('cd,chkd->chk', hc, u[expert],
                                     preferred_element_type=jnp.float32), approximate=False)
        wgt = (g * act).astype(hc.dtype)
        return jnp.einsum('chk,chkd->cd', wgt, v[expert])

    out = lax.map(chunk, h.reshape(-1, PEER_CHUNK, d))
    return out.reshape(b, s, d)


def setup_inputs(seed: int = 0) -> dict:
    key = jax.random.key(seed)
    ks = jax.random.split(key, 21)
    f32 = jnp.float32
    L, D = DEPTH, D_MODEL

    def nrm(k, shape, sc):
        return jax.random.normal(k, shape, f32) * sc

    def gain(k, shape):
        return 1.0 + 0.05 * jax.random.normal(k, shape, f32)

    x = nrm(ks[0], (BATCH, SEQ, D), 1.0)
    c = nrm(ks[1], (BATCH, D), 1.0)
    positions = (jnp.arange(SEQ, dtype=jnp.int32)[None, :]
                 + jax.random.randint(ks[2], (BATCH, 1), 0, 1024, dtype=jnp.int32))
    w_ada = nrm(ks[3], (L, D, 6 * D), 0.2 * D ** -0.5)
    b_ada = nrm(ks[4], (L, 6 * D), 0.1)
    g_mix = gain(ks[5], (L, D))
    w_in = nrm(ks[6], (L, D, IN_COLS), D ** -0.5)
    g_kv = gain(ks[7], (L, KV_RANK))
    w_uk = nrm(ks[8], (L, KV_RANK, N_HEADS_A, NOPE_DIM_A), KV_RANK ** -0.5)
    w_uv = nrm(ks[9], (L, KV_RANK, N_HEADS_A, HEAD_DIM_A), KV_RANK ** -0.5)
    b_forget = jax.random.uniform(ks[10], (L, N_HEADS_B), f32, 2.0, 6.0)
    g_out_a = gain(ks[11], (L, WIDTH_A))
    g_out_b = gain(ks[12], (L, WIDTH_B))
    w_out = nrm(ks[13], (L, MIX_WIDTH, D), MIX_WIDTH ** -0.5)
    g_ffn = gain(ks[14], (L, D))
    w_peer_q = nrm(ks[15], (L, D, PEER_HEADS * PEER_KEY_DIM), D ** -0.5)
    peer_keys1 = nrm(ks[16], (L, N_KEYS, PEER_KEY_DIM // 2), (PEER_KEY_DIM // 2) ** -0.5)
    peer_keys2 = nrm(ks[17], (L, N_KEYS, PEER_KEY_DIM // 2), (PEER_KEY_DIM // 2) ** -0.5)
    peer_u = nrm(ks[18], (L, N_EXPERTS, D), D ** -0.5)
    peer_v = nrm(ks[19], (L, N_EXPERTS, D), 0.5)
    g_final = gain(ks[20], (D,))
    return {'x': x, 'c': c, 'positions': positions, 'w_ada': w_ada, 'b_ada': b_ada,
            'g_mix': g_mix, 'w_in': w_in, 'g_kv': g_kv, 'w_uk': w_uk, 'w_uv': w_uv,
            'b_forget': b_forget, 'g_out_a': g_out_a, 'g_out_b': g_out_b, 'w_out': w_out,
            'g_ffn': g_ffn, 'w_peer_q': w_peer_q, 'peer_keys1': peer_keys1,
            'peer_keys2': peer_keys2, 'peer_u': peer_u, 'peer_v': peer_v, 'g_final': g_final}


def reference(x, c, positions, w_ada, b_ada, g_mix, w_in, g_kv, w_uk, w_uv, b_forget,
              g_out_a, g_out_b, w_out, g_ffn, w_peer_q, peer_keys1, peer_keys2, peer_u,
              peer_v, g_final):
    b, s, d = x.shape
    cos_a, sin_a = rope_tables(positions, ROPE_DIM_A)
    cos_i, sin_i = rope_tables(positions, IDX_ROPE_DIM)
    c_act = jax.nn.silu(c)
    split_at = []
    acc = 0
    for n in IN_SIZES[:-1]:
        acc += n
        split_at.append(acc)
    for l in range(DEPTH):
        mod = (c_act @ w_ada[l] + b_ada[l])[:, None, :]
        shift1, scale1, gate1, shift2, scale2, gate2 = jnp.split(mod, 6, axis=-1)
        h = rms_norm(x, g_mix[l]) * (1.0 + scale1) + shift1
        proj = h @ w_in[l]
        q_a, lat, k_r, iq, ik, iw, q_b, k_b, v_b, f_b = jnp.split(proj, split_at, axis=-1)
        o_a = dsa_attention(q_a.reshape(b, s, N_HEADS_A, HEAD_DIM_A), rms_norm(lat, g_kv[l]), k_r,
                            iq.reshape(b, s, IDX_HEADS, IDX_DIM), ik, iw,
                            cos_a, sin_a, cos_i, sin_i, w_uk[l], w_uv[l])
        o_b = forgetting_attention(q_b.reshape(b, s, N_HEADS_B, HEAD_DIM_B),
                                   k_b.reshape(b, s, N_HEADS_B, HEAD_DIM_B),
                                   v_b.reshape(b, s, N_HEADS_B, HEAD_DIM_B),
                                   f_b + b_forget[l])
        mixed = jnp.concatenate([rms_norm(o_a, g_out_a[l]), rms_norm(o_b, g_out_b[l])], axis=-1)
        x = x + gate1 * (mixed @ w_out[l])
        h2 = rms_norm(x, g_ffn[l]) * (1.0 + scale2) + shift2
        x = x + gate2 * peer_ffn(h2, w_peer_q[l], peer_keys1[l], peer_keys2[l], peer_u[l], peer_v[l])
    return rms_norm(x, g_final)
```

```python
import functools

import jax
import jax.numpy as jnp
import numpy as np
from jax import lax
from jax.experimental import pallas as pl
from jax.experimental.pallas import tpu as pltpu

f32 = jnp.float32
bf16 = jnp.bfloat16
i32 = jnp.int32

N_HEADS_A = 8
HEAD_DIM_A = 64
ROPE_DIM_A = 16
NOPE_DIM_A = HEAD_DIM_A - ROPE_DIM_A
KV_RANK = 128
IDX_HEADS = 8
IDX_DIM = 32
IDX_ROPE_DIM = 8
TOPK_MAX = 256
N_HEADS_B = 8
HEAD_DIM_B = 64
WIDTH_A = N_HEADS_A * HEAD_DIM_A
WIDTH_B = N_HEADS_B * HEAD_DIM_B
ROPE_THETA = 500000.0
PEER_HEADS = 8
N_KEYS = 128
PEER_KEY_DIM = 128
PEER_TOPK = 16
PEER_SLOTS = PEER_HEADS * PEER_TOPK
EPS = 1e-6
IN_SIZES = (WIDTH_A, KV_RANK, ROPE_DIM_A, IDX_HEADS * IDX_DIM, IDX_DIM, IDX_HEADS,
            WIDTH_B, WIDTH_B, WIDTH_B, N_HEADS_B)

LANES = 128
NEG_BIG = -1e30
INT_MIN = -2147483648
VMEM_LIMIT = 56 * 1024 * 1024

C_R1 = 0
C_R2 = 128
C_QN = 256
C_LAT = 640
C_MISC = 768
C_QB = 1024
C_KB = 1536
C_VB = 2048
C_TOTAL = 2560

NT_DIMS = (((1,), (1,)), ((), ()))


def _in_perm():
    offs = np.cumsum((0,) + IN_SIZES)
    o_qa, o_lat, o_kr, o_iq, o_ik, o_iw, o_qb, o_kb, o_vb, o_fb = offs[:10]
    perm = -np.ones((C_TOTAL,), np.int64)
    ha, hi = ROPE_DIM_A // 2, IDX_ROPE_DIM // 2
    for h in range(N_HEADS_A):
        for j in range(ha):
            perm[C_R1 + h * ha + j] = o_qa + h * HEAD_DIM_A + j
            perm[C_R1 + 64 + h * ha + j] = o_qa + h * HEAD_DIM_A + ha + j
        for j in range(NOPE_DIM_A):
            perm[C_QN + h * NOPE_DIM_A + j] = o_qa + h * HEAD_DIM_A + ROPE_DIM_A + j
    for h in range(IDX_HEADS):
        for j in range(hi):
            perm[C_R2 + h * hi + j] = o_iq + h * IDX_DIM + j
            perm[C_R2 + 64 + h * hi + j] = o_iq + h * IDX_DIM + hi + j
        for j in range(IDX_DIM - IDX_ROPE_DIM):
            perm[C_MISC + h * 24 + j] = o_iq + h * IDX_DIM + IDX_ROPE_DIM + j
    for j in range(ha):
        perm[C_R2 + 32 + j] = o_kr + j
        perm[C_R2 + 64 + 32 + j] = o_kr + ha + j
    for j in range(hi):
        perm[C_R2 + 40 + j] = o_ik + j
        perm[C_R2 + 64 + 40 + j] = o_ik + hi + j
    for j in range(IDX_DIM - IDX_ROPE_DIM):
        perm[C_MISC + 192 + j] = o_ik + IDX_ROPE_DIM + j
    for j in range(IDX_HEADS):
        perm[C_MISC + 216 + j] = o_iw + j
    for j in range(N_HEADS_B):
        perm[C_MISC + 224 + j] = o_fb + j
    perm[C_LAT:C_LAT + KV_RANK] = o_lat + np.arange(KV_RANK)
    perm[C_QB:C_QB + WIDTH_B] = o_qb + np.arange(WIDTH_B)
    perm[C_KB:C_KB + WIDTH_B] = o_kb + np.arange(WIDTH_B)
    perm[C_VB:C_VB + WIDTH_B] = o_vb + np.arange(WIDTH_B)
    return perm


def _params(sem):
    return pltpu.CompilerParams(dimension_semantics=sem, vmem_limit_bytes=VMEM_LIMIT)


def _mod_kernel(c_ref, w_ref, b_ref, o_ref):
    c = c_ref[...]
    ca = c * jax.nn.sigmoid(c)
    o_ref[...] = jnp.dot(ca, w_ref[...], preferred_element_type=f32,
                         precision=lax.Precision.HIGHEST) + b_ref[...]


def _mod_call(c, w, b):
    bsz, d = c.shape
    n = w.shape[1]
    return pl.pallas_call(
        _mod_kernel,
        grid=(n // d,),
        in_specs=[pl.BlockSpec((bsz, d), lambda j: (0, 0)),
                  pl.BlockSpec((d, d), lambda j: (0, j)),
                  pl.BlockSpec((1, d), lambda j: (0, j))],
        out_specs=pl.BlockSpec((bsz, d), lambda j: (0, j)),
        out_shape=jax.ShapeDtypeStruct((bsz, n), f32),
        compiler_params=_params(("arbitrary",)),
        name="mod",
    )(c, w, b.reshape(1, n))


def _inproj_kernel(x_ref, sc_ref, sh_ref, g_ref, w_ref, c1_ref, s1_ref, c2_ref, s2_ref,
                   gkv_ref, o_ref):
    x = x_ref[...]
    ms = jnp.mean(x * x, axis=-1, keepdims=True)
    h = x * lax.rsqrt(ms + EPS) * g_ref[...]
    h = h * (1.0 + sc_ref[0]) + sh_ref[0]
    p = jnp.dot(h.astype(bf16), w_ref[...], preferred_element_type=f32)
    r1 = p[:, C_R1:C_R1 + LANES]
    o_ref[:, C_R1:C_R1 + LANES] = r1 * c1_ref[...] + pltpu.roll(r1, 64, 1) * s1_ref[...]
    r2 = p[:, C_R2:C_R2 + LANES]
    o_ref[:, C_R2:C_R2 + LANES] = r2 * c2_ref[...] + pltpu.roll(r2, 64, 1) * s2_ref[...]
    o_ref[:, C_QN:C_LAT] = p[:, C_QN:C_LAT]
    lat = p[:, C_LAT:C_LAT + KV_RANK]
    lms = jnp.mean(lat * lat, axis=-1, keepdims=True)
    o_ref[:, C_LAT:C_LAT + KV_RANK] = lat * lax.rsqrt(lms + EPS) * gkv_ref[...]
    o_ref[:, C_MISC:] = p[:, C_MISC:]


def _inproj_call(x2, sc, sh, g, w, c1, s1, c2, s2, gkv, seq, tm):
    n, d = x2.shape
    per_b = seq // tm
    tok = lambda i: (i, 0)
    bat = lambda i: (i // per_b, 0, 0)
    cst = lambda i: (0, 0)
    return pl.pallas_call(
        _inproj_kernel,
        grid=(n // tm,),
        in_specs=[pl.BlockSpec((tm, d), tok),
                  pl.BlockSpec((1, 1, d), bat),
                  pl.BlockSpec((1, 1, d), bat),
                  pl.BlockSpec((1, d), cst),
                  pl.BlockSpec((d, C_TOTAL), cst),
                  pl.BlockSpec((tm, LANES), tok),
                  pl.BlockSpec((tm, LANES), tok),
                  pl.BlockSpec((tm, LANES), tok),
                  pl.BlockSpec((tm, LANES), tok),
                  pl.BlockSpec((1, KV_RANK), cst)],
        out_specs=pl.BlockSpec((tm, C_TOTAL), tok),
        out_shape=jax.ShapeDtypeStruct((n, C_TOTAL), f32),
        compiler_params=_params(("parallel",)),
        name="inproj",
    )(x2, sc, sh, g, w, c1, s1, c2, s2, gkv)


def _dsa_kernel(qn_ref, qr_ref, iq_ref, iw_ref, lat_ref, kr_ref, ik_ref, wuk_ref, wuv_ref,
                o_ref, key_scr, m_scr, l_scr, acc_scr, *, tq, tk, topk, nbits):
    nh = N_HEADS_A
    qi = pl.program_id(1)
    nc = ((qi + 1) * tq + tk - 1) // tk
    row = qi * tq + lax.broadcasted_iota(i32, (tq, tk), 0)
    lane = lax.broadcasted_iota(i32, (tq, tk), 1)

    iq_all = iq_ref[...].reshape(IDX_HEADS * tq, IDX_DIM).astype(bf16)
    iw = iw_ref[...] * (IDX_DIM ** -0.5 * IDX_HEADS ** -0.5)

    def score_chunk(c, carry):
        start = pl.multiple_of(c * tk, tk)
        ikc = ik_ref[pl.ds(start, tk), :].astype(bf16)
        d = lax.dot_general(iq_all, ikc, NT_DIMS, preferred_element_type=f32)
        sc = jnp.zeros((tq, tk), f32)
        for h in range(IDX_HEADS):
            sc = sc + jnp.maximum(d[h * tq:(h + 1) * tq], 0.0) * iw[:, h:h + 1]
        sc = jnp.where(start + lane <= row, sc, -jnp.inf)
        bits = lax.bitcast_convert_type(sc, i32)
        key_scr[c] = bits ^ ((bits >> 31) & 0x7FFFFFFF)
        return carry

    lax.fori_loop(0, nc, score_chunk, 0)

    def count(pred):
        def body(c, acc):
            m = pred(key_scr[c], c * tk + lane)
            part = m[:, 0:LANES]
            for j in range(1, tk // LANES):
                part = part + m[:, j * LANES:(j + 1) * LANES]
            return acc + part
        acc = lax.fori_loop(0, nc, body, jnp.zeros((tq, LANES), i32))
        return jnp.sum(acc, axis=1, keepdims=True)

    def bit_body(i, thr):
        cand = thr + lax.shift_left(jnp.int32(1), 31 - i)
        cnt = count(lambda k, col: jnp.where(k >= cand, 1, 0))
        return jnp.where(cnt >= topk, cand, thr)

    thr = lax.fori_loop(0, 32, bit_body, jnp.full((tq, 1), INT_MIN, i32))
    need = topk - count(lambda k, col: jnp.where(k > thr, 1, 0))

    def tie_body(i, last):
        cand = last + lax.shift_left(jnp.int32(1), nbits - 1 - i)
        below = count(lambda k, col: jnp.where(k == thr, jnp.where(col < cand, 1, 0), 0))
        return jnp.where(below < need, cand, last)

    last = lax.fori_loop(0, nbits, tie_body, jnp.zeros((tq, 1), i32))

    scale = HEAD_DIM_A ** -0.5
    qls = []
    for h in range(nh):
        qls.append(jnp.dot(qn_ref[h].astype(bf16), wuk_ref[h], preferred_element_type=f32))
    q_lat = (jnp.concatenate(qls, axis=0) * scale).astype(bf16)
    q_rope = (qr_ref[...].reshape(nh * tq, ROPE_DIM_A) * scale).astype(bf16)

    m_scr[...] = jnp.full(m_scr.shape, NEG_BIG, f32)
    l_scr[...] = jnp.zeros(l_scr.shape, f32)
    acc_scr[...] = jnp.zeros(acc_scr.shape, f32)

    def attend_chunk(c, carry):
        start = pl.multiple_of(c * tk, tk)
        latc = lat_ref[pl.ds(start, tk), :].astype(bf16)
        krc = kr_ref[pl.ds(start, tk), :].astype(bf16)
        s = (lax.dot_general(q_lat, latc, NT_DIMS, preferred_element_type=f32)
             + lax.dot_general(q_rope, krc, NT_DIMS, preferred_element_type=f32))
        k = key_scr[c]
        col = start + lane
        tie = jnp.where(k == thr, jnp.where(col <= last, 0.0, NEG_BIG), NEG_BIG)
        bias = jnp.where(col <= row, jnp.where(k > thr, 0.0, tie), NEG_BIG)
        s = (s.reshape(nh, tq, tk) + bias[None]).reshape(nh * tq, tk)
        m_old = m_scr[...]
        m_new = jnp.maximum(m_old, jnp.max(s, axis=1, keepdims=True))
        p = jnp.exp(s - m_new)
        alpha = jnp.exp(m_old - m_new)
        l_scr[...] = alpha * l_scr[...] + jnp.sum(p, axis=1, keepdims=True)
        acc_scr[...] = alpha * acc_scr[...] + jnp.dot(p.astype(bf16), latc,
                                                      preferred_element_type=f32)
        m_scr[...] = m_new
        return carry

    lax.fori_loop(0, nc, attend_chunk, 0)

    o = (acc_scr[...] / l_scr[...]).astype(bf16)
    outs = []
    for h in range(nh):
        outs.append(jnp.dot(o[h * tq:(h + 1) * tq], wuv_ref[h], preferred_element_type=f32))
    o_ref[...] = jnp.concatenate(outs, axis=1)


def _dsa_call(qn, qr, iq, iw, lat, kr, ik, wuk, wuv, tq, tk):
    bsz, nh, seq, _ = qn.shape
    topk = min(TOPK_MAX, seq // 4)
    nbits = max(1, (seq - 1).bit_length())
    kern = functools.partial(_dsa_kernel, tq=tq, tk=tk, topk=topk, nbits=nbits)
    qblk = lambda last: pl.BlockSpec((None, nh, tq, last), lambda b, i: (b, 0, i, 0))
    kblk = lambda last: pl.BlockSpec((None, seq, last), lambda b, i: (b, 0, 0))
    return pl.pallas_call(
        kern,
        grid=(bsz, seq // tq),
        in_specs=[qblk(NOPE_DIM_A), qblk(ROPE_DIM_A), qblk(IDX_DIM),
                  pl.BlockSpec((None, tq, IDX_HEADS), lambda b, i: (b, i, 0)),
                  kblk(KV_RANK), kblk(ROPE_DIM_A), kblk(IDX_DIM),
                  pl.BlockSpec(wuk.shape, lambda b, i: (0, 0, 0)),
                  pl.BlockSpec(wuv.shape, lambda b, i: (0, 0, 0))],
        out_specs=pl.BlockSpec((None, tq, WIDTH_A), lambda b, i: (b, i, 0)),
        out_shape=jax.ShapeDtypeStruct((bsz, seq, WIDTH_A), f32),
        scratch_shapes=[pltpu.VMEM((seq // tk, tq, tk), i32),
                        pltpu.VMEM((nh * tq, 1), f32),
                        pltpu.VMEM((nh * tq, 1), f32),
                        pltpu.VMEM((nh * tq, KV_RANK), f32)],
        compiler_params=_params(("parallel", "arbitrary")),
        name="dsa",
    )(qn, qr, iq, iw, lat, kr, ik, wuk, wuv)


def _fox_kernel(q_ref, k_ref, v_ref, cq_ref, ck_ref, o_ref, m_scr, l_scr, acc_scr, *, t):
    qi = pl.program_id(2)
    kj = pl.program_id(3)

    @pl.when(kj == 0)
    def _():
        m_scr[...] = jnp.full(m_scr.shape, NEG_BIG, f32)
        l_scr[...] = jnp.zeros(l_scr.shape, f32)
        acc_scr[...] = jnp.zeros(acc_scr.shape, f32)

    @pl.when(kj <= qi)
    def _():
        q = (q_ref[...] * (HEAD_DIM_B ** -0.5)).astype(bf16)
        s = lax.dot_general(q, k_ref[...].astype(bf16), NT_DIMS, preferred_element_type=f32)
        s = s + cq_ref[...] - ck_ref[...]
        row = qi * t + lax.broadcasted_iota(i32, (t, t), 0)
        col = kj * t + lax.broadcasted_iota(i32, (t, t), 1)
        s = jnp.where(col <= row, s, NEG_BIG)
        m_old = m_scr[...]
        m_new = jnp.maximum(m_old, jnp.max(s, axis=1, keepdims=True))
        p = jnp.exp(s - m_new)
        alpha = jnp.exp(m_old - m_new)
        l_scr[...] = alpha * l_scr[...] + jnp.sum(p, axis=1, keepdims=True)
        acc_scr[...] = alpha * acc_scr[...] + jnp.dot(p.astype(bf16), v_ref[...].astype(bf16),
                                                      preferred_element_type=f32)
        m_scr[...] = m_new

    @pl.when(kj == pl.num_programs(3) - 1)
    def _():
        o_ref[...] = acc_scr[...] / l_scr[...]


def _fox_call(q, k, v, cq, ck, t):
    bsz, nh, seq, hd = q.shape
    nb = seq // t
    kern = functools.partial(_fox_kernel, t=t)
    return pl.pallas_call(
        kern,
        grid=(bsz, nh, nb, nb),
        in_specs=[pl.BlockSpec((None, None, t, hd), lambda b, h, i, j: (b, h, i, 0)),
                  pl.BlockSpec((None, None, t, hd), lambda b, h, i, j: (b, h, jnp.minimum(i, j), 0)),
                  pl.BlockSpec((None, None, t, hd), lambda b, h, i, j: (b, h, jnp.minimum(i, j), 0)),
                  pl.BlockSpec((None, None, t, 1), lambda b, h, i, j: (b, h, i, 0)),
                  pl.BlockSpec((None, None, 1, t), lambda b, h, i, j: (b, h, 0, jnp.minimum(i, j)))],
        out_specs=pl.BlockSpec((None, None, t, hd), lambda b, h, i, j: (b, h, i, 0)),
        out_shape=jax.ShapeDtypeStruct((bsz, nh, seq, hd), f32),
        scratch_shapes=[pltpu.VMEM((t, 1), f32), pltpu.VMEM((t, 1), f32),
                        pltpu.VMEM((t, hd), f32)],
        compiler_params=_params(("parallel", "parallel", "parallel", "arbitrary")),
        name="fox",
    )(q, k, v, cq, ck)


def _outproj_kernel(oa_ref, ob_ref, x_ref, g1_ref, ga_ref, gb_ref, wa_ref, wb_ref,
                    gf_ref, sc_ref, sh_ref, wq_ref, x1_ref, h2_ref, qp_ref):
    oa = oa_ref[...]
    ob = ob_ref[...]
    na = oa * lax.rsqrt(jnp.mean(oa * oa, axis=-1, keepdims=True) + EPS) * ga_ref[...]
    nb = ob * lax.rsqrt(jnp.mean(ob * ob, axis=-1, keepdims=True) + EPS) * gb_ref[...]
    res = (jnp.dot(na.astype(bf16), wa_ref[...], preferred_element_type=f32)
           + jnp.dot(nb.astype(bf16), wb_ref[...], preferred_element_type=f32))
    x1 = x_ref[...] + g1_ref[0] * res
    x1_ref[...] = x1
    h2 = x1 * lax.rsqrt(jnp.mean(x1 * x1, axis=-1, keepdims=True) + EPS) * gf_ref[...]
    h2 = h2 * (1.0 + sc_ref[0]) + sh_ref[0]
    h2_ref[...] = h2
    qp_ref[...] = jnp.dot(h2.astype(bf16), wq_ref[...], preferred_element_type=f32)


def _outproj_call(oa, ob, x2, g1, ga, gb, wa, wb, gf, sc, sh, wq, seq, tm):
    n, d = x2.shape
    per_b = seq // tm
    tok = lambda i: (i, 0)
    bat = lambda i: (i // per_b, 0, 0)
    cst = lambda i: (0, 0)
    nq = wq.shape[1]
    return pl.pallas_call(
        _outproj_kernel,
        grid=(n // tm,),
        in_specs=[pl.BlockSpec((tm, WIDTH_A), tok), pl.BlockSpec((tm, WIDTH_B), tok),
                  pl.BlockSpec((tm, d), tok), pl.BlockSpec((1, 1, d), bat),
                  pl.BlockSpec((1, WIDTH_A), cst), pl.BlockSpec((1, WIDTH_B), cst),
                  pl.BlockSpec((WIDTH_A, d), cst), pl.BlockSpec((WIDTH_B, d), cst),
                  pl.BlockSpec((1, d), cst), pl.BlockSpec((1, 1, d), bat),
                  pl.BlockSpec((1, 1, d), bat), pl.BlockSpec((d, nq), cst)],
        out_specs=[pl.BlockSpec((tm, d), tok), pl.BlockSpec((tm, d), tok),
                   pl.BlockSpec((tm, nq), tok)],
        out_shape=[jax.ShapeDtypeStruct((n, d), f32), jax.ShapeDtypeStruct((n, d), f32),
                   jax.ShapeDtypeStruct((n, nq), f32)],
        compiler_params=_params(("parallel",)),
        name="outproj",
    )(oa, ob, x2, g1, ga, gb, wa, wb, gf, sc, sh, wq)


def _topk_rows(x, kk):
    r = x.shape[0]
    iota = lax.broadcasted_iota(i32, x.shape, 0)
    vals, idxs = [], []
    for _ in range(kk):
        m = jnp.max(x, axis=0, keepdims=True)
        pos = jnp.min(jnp.where(x == m, iota, r), axis=0, keepdims=True)
        vals.append(m)
        idxs.append(pos)
        x = jnp.where(iota == pos, -jnp.inf, x)
    return jnp.concatenate(vals, axis=0), jnp.concatenate(idxs, axis=0)


def _route_kernel(qp_ref, k1_ref, k2_ref, ids_ref, g_ref):
    half = PEER_KEY_DIM // 2
    q = qp_ref[...]
    t = q.shape[0]
    hp = lax.Precision.HIGHEST
    s1 = lax.dot_general(k1_ref[...], q[:, :half], NT_DIMS, preferred_element_type=f32, precision=hp)
    s2 = lax.dot_general(k2_ref[...], q[:, half:], NT_DIMS, preferred_element_type=f32, precision=hp)
    v1, i1 = _topk_rows(s1, PEER_TOPK)
    v2, i2 = _topk_rows(s2, PEER_TOPK)
    cand = (v1[:, None, :] + v2[None, :, :]).reshape(PEER_TOPK * PEER_TOPK, t)
    cand_i = (i1[:, None, :] * N_KEYS + i2[None, :, :]).reshape(PEER_TOPK * PEER_TOPK, t)
    iota = lax.broadcasted_iota(i32, cand.shape, 0)
    best, experts = [], []
    for _ in range(PEER_TOPK):
        m = jnp.max(cand, axis=0, keepdims=True)
        pos = jnp.min(jnp.where(cand == m, iota, PEER_TOPK * PEER_TOPK), axis=0, keepdims=True)
        hit = iota == pos
        experts.append(jnp.sum(jnp.where(hit, cand_i, 0), axis=0, keepdims=True))
        best.append(m)
        cand = jnp.where(hit, -jnp.inf, cand)
    best = jnp.concatenate(best, axis=0)
    e = jnp.exp(best - best[0:1])
    g_ref[...] = e / jnp.sum(e, axis=0, keepdims=True)
    ids_ref[...] = jnp.concatenate(experts, axis=0)


def _route_call(qp, k1, k2, tr):
    n = qp.shape[0]
    return pl.pallas_call(
        _route_kernel,
        grid=(n // tr, PEER_HEADS),
        in_specs=[pl.BlockSpec((tr, PEER_KEY_DIM), lambda i, h: (i, h)),
                  pl.BlockSpec(k1.shape, lambda i, h: (0, 0)),
                  pl.BlockSpec(k2.shape, lambda i, h: (0, 0))],
        out_specs=[pl.BlockSpec((PEER_TOPK, tr), lambda i, h: (h, i)),
                   pl.BlockSpec((PEER_TOPK, tr), lambda i, h: (h, i))],
        out_shape=[jax.ShapeDtypeStruct((PEER_SLOTS, n), i32),
                   jax.ShapeDtypeStruct((PEER_SLOTS, n), f32)],
        compiler_params=_params(("parallel", "arbitrary")),
        name="route",
    )(qp, k1, k2)


ROWS_PER_EXPERT = 4
TILE_STRIDE = 129


def _pack_table(tab):
    e, d = tab.shape
    bits = lax.bitcast_convert_type(tab.astype(bf16), jnp.uint16).astype(jnp.uint32)
    word = bits[:, :d // 2] | (bits[:, d // 2:] << 16)
    return lax.bitcast_convert_type(word, i32).reshape(e * ROWS_PER_EXPERT, LANES)


def _unpack(word):
    lo = lax.bitcast_convert_type(word << 16, f32)
    hi = lax.bitcast_convert_type(word & jnp.int32(-65536), f32)
    return lo, hi


def _gather_rows(tab_ref, e):
    return tab_ref[pl.ds(pl.multiple_of(e * ROWS_PER_EXPERT, ROWS_PER_EXPERT), ROWS_PER_EXPERT), :]


def _peer_u_kernel(ids_ref, h_ref, g_ref, u_ref, o_ref, tile_scr, q_scr, *, tt):
    def token(t, carry):
        h = h_ref[t]
        hlo, hhi = h[0:4], h[4:8]
        for k in range(PEER_SLOTS):
            lo, hi = _unpack(_gather_rows(u_ref, ids_ref[t, k]))
            tile_scr[pl.ds(k, ROWS_PER_EXPERT, stride=TILE_STRIDE), :] = lo * hlo + hi * hhi
        q = tile_scr[0:PEER_SLOTS]
        for j in range(1, ROWS_PER_EXPERT):
            q = q + tile_scr[j * TILE_STRIDE:j * TILE_STRIDE + PEER_SLOTS]
        q_scr[t] = q
        return carry

    lax.fori_loop(0, tt, token, 0)
    ones = jnp.ones((8, LANES), f32)
    sums = lax.dot_general(ones, q_scr[...].reshape(tt * PEER_SLOTS, LANES), NT_DIMS,
                           preferred_element_type=f32, precision=lax.Precision.HIGHEST)
    for t in range(tt):
        act = sums[0:1, t * PEER_SLOTS:(t + 1) * PEER_SLOTS]
        gelu = 0.5 * act * (1.0 + lax.erf(act * (2.0 ** -0.5)))
        o_ref[t:t + 1, :] = g_ref[t:t + 1, :] * gelu


def _peer_u_call(ids, h3, g, u_packed, tt):
    n = ids.shape[0]
    kern = functools.partial(_peer_u_kernel, tt=tt)
    return pl.pallas_call(
        kern,
        grid=(n // tt,),
        in_specs=[pl.BlockSpec((tt, PEER_SLOTS), lambda i: (i, 0), memory_space=pltpu.SMEM),
                  pl.BlockSpec((tt, 8, LANES), lambda i: (i, 0, 0)),
                  pl.BlockSpec((tt, PEER_SLOTS), lambda i: (i, 0)),
                  pl.BlockSpec(u_packed.shape, lambda i: (0, 0), pipeline_mode=pl.Buffered(1))],
        out_specs=pl.BlockSpec((tt, PEER_SLOTS), lambda i: (i, 0)),
        out_shape=jax.ShapeDtypeStruct((n, PEER_SLOTS), f32),
        scratch_shapes=[pltpu.VMEM((ROWS_PER_EXPERT * TILE_STRIDE + 8, LANES), f32),
                        pltpu.VMEM((tt, PEER_SLOTS, LANES), f32)],
        compiler_params=_params(("arbitrary",)),
        name="peer_u",
    )(ids, h3, g, u_packed)


def _peer_v_kernel(ids_ref, w_ref, v_ref, o_ref, *, tt):
    def token(t, carry):
        nacc = 4
        acc_lo = [jnp.zeros((ROWS_PER_EXPERT, LANES), f32) for _ in range(nacc)]
        acc_hi = [jnp.zeros((ROWS_PER_EXPERT, LANES), f32) for _ in range(nacc)]
        for k in range(PEER_SLOTS):
            lo, hi = _unpack(_gather_rows(v_ref, ids_ref[t, k]))
            w = w_ref[t, k]
            acc_lo[k % nacc] = acc_lo[k % nacc] + w * lo
            acc_hi[k % nacc] = acc_hi[k % nacc] + w * hi
        lo = (acc_lo[0] + acc_lo[1]) + (acc_lo[2] + acc_lo[3])
        hi = (acc_hi[0] + acc_hi[1]) + (acc_hi[2] + acc_hi[3])
        o_ref[t] = jnp.concatenate([lo, hi], axis=0)
        return carry

    lax.fori_loop(0, tt, token, 0)


def _peer_v_call(ids, wgt, v_packed, tt):
    n = ids.shape[0]
    kern = functools.partial(_peer_v_kernel, tt=tt)
    return pl.pallas_call(
        kern,
        grid=(n // tt,),
        in_specs=[pl.BlockSpec((tt, PEER_SLOTS), lambda i: (i, 0), memory_space=pltpu.SMEM),
                  pl.BlockSpec((tt, PEER_SLOTS), lambda i: (i, 0), memory_space=pltpu.SMEM),
                  pl.BlockSpec(v_packed.shape, lambda i: (0, 0), pipeline_mode=pl.Buffered(1))],
        out_specs=pl.BlockSpec((tt, 8, LANES), lambda i: (i, 0, 0)),
        out_shape=jax.ShapeDtypeStruct((n, 8, LANES), f32),
        compiler_params=_params(("arbitrary",)),
        name="peer_v",
    )(ids, wgt, v_packed)


def _final_kernel(x1_ref, p_ref, g2_ref, gf_ref, o_ref):
    x2 = x1_ref[...] + g2_ref[0] * p_ref[...]
    o_ref[...] = x2 * lax.rsqrt(jnp.mean(x2 * x2, axis=-1, keepdims=True) + EPS) * gf_ref[...]


def _final_call(x1, peer, g2, gf, seq, tm):
    n, d = x1.shape
    per_b = seq // tm
    tok = lambda i: (i, 0)
    return pl.pallas_call(
        _final_kernel,
        grid=(n // tm,),
        in_specs=[pl.BlockSpec((tm, d), tok), pl.BlockSpec((tm, d), tok),
                  pl.BlockSpec((1, 1, d), lambda i: (i // per_b, 0, 0)),
                  pl.BlockSpec((1, d), lambda i: (0, 0))],
        out_specs=pl.BlockSpec((tm, d), tok),
        out_shape=jax.ShapeDtypeStruct((n, d), f32),
        compiler_params=_params(("parallel",)),
        name="final",
    )(x1, peer, g2, gf)


def _rope_tables(positions):
    def tab(rot):
        inv = ROPE_THETA ** (-jnp.arange(0, rot, 2, dtype=f32) / rot)
        ang = positions.astype(f32)[..., None] * inv
        return jnp.cos(ang), jnp.sin(ang)

    cos_a, sin_a = tab(ROPE_DIM_A)
    cos_i, sin_i = tab(IDX_ROPE_DIM)
    b, s = positions.shape
    n = b * s
    c_half1 = jnp.tile(cos_a, (1, 1, N_HEADS_A)).reshape(n, 64)
    s_half1 = jnp.tile(sin_a, (1, 1, N_HEADS_A)).reshape(n, 64)
    c1 = jnp.concatenate([c_half1, c_half1], axis=1)
    s1 = jnp.concatenate([-s_half1, s_half1], axis=1)
    pad_c = jnp.ones((n, 20), f32)
    pad_s = jnp.zeros((n, 20), f32)
    c_half2 = jnp.concatenate([jnp.tile(cos_i, (1, 1, IDX_HEADS)).reshape(n, 32),
                               cos_a.reshape(n, 8), cos_i.reshape(n, 4), pad_c], axis=1)
    s_half2 = jnp.concatenate([jnp.tile(sin_i, (1, 1, IDX_HEADS)).reshape(n, 32),
                               sin_a.reshape(n, 8), sin_i.reshape(n, 4), pad_s], axis=1)
    c2 = jnp.concatenate([c_half2, c_half2], axis=1)
    s2 = jnp.concatenate([-s_half2, s_half2], axis=1)
    return c1, s1, c2, s2


def kernel(x, c, positions, w_ada, b_ada, g_mix, w_in, g_kv, w_uk, w_uv, b_forget,
           g_out_a, g_out_b, w_out, g_ffn, w_peer_q, peer_keys1, peer_keys2, peer_u,
           peer_v, g_final):
    b, s, d = x.shape
    n = b * s
    assert w_ada.shape[0] == 1, "single layer supported"
    tm = min(512, s)
    x2 = x.reshape(n, d)

    mod = _mod_call(c, w_ada[0], b_ada[0])
    mod = mod.reshape(b, 6, 1, d)
    shift1, scale1, gate1, shift2, scale2, gate2 = [mod[:, j] for j in range(6)]

    perm = _in_perm()
    w_in_r = jnp.where((perm >= 0)[None, :], w_in[0][:, np.maximum(perm, 0)], 0.0).astype(bf16)
    c1, s1, c2, s2 = _rope_tables(positions)
    proj = _inproj_call(x2, scale1, shift1, g_mix[0].reshape(1, d), w_in_r, c1, s1, c2, s2,
                        g_kv[0].reshape(1, KV_RANK), s, tm)

    r1 = proj[:, C_R1:C_R1 + LANES]
    r2 = proj[:, C_R2:C_R2 + LANES]
    misc = proj[:, C_MISC:C_MISC + 256]

    def heads(a):
        return a.reshape(b, s, a.shape[1], a.shape[2]).transpose(0, 2, 1, 3)

    qn = heads(proj[:, C_QN:C_LAT].reshape(n, N_HEADS_A, NOPE_DIM_A))
    qr = heads(jnp.concatenate([r1[:, :64].reshape(n, N_HEADS_A, 8),
                                r1[:, 64:].reshape(n, N_HEADS_A, 8)], axis=-1))
    iq = heads(jnp.concatenate([r2[:, 0:32].reshape(n, IDX_HEADS, 4),
                                r2[:, 64:96].reshape(n, IDX_HEADS, 4),
                                misc[:, 0:192].reshape(n, IDX_HEADS, 24)], axis=-1))
    ik = jnp.concatenate([r2[:, 40:44], r2[:, 104:108], misc[:, 192:216]], axis=-1).reshape(b, s, IDX_DIM)
    kr = jnp.concatenate([r2[:, 32:40], r2[:, 96:104]], axis=-1).reshape(b, s, ROPE_DIM_A)
    iw = misc[:, 216:224].reshape(b, s, IDX_HEADS)
    lat = proj[:, C_LAT:C_LAT + KV_RANK].reshape(b, s, KV_RANK)
    wuk = w_uk[0].transpose(1, 2, 0).astype(bf16)
    wuv = w_uv[0].transpose(1, 0, 2).astype(bf16)
    o_a = _dsa_call(qn, qr, iq, iw, lat, kr, ik, wuk, wuv, tq=128, tk=min(512, s))
    o_a = o_a.reshape(n, WIDTH_A)

    qb = heads(proj[:, C_QB:C_KB].reshape(n, N_HEADS_B, HEAD_DIM_B))
    kb = heads(proj[:, C_KB:C_VB].reshape(n, N_HEADS_B, HEAD_DIM_B))
    vb = heads(proj[:, C_VB:C_TOTAL].reshape(n, N_HEADS_B, HEAD_DIM_B))
    fl = misc[:, 224:232].reshape(b, s, N_HEADS_B) + b_forget[0]
    cum = jnp.cumsum(jax.nn.log_sigmoid(fl), axis=1).transpose(0, 2, 1)
    o_b = _fox_call(qb, kb, vb, cum[..., None], cum[:, :, None, :], t=min(512, s))
    o_b = o_b.transpose(0, 2, 1, 3).reshape(n, WIDTH_B)

    wo = w_out[0].astype(bf16)
    x1, h2, qp = _outproj_call(o_a, o_b, x2, gate1, g_out_a[0].reshape(1, WIDTH_A),
                               g_out_b[0].reshape(1, WIDTH_B), wo[:WIDTH_A], wo[WIDTH_A:],
                               g_ffn[0].reshape(1, d), scale2, shift2,
                               w_peer_q[0].astype(bf16), s, tm)

    ids_t, g_t = _route_call(qp, peer_keys1[0], peer_keys2[0], tr=min(256, n))
    ids = ids_t.T
    gates = g_t.T
    tt = 32
    wgt = _peer_u_call(ids, h2.reshape(n, 8, LANES), gates, _pack_table(peer_u[0]), tt)
    peer = _peer_v_call(ids, wgt, _pack_table(peer_v[0]), tt).reshape(n, d)

    out = _final_call(x1, peer, gate2, g_final.reshape(1, d), s, tm)
    return out.reshape(b, s, d)
```

```python
import functools

import jax
import jax.numpy as jnp
import numpy as np
from jax import lax
from jax.experimental import pallas as pl
from jax.experimental.pallas import tpu as pltpu

f32 = jnp.float32
bf16 = jnp.bfloat16
i32 = jnp.int32

N_HEADS_A = 8
HEAD_DIM_A = 64
ROPE_DIM_A = 16
NOPE_DIM_A = HEAD_DIM_A - ROPE_DIM_A
KV_RANK = 128
IDX_HEADS = 8
IDX_DIM = 32
IDX_ROPE_DIM = 8
TOPK_MAX = 256
N_HEADS_B = 8
HEAD_DIM_B = 64
WIDTH_A = N_HEADS_A * HEAD_DIM_A
WIDTH_B = N_HEADS_B * HEAD_DIM_B
ROPE_THETA = 500000.0
PEER_HEADS = 8
N_KEYS = 128
PEER_KEY_DIM = 128
PEER_TOPK = 16
PEER_SLOTS = PEER_HEADS * PEER_TOPK
EPS = 1e-6
IN_SIZES = (WIDTH_A, KV_RANK, ROPE_DIM_A, IDX_HEADS * IDX_DIM, IDX_DIM, IDX_HEADS,
            WIDTH_B, WIDTH_B, WIDTH_B, N_HEADS_B)

LANES = 128
NEG_BIG = -1e30
INT_MIN = -2147483648
VMEM_LIMIT = 56 * 1024 * 1024

C_R1 = 0
C_R2 = 128
C_QN = 256
C_LAT = 640
C_MISC = 768
C_QB = 1024
C_KB = 1536
C_VB = 2048
C_TOTAL = 2560

NT_DIMS = (((1,), (1,)), ((), ()))


def _in_perm():
    offs = np.cumsum((0,) + IN_SIZES)
    o_qa, o_lat, o_kr, o_iq, o_ik, o_iw, o_qb, o_kb, o_vb, o_fb = offs[:10]
    perm = -np.ones((C_TOTAL,), np.int64)
    ha, hi = ROPE_DIM_A // 2, IDX_ROPE_DIM // 2
    for h in range(N_HEADS_A):
        for j in range(ha):
            perm[C_R1 + h * ha + j] = o_qa + h * HEAD_DIM_A + j
            perm[C_R1 + 64 + h * ha + j] = o_qa + h * HEAD_DIM_A + ha + j
        for j in range(NOPE_DIM_A):
            perm[C_QN + h * NOPE_DIM_A + j] = o_qa + h * HEAD_DIM_A + ROPE_DIM_A + j
    for h in range(IDX_HEADS):
        for j in range(hi):
            perm[C_R2 + h * hi + j] = o_iq + h * IDX_DIM + j
            perm[C_R2 + 64 + h * hi + j] = o_iq + h * IDX_DIM + hi + j
        for j in range(IDX_DIM - IDX_ROPE_DIM):
            perm[C_MISC + h * 24 + j] = o_iq + h * IDX_DIM + IDX_ROPE_DIM + j
    for j in range(ha):
        perm[C_R2 + 32 + j] = o_kr + j
        perm[C_R2 + 64 + 32 + j] = o_kr + ha + j
    for j in range(hi):
        perm[C_R2 + 40 + j] = o_ik + j
        perm[C_R2 + 64 + 40 + j] = o_ik + hi + j
    for j in range(IDX_DIM - IDX_ROPE_DIM):
        perm[C_MISC + 192 + j] = o_ik + IDX_ROPE_DIM + j
    for j in range(IDX_HEADS):
        perm[C_MISC + 216 + j] = o_iw + j
    for j in range(N_HEADS_B):
        perm[C_MISC + 224 + j] = o_fb + j
    perm[C_LAT:C_LAT + KV_RANK] = o_lat + np.arange(KV_RANK)
    perm[C_QB:C_QB + WIDTH_B] = o_qb + np.arange(WIDTH_B)
    perm[C_KB:C_KB + WIDTH_B] = o_kb + np.arange(WIDTH_B)
    perm[C_VB:C_VB + WIDTH_B] = o_vb + np.arange(WIDTH_B)
    return perm


def _params(sem):
    return pltpu.CompilerParams(dimension_semantics=sem, vmem_limit_bytes=VMEM_LIMIT)


def _mod_kernel(c_ref, w_ref, b_ref, o_ref):
    c = c_ref[...]
    ca = c * jax.nn.sigmoid(c)
    o_ref[...] = jnp.dot(ca, w_ref[...], preferred_element_type=f32,
                         precision=lax.Precision.HIGHEST) + b_ref[...]


def _mod_call(c, w, b):
    bsz, d = c.shape
    n = w.shape[1]
    return pl.pallas_call(
        _mod_kernel,
        grid=(n // d,),
        in_specs=[pl.BlockSpec((bsz, d), lambda j: (0, 0)),
                  pl.BlockSpec((d, d), lambda j: (0, j)),
                  pl.BlockSpec((1, d), lambda j: (0, j))],
        out_specs=pl.BlockSpec((bsz, d), lambda j: (0, j)),
        out_shape=jax.ShapeDtypeStruct((bsz, n), f32),
        compiler_params=_params(("arbitrary",)),
        name="mod",
    )(c, w, b.reshape(1, n))


def _inproj_kernel(x_ref, sc_ref, sh_ref, g_ref, w_ref, c1_ref, s1_ref, c2_ref, s2_ref,
                   gkv_ref, o_ref):
    x = x_ref[...]
    ms = jnp.mean(x * x, axis=-1, keepdims=True)
    h = x * lax.rsqrt(ms + EPS) * g_ref[...]
    h = h * (1.0 + sc_ref[0]) + sh_ref[0]
    p = jnp.dot(h.astype(bf16), w_ref[...], preferred_element_type=f32)
    r1 = p[:, C_R1:C_R1 + LANES]
    o_ref[:, C_R1:C_R1 + LANES] = r1 * c1_ref[...] + pltpu.roll(r1, 64, 1) * s1_ref[...]
    r2 = p[:, C_R2:C_R2 + LANES]
    o_ref[:, C_R2:C_R2 + LANES] = r2 * c2_ref[...] + pltpu.roll(r2, 64, 1) * s2_ref[...]
    o_ref[:, C_QN:C_LAT] = p[:, C_QN:C_LAT]
    lat = p[:, C_LAT:C_LAT + KV_RANK]
    lms = jnp.mean(lat * lat, axis=-1, keepdims=True)
    o_ref[:, C_LAT:C_LAT + KV_RANK] = lat * lax.rsqrt(lms + EPS) * gkv_ref[...]
    o_ref[:, C_MISC:] = p[:, C_MISC:]


def _inproj_call(x2, sc, sh, g, w, c1, s1, c2, s2, gkv, seq, tm):
    n, d = x2.shape
    per_b = seq // tm
    tok = lambda i: (i, 0)
    bat = lambda i: (i // per_b, 0, 0)
    cst = lambda i: (0, 0)
    return pl.pallas_call(
        _inproj_kernel,
        grid=(n // tm,),
        in_specs=[pl.BlockSpec((tm, d), tok),
                  pl.BlockSpec((1, 1, d), bat),
                  pl.BlockSpec((1, 1, d), bat),
                  pl.BlockSpec((1, d), cst),
                  pl.BlockSpec((d, C_TOTAL), cst),
                  pl.BlockSpec((tm, LANES), tok),
                  pl.BlockSpec((tm, LANES), tok),
                  pl.BlockSpec((tm, LANES), tok),
                  pl.BlockSpec((tm, LANES), tok),
                  pl.BlockSpec((1, KV_RANK), cst)],
        out_specs=pl.BlockSpec((tm, C_TOTAL), tok),
        out_shape=jax.ShapeDtypeStruct((n, C_TOTAL), f32),
        compiler_params=_params(("parallel",)),
        name="inproj",
    )(x2, sc, sh, g, w, c1, s1, c2, s2, gkv)


def _dsa_kernel(qn_ref, qr_ref, iq_ref, iw_ref, lat_ref, kr_ref, ik_ref, wuk_ref, wuv_ref,
                o_ref, key_scr, m_scr, l_scr, acc_scr, *, tq, tk, topk, nbits):
    nh = N_HEADS_A
    qi = pl.program_id(1)
    nc = ((qi + 1) * tq + tk - 1) // tk
    row = qi * tq + lax.broadcasted_iota(i32, (tq, tk), 0)
    lane = lax.broadcasted_iota(i32, (tq, tk), 1)

    iq_all = iq_ref[...].reshape(IDX_HEADS * tq, IDX_DIM).astype(bf16)
    iw = iw_ref[...] * (IDX_DIM ** -0.5 * IDX_HEADS ** -0.5)

    def score_chunk(c, carry):
        start = pl.multiple_of(c * tk, tk)
        ikc = ik_ref[pl.ds(start, tk), :].astype(bf16)
        d = lax.dot_general(iq_all, ikc, NT_DIMS, preferred_element_type=f32)
        sc = jnp.zeros((tq, tk), f32)
        for h in range(IDX_HEADS):
            sc = sc + jnp.maximum(d[h * tq:(h + 1) * tq], 0.0) * iw[:, h:h + 1]
        sc = jnp.where(start + lane <= row, sc, -jnp.inf)
        bits = lax.bitcast_convert_type(sc, i32)
        key_scr[c] = bits ^ ((bits >> 31) & 0x7FFFFFFF)
        return carry

    lax.fori_loop(0, nc, score_chunk, 0)

    def count(pred):
        def body(c, acc):
            m = pred(key_scr[c], c * tk + lane)
            part = m[:, 0:LANES]
            for j in range(1, tk // LANES):
                part = part + m[:, j * LANES:(j + 1) * LANES]
            return acc + part
        acc = lax.fori_loop(0, nc, body, jnp.zeros((tq, LANES), i32))
        return jnp.sum(acc, axis=1, keepdims=True)

    def bit_body(i, thr):
        cand = thr + lax.shift_left(jnp.int32(1), 31 - i)
        cnt = count(lambda k, col: jnp.where(k >= cand, 1, 0))
        return jnp.where(cnt >= topk, cand, thr)

    thr = lax.fori_loop(0, 32, bit_body, jnp.full((tq, 1), INT_MIN, i32))
    need = topk - count(lambda k, col: jnp.where(k > thr, 1, 0))

    def tie_body(i, last):
        cand = last + lax.shift_left(jnp.int32(1), nbits - 1 - i)
        below = count(lambda k, col: jnp.where(k == thr, jnp.where(col < cand, 1, 0), 0))
        return jnp.where(below < need, cand, last)

    last = lax.fori_loop(0, nbits, tie_body, jnp.zeros((tq, 1), i32))

    scale = HEAD_DIM_A ** -0.5
    qls = []
    for h in range(nh):
        qls.append(jnp.dot(qn_ref[h].astype(bf16), wuk_ref[h], preferred_element_type=f32))
    q_lat = (jnp.concatenate(qls, axis=0) * scale).astype(bf16)
    q_rope = (qr_ref[...].reshape(nh * tq, ROPE_DIM_A) * scale).astype(bf16)

    m_scr[...] = jnp.full(m_scr.shape, NEG_BIG, f32)
    l_scr[...] = jnp.zeros(l_scr.shape, f32)
    acc_scr[...] = jnp.zeros(acc_scr.shape, f32)

    def attend_chunk(c, carry):
        start = pl.multiple_of(c * tk, tk)
        latc = lat_ref[pl.ds(start, tk), :].astype(bf16)
        krc = kr_ref[pl.ds(start, tk), :].astype(bf16)
        s = (lax.dot_general(q_lat, latc, NT_DIMS, preferred_element_type=f32)
             + lax.dot_general(q_rope, krc, NT_DIMS, preferred_element_type=f32))
        k = key_scr[c]
        col = start + lane
        tie = jnp.where(k == thr, jnp.where(col <= last, 0.0, NEG_BIG), NEG_BIG)
        bias = jnp.where(col <= row, jnp.where(k > thr, 0.0, tie), NEG_BIG)
        s = (s.reshape(nh, tq, tk) + bias[None]).reshape(nh * tq, tk)
        m_old = m_scr[...]
        m_new = jnp.maximum(m_old, jnp.max(s, axis=1, keepdims=True))
        p = jnp.exp(s - m_new)
        alpha = jnp.exp(m_old - m_new)
        l_scr[...] = alpha * l_scr[...] + jnp.sum(p, axis=1, keepdims=True)
        acc_scr[...] = alpha * acc_scr[...] + jnp.dot(p.astype(bf16), latc,
                                                      preferred_element_type=f32)
        m_scr[...] = m_new
        return carry

    lax.fori_loop(0, nc, attend_chunk, 0)

    o = (acc_scr[...] / l_scr[...]).astype(bf16)
    outs = []
    for h in range(nh):
        outs.append(jnp.dot(o[h * tq:(h + 1) * tq], wuv_ref[h], preferred_element_type=f32))
    o_ref[...] = jnp.concatenate(outs, axis=1)


def _dsa_call(qn, qr, iq, iw, lat, kr, ik, wuk, wuv, tq, tk):
    bsz, nh, seq, _ = qn.shape
    topk = min(TOPK_MAX, seq // 4)
    nbits = max(1, (seq - 1).bit_length())
    kern = functools.partial(_dsa_kernel, tq=tq, tk=tk, topk=topk, nbits=nbits)
    qblk = lambda last: pl.BlockSpec((None, nh, tq, last), lambda b, i: (b, 0, i, 0))
    kblk = lambda last: pl.BlockSpec((None, seq, last), lambda b, i: (b, 0, 0))
    return pl.pallas_call(
        kern,
        grid=(bsz, seq // tq),
        in_specs=[qblk(NOPE_DIM_A), qblk(ROPE_DIM_A), qblk(IDX_DIM),
                  pl.BlockSpec((None, tq, IDX_HEADS), lambda b, i: (b, i, 0)),
                  kblk(KV_RANK), kblk(ROPE_DIM_A), kblk(IDX_DIM),
                  pl.BlockSpec(wuk.shape, lambda b, i: (0, 0, 0)),
                  pl.BlockSpec(wuv.shape, lambda b, i: (0, 0, 0))],
        out_specs=pl.BlockSpec((None, tq, WIDTH_A), lambda b, i: (b, i, 0)),
        out_shape=jax.ShapeDtypeStruct((bsz, seq, WIDTH_A), f32),
        scratch_shapes=[pltpu.VMEM((seq // tk, tq, tk), i32),
                        pltpu.VMEM((nh * tq, 1), f32),
                        pltpu.VMEM((nh * tq, 1), f32),
                        pltpu.VMEM((nh * tq, KV_RANK), f32)],
        compiler_params=_params(("parallel", "arbitrary")),
        name="dsa",
    )(qn, qr, iq, iw, lat, kr, ik, wuk, wuv)


def _fox_kernel(q_ref, k_ref, v_ref, cq_ref, ck_ref, o_ref, m_scr, l_scr, acc_scr, *, t):
    qi = pl.program_id(2)
    kj = pl.program_id(3)

    @pl.when(kj == 0)
    def _():
        m_scr[...] = jnp.full(m_scr.shape, NEG_BIG, f32)
        l_scr[...] = jnp.zeros(l_scr.shape, f32)
        acc_scr[...] = jnp.zeros(acc_scr.shape, f32)

    @pl.when(kj <= qi)
    def _():
        q = (q_ref[...] * (HEAD_DIM_B ** -0.5)).astype(bf16)
        s = lax.dot_general(q, k_ref[...].astype(bf16), NT_DIMS, preferred_element_type=f32)
        s = s + cq_ref[...] - ck_ref[...]
        row = qi * t + lax.broadcasted_iota(i32, (t, t), 0)
        col = kj * t + lax.broadcasted_iota(i32, (t, t), 1)
        s = jnp.where(col <= row, s, NEG_BIG)
        m_old = m_scr[...]
        m_new = jnp.maximum(m_old, jnp.max(s, axis=1, keepdims=True))
        p = jnp.exp(s - m_new)
        alpha = jnp.exp(m_old - m_new)
        l_scr[...] = alpha * l_scr[...] + jnp.sum(p, axis=1, keepdims=True)
        acc_scr[...] = alpha * acc_scr[...] + jnp.dot(p.astype(bf16), v_ref[...].astype(bf16),
                                                      preferred_element_type=f32)
        m_scr[...] = m_new

    @pl.when(kj == pl.num_programs(3) - 1)
    def _():
        o_ref[...] = acc_scr[...] / l_scr[...]


def _fox_call(q, k, v, cq, ck, t):
    bsz, nh, seq, hd = q.shape
    nb = seq // t
    kern = functools.partial(_fox_kernel, t=t)
    return pl.pallas_call(
        kern,
        grid=(bsz, nh, nb, nb),
        in_specs=[pl.BlockSpec((None, None, t, hd), lambda b, h, i, j: (b, h, i, 0)),
                  pl.BlockSpec((None, None, t, hd), lambda b, h, i, j: (b, h, jnp.minimum(i, j), 0)),
                  pl.BlockSpec((None, None, t, hd), lambda b, h, i, j: (b, h, jnp.minimum(i, j), 0)),
                  pl.BlockSpec((None, None, t, 1), lambda b, h, i, j: (b, h, i, 0)),
                  pl.BlockSpec((None, None, 1, t), lambda b, h, i, j: (b, h, 0, jnp.minimum(i, j)))],
        out_specs=pl.BlockSpec((None, None, t, hd), lambda b, h, i, j: (b, h, i, 0)),
        out_shape=jax.ShapeDtypeStruct((bsz, nh, seq, hd), f32),
        scratch_shapes=[pltpu.VMEM((t, 1), f32), pltpu.VMEM((t, 1), f32),
                        pltpu.VMEM((t, hd), f32)],
        compiler_params=_params(("parallel", "parallel", "parallel", "arbitrary")),
        name="fox",
    )(q, k, v, cq, ck)


def _outproj_kernel(oa_ref, ob_ref, x_ref, g1_ref, ga_ref, gb_ref, wa_ref, wb_ref,
                    gf_ref, sc_ref, sh_ref, wq_ref, x1_ref, h2_ref, qp_ref):
    oa = oa_ref[...]
    ob = ob_ref[...]
    na = oa * lax.rsqrt(jnp.mean(oa * oa, axis=-1, keepdims=True) + EPS) * ga_ref[...]
    nb = ob * lax.rsqrt(jnp.mean(ob * ob, axis=-1, keepdims=True) + EPS) * gb_ref[...]
    res = (jnp.dot(na.astype(bf16), wa_ref[...], preferred_element_type=f32)
           + jnp.dot(nb.astype(bf16), wb_ref[...], preferred_element_type=f32))
    x1 = x_ref[...] + g1_ref[0] * res
    x1_ref[...] = x1
    h2 = x1 * lax.rsqrt(jnp.mean(x1 * x1, axis=-1, keepdims=True) + EPS) * gf_ref[...]
    h2 = h2 * (1.0 + sc_ref[0]) + sh_ref[0]
    h2_ref[...] = h2
    qp_ref[...] = jnp.dot(h2.astype(bf16), wq_ref[...], preferred_element_type=f32)


def _outproj_call(oa, ob, x2, g1, ga, gb, wa, wb, gf, sc, sh, wq, seq, tm):
    n, d = x2.shape
    per_b = seq // tm
    tok = lambda i: (i, 0)
    bat = lambda i: (i // per_b, 0, 0)
    cst = lambda i: (0, 0)
    nq = wq.shape[1]
    return pl.pallas_call(
        _outproj_kernel,
        grid=(n // tm,),
        in_specs=[pl.BlockSpec((tm, WIDTH_A), tok), pl.BlockSpec((tm, WIDTH_B), tok),
                  pl.BlockSpec((tm, d), tok), pl.BlockSpec((1, 1, d), bat),
                  pl.BlockSpec((1, WIDTH_A), cst), pl.BlockSpec((1, WIDTH_B), cst),
                  pl.BlockSpec((WIDTH_A, d), cst), pl.BlockSpec((WIDTH_B, d), cst),
                  pl.BlockSpec((1, d), cst), pl.BlockSpec((1, 1, d), bat),
                  pl.BlockSpec((1, 1, d), bat), pl.BlockSpec((d, nq), cst)],
        out_specs=[pl.BlockSpec((tm, d), tok), pl.BlockSpec((tm, d), tok),
                   pl.BlockSpec((tm, nq), tok)],
        out_shape=[jax.ShapeDtypeStruct((n, d), f32), jax.ShapeDtypeStruct((n, d), f32),
                   jax.ShapeDtypeStruct((n, nq), f32)],
        compiler_params=_params(("parallel",)),
        name="outproj",
    )(oa, ob, x2, g1, ga, gb, wa, wb, gf, sc, sh, wq)


def _topk_rows(x, kk):
    r = x.shape[0]
    iota = lax.broadcasted_iota(i32, x.shape, 0)
    vals, idxs = [], []
    for _ in range(kk):
        m = jnp.max(x, axis=0, keepdims=True)
        pos = jnp.min(jnp.where(x == m, iota, r), axis=0, keepdims=True)
        vals.append(m)
        idxs.append(pos)
        x = jnp.where(iota == pos, -jnp.inf, x)
    return jnp.concatenate(vals, axis=0), jnp.concatenate(idxs, axis=0)


def _route_kernel(qp_ref, k1_ref, k2_ref, ids_ref, g_ref):
    half = PEER_KEY_DIM // 2
    q = qp_ref[...]
    t = q.shape[0]
    hp = lax.Precision.HIGHEST
    s1 = lax.dot_general(k1_ref[...], q[:, :half], NT_DIMS, preferred_element_type=f32, precision=hp)
    s2 = lax.dot_general(k2_ref[...], q[:, half:], NT_DIMS, preferred_element_type=f32, precision=hp)
    v1, i1 = _topk_rows(s1, PEER_TOPK)
    v2, i2 = _topk_rows(s2, PEER_TOPK)
    cand = (v1[:, None, :] + v2[None, :, :]).reshape(PEER_TOPK * PEER_TOPK, t)
    cand_i = (i1[:, None, :] * N_KEYS + i2[None, :, :]).reshape(PEER_TOPK * PEER_TOPK, t)
    iota = lax.broadcasted_iota(i32, cand.shape, 0)
    best, experts = [], []
    for _ in range(PEER_TOPK):
        m = jnp.max(cand, axis=0, keepdims=True)
        pos = jnp.min(jnp.where(cand == m, iota, PEER_TOPK * PEER_TOPK), axis=0, keepdims=True)
        hit = iota == pos
        experts.append(jnp.sum(jnp.where(hit, cand_i, 0), axis=0, keepdims=True))
        best.append(m)
        cand = jnp.where(hit, -jnp.inf, cand)
    best = jnp.concatenate(best, axis=0)
    e = jnp.exp(best - best[0:1])
    g_ref[...] = e / jnp.sum(e, axis=0, keepdims=True)
    ids_ref[...] = jnp.concatenate(experts, axis=0)


def _route_call(qp, k1, k2, tr):
    n = qp.shape[0]
    return pl.pallas_call(
        _route_kernel,
        grid=(n // tr, PEER_HEADS),
        in_specs=[pl.BlockSpec((tr, PEER_KEY_DIM), lambda i, h: (i, h)),
                  pl.BlockSpec(k1.shape, lambda i, h: (0, 0)),
                  pl.BlockSpec(k2.shape, lambda i, h: (0, 0))],
        out_specs=[pl.BlockSpec((PEER_TOPK, tr), lambda i, h: (h, i)),
                   pl.BlockSpec((PEER_TOPK, tr), lambda i, h: (h, i))],
        out_shape=[jax.ShapeDtypeStruct((PEER_SLOTS, n), i32),
                   jax.ShapeDtypeStruct((PEER_SLOTS, n), f32)],
        compiler_params=_params(("parallel", "arbitrary")),
        name="route",
    )(qp, k1, k2)


ROWS_PER_EXPERT = 4
TILE_STRIDE = 136
TOK_UNROLL = 8
HALF = 512


def _pack_table(tab):
    e, d = tab.shape
    bits = lax.bitcast_convert_type(tab.astype(bf16), jnp.uint16).astype(jnp.uint32)
    word = bits[:, :d // 2] | (bits[:, d // 2:] << 16)
    return lax.bitcast_convert_type(word, i32).reshape(e * ROWS_PER_EXPERT, LANES)


def _gather_tile(tab_ref, rows_ref, t, tile_ref):
    tok_rows = rows_ref.at[t]
    for k in range(PEER_SLOTS):
        row = pl.multiple_of(tok_rows[k], ROWS_PER_EXPERT)
        tile_ref[pl.ds(k, ROWS_PER_EXPERT, stride=TILE_STRIDE), :] = tab_ref[pl.ds(row, ROWS_PER_EXPERT), :]
    chunks = [pltpu.bitcast(tile_ref[j * TILE_STRIDE:j * TILE_STRIDE + PEER_SLOTS, :], bf16)
              for j in range(ROWS_PER_EXPERT)]
    return jnp.concatenate(chunks, axis=1)


def _peer_u_kernel(ids_ref, h_ref, g_ref, u_ref, o_ref, tile_scr, act_scr, *, tt):
    even = lax.broadcasted_iota(i32, (1, 2 * PEER_SLOTS), 1) % 2 == 0

    def group(gi, carry):
        rows = []
        for tl in range(TOK_UNROLL):
            t = gi * TOK_UNROLL + tl
            r = _gather_tile(u_ref, ids_ref, t, tile_scr.at[tl % 2])
            h = h_ref[t]
            major = h.astype(bf16)
            minor = (h - major.astype(f32)).astype(bf16)
            out = lax.dot_general(jnp.concatenate([major, minor], axis=0), r, NT_DIMS,
                                  preferred_element_type=f32)
            rows.append(jnp.where(even, out[0:1] + out[2:3], out[1:2] + out[3:4]))
        act_scr[pl.ds(pl.multiple_of(gi * TOK_UNROLL, TOK_UNROLL), TOK_UNROLL), :] = (
            jnp.concatenate(rows, axis=0))
        return carry

    lax.fori_loop(0, tt // TOK_UNROLL, group, 0)
    part = act_scr[...]
    act = part + jnp.where(even, pltpu.roll(part, 2 * PEER_SLOTS - 1, 1), pltpu.roll(part, 1, 1))
    gelu = 0.5 * act * (1.0 + lax.erf(act * (2.0 ** -0.5)))
    o_ref[...] = g_ref[...] * gelu


def _peer_u_call(ids, h3, g2, u_packed, tt):
    n = ids.shape[0]
    kern = functools.partial(_peer_u_kernel, tt=tt)
    return pl.pallas_call(
        kern,
        grid=(n // tt,),
        in_specs=[pl.BlockSpec((tt, PEER_SLOTS), lambda i: (i, 0), memory_space=pltpu.SMEM),
                  pl.BlockSpec((tt, 2, HALF), lambda i: (i, 0, 0)),
                  pl.BlockSpec((tt, 2 * PEER_SLOTS), lambda i: (i, 0)),
                  pl.BlockSpec(u_packed.shape, lambda i: (0, 0), pipeline_mode=pl.Buffered(1))],
        out_specs=pl.BlockSpec((tt, 2 * PEER_SLOTS), lambda i: (i, 0)),
        out_shape=jax.ShapeDtypeStruct((n, 2 * PEER_SLOTS), f32),
        scratch_shapes=[pltpu.VMEM((2, ROWS_PER_EXPERT * TILE_STRIDE, LANES), i32),
                        pltpu.VMEM((tt, 2 * PEER_SLOTS), f32)],
        compiler_params=_params(("arbitrary",)),
        name="peer_u",
    )(ids, h3, g2, u_packed)


def _peer_v_kernel(ids_ref, w_ref, v_ref, o_ref, tile_scr, *, tt):
    even = lax.broadcasted_iota(i32, (1, 2 * PEER_SLOTS), 1) % 2 == 0

    def group(gi, carry):
        w8 = w_ref[pl.ds(pl.multiple_of(gi * TOK_UNROLL, TOK_UNROLL), TOK_UNROLL), :]
        for tl in range(TOK_UNROLL):
            t = gi * TOK_UNROLL + tl
            r = _gather_tile(v_ref, ids_ref, t, tile_scr.at[tl % 2])
            w = w8[tl:tl + 1]
            lhs = jnp.concatenate([jnp.where(even, w, 0.0), jnp.where(even, 0.0, w)], axis=0)
            o_ref[t] = jnp.dot(lhs.astype(bf16), r, preferred_element_type=f32)
        return carry

    lax.fori_loop(0, tt // TOK_UNROLL, group, 0)


def _peer_v_call(ids, wgt2, v_packed, tt):
    n = ids.shape[0]
    kern = functools.partial(_peer_v_kernel, tt=tt)
    return pl.pallas_call(
        kern,
        grid=(n // tt,),
        in_specs=[pl.BlockSpec((tt, PEER_SLOTS), lambda i: (i, 0), memory_space=pltpu.SMEM),
                  pl.BlockSpec((tt, 2 * PEER_SLOTS), lambda i: (i, 0)),
                  pl.BlockSpec(v_packed.shape, lambda i: (0, 0), pipeline_mode=pl.Buffered(1))],
        out_specs=pl.BlockSpec((tt, 2, HALF), lambda i: (i, 0, 0)),
        out_shape=jax.ShapeDtypeStruct((n, 2, HALF), f32),
        scratch_shapes=[pltpu.VMEM((2, ROWS_PER_EXPERT * TILE_STRIDE, LANES), i32)],
        compiler_params=_params(("arbitrary",)),
        name="peer_v",
    )(ids, wgt2, v_packed)


def _final_kernel(x1_ref, p_ref, g2_ref, gf_ref, o_ref):
    x2 = x1_ref[...] + g2_ref[0] * p_ref[...]
    o_ref[...] = x2 * lax.rsqrt(jnp.mean(x2 * x2, axis=-1, keepdims=True) + EPS) * gf_ref[...]


def _final_call(x1, peer, g2, gf, seq, tm):
    n, d = x1.shape
    per_b = seq // tm
    tok = lambda i: (i, 0)
    return pl.pallas_call(
        _final_kernel,
        grid=(n // tm,),
        in_specs=[pl.BlockSpec((tm, d), tok), pl.BlockSpec((tm, d), tok),
                  pl.BlockSpec((1, 1, d), lambda i: (i // per_b, 0, 0)),
                  pl.BlockSpec((1, d), lambda i: (0, 0))],
        out_specs=pl.BlockSpec((tm, d), tok),
        out_shape=jax.ShapeDtypeStruct((n, d), f32),
        compiler_params=_params(("parallel",)),
        name="final",
    )(x1, peer, g2, gf)


def _rope_tables(positions):
    def tab(rot):
        inv = ROPE_THETA ** (-jnp.arange(0, rot, 2, dtype=f32) / rot)
        ang = positions.astype(f32)[..., None] * inv
        return jnp.cos(ang), jnp.sin(ang)

    cos_a, sin_a = tab(ROPE_DIM_A)
    cos_i, sin_i = tab(IDX_ROPE_DIM)
    b, s = positions.shape
    n = b * s
    c_half1 = jnp.tile(cos_a, (1, 1, N_HEADS_A)).reshape(n, 64)
    s_half1 = jnp.tile(sin_a, (1, 1, N_HEADS_A)).reshape(n, 64)
    c1 = jnp.concatenate([c_half1, c_half1], axis=1)
    s1 = jnp.concatenate([-s_half1, s_half1], axis=1)
    pad_c = jnp.ones((n, 20), f32)
    pad_s = jnp.zeros((n, 20), f32)
    c_half2 = jnp.concatenate([jnp.tile(cos_i, (1, 1, IDX_HEADS)).reshape(n, 32),
                               cos_a.reshape(n, 8), cos_i.reshape(n, 4), pad_c], axis=1)
    s_half2 = jnp.concatenate([jnp.tile(sin_i, (1, 1, IDX_HEADS)).reshape(n, 32),
                               sin_a.reshape(n, 8), sin_i.reshape(n, 4), pad_s], axis=1)
    c2 = jnp.concatenate([c_half2, c_half2], axis=1)
    s2 = jnp.concatenate([-s_half2, s_half2], axis=1)
    return c1, s1, c2, s2


def kernel(x, c, positions, w_ada, b_ada, g_mix, w_in, g_kv, w_uk, w_uv, b_forget,
           g_out_a, g_out_b, w_out, g_ffn, w_peer_q, peer_keys1, peer_keys2, peer_u,
           peer_v, g_final):
    b, s, d = x.shape
    n = b * s
    assert w_ada.shape[0] == 1, "single layer supported"
    tm = min(512, s)
    x2 = x.reshape(n, d)

    mod = _mod_call(c, w_ada[0], b_ada[0])
    mod = mod.reshape(b, 6, 1, d)
    shift1, scale1, gate1, shift2, scale2, gate2 = [mod[:, j] for j in range(6)]

    perm = _in_perm()
    w_in_r = jnp.where((perm >= 0)[None, :], w_in[0][:, np.maximum(perm, 0)], 0.0).astype(bf16)
    c1, s1, c2, s2 = _rope_tables(positions)
    proj = _inproj_call(x2, scale1, shift1, g_mix[0].reshape(1, d), w_in_r, c1, s1, c2, s2,
                        g_kv[0].reshape(1, KV_RANK), s, tm)

    r1 = proj[:, C_R1:C_R1 + LANES]
    r2 = proj[:, C_R2:C_R2 + LANES]
    misc = proj[:, C_MISC:C_MISC + 256]

    def heads(a):
        return a.reshape(b, s, a.shape[1], a.shape[2]).transpose(0, 2, 1, 3)

    qn = heads(proj[:, C_QN:C_LAT].reshape(n, N_HEADS_A, NOPE_DIM_A))
    qr = heads(jnp.concatenate([r1[:, :64].reshape(n, N_HEADS_A, 8),
                                r1[:, 64:].reshape(n, N_HEADS_A, 8)], axis=-1))
    iq = heads(jnp.concatenate([r2[:, 0:32].reshape(n, IDX_HEADS, 4),
                                r2[:, 64:96].reshape(n, IDX_HEADS, 4),
                                misc[:, 0:192].reshape(n, IDX_HEADS, 24)], axis=-1))
    ik = jnp.concatenate([r2[:, 40:44], r2[:, 104:108], misc[:, 192:216]], axis=-1).reshape(b, s, IDX_DIM)
    kr = jnp.concatenate([r2[:, 32:40], r2[:, 96:104]], axis=-1).reshape(b, s, ROPE_DIM_A)
    iw = misc[:, 216:224].reshape(b, s, IDX_HEADS)
    lat = proj[:, C_LAT:C_LAT + KV_RANK].reshape(b, s, KV_RANK)
    wuk = w_uk[0].transpose(1, 2, 0).astype(bf16)
    wuv = w_uv[0].transpose(1, 0, 2).astype(bf16)
    o_a = _dsa_call(qn, qr, iq, iw, lat, kr, ik, wuk, wuv, tq=128, tk=min(512, s))
    o_a = o_a.reshape(n, WIDTH_A)

    qb = heads(proj[:, C_QB:C_KB].reshape(n, N_HEADS_B, HEAD_DIM_B))
    kb = heads(proj[:, C_KB:C_VB].reshape(n, N_HEADS_B, HEAD_DIM_B))
    vb = heads(proj[:, C_VB:C_TOTAL].reshape(n, N_HEADS_B, HEAD_DIM_B))
    fl = misc[:, 224:232].reshape(b, s, N_HEADS_B) + b_forget[0]
    cum = jnp.cumsum(jax.nn.log_sigmoid(fl), axis=1).transpose(0, 2, 1)
    o_b = _fox_call(qb, kb, vb, cum[..., None], cum[:, :, None, :], t=min(512, s))
    o_b = o_b.transpose(0, 2, 1, 3).reshape(n, WIDTH_B)

    wo = w_out[0].astype(bf16)
    x1, h2, qp = _outproj_call(o_a, o_b, x2, gate1, g_out_a[0].reshape(1, WIDTH_A),
                               g_out_b[0].reshape(1, WIDTH_B), wo[:WIDTH_A], wo[WIDTH_A:],
                               g_ffn[0].reshape(1, d), scale2, shift2,
                               w_peer_q[0].astype(bf16), s, tm)

    ids_t, g_t = _route_call(qp, peer_keys1[0], peer_keys2[0], tr=min(256, n))
    ids = ids_t.T * ROWS_PER_EXPERT
    gates2 = jnp.repeat(g_t.T, 2, axis=1)
    tt = 32
    wgt2 = _peer_u_call(ids, h2.reshape(n, 2, HALF), gates2, _pack_table(peer_u[0]), tt)
    peer = _peer_v_call(ids, wgt2, _pack_table(peer_v[0]), tt).reshape(n, d)

    out = _final_call(x1, peer, gate2, g_final.reshape(1, d), s, tm)
    return out.reshape(b, s, d)
```

```python
import functools

import jax
import jax.numpy as jnp
import numpy as np
from jax import lax
from jax.experimental import pallas as pl
from jax.experimental.pallas import tpu as pltpu

f32 = jnp.float32
bf16 = jnp.bfloat16
i32 = jnp.int32

N_HEADS_A = 8
HEAD_DIM_A = 64
ROPE_DIM_A = 16
NOPE_DIM_A = HEAD_DIM_A - ROPE_DIM_A
KV_RANK = 128
IDX_HEADS = 8
IDX_DIM = 32
IDX_ROPE_DIM = 8
TOPK_MAX = 256
N_HEADS_B = 8
HEAD_DIM_B = 64
WIDTH_A = N_HEADS_A * HEAD_DIM_A
WIDTH_B = N_HEADS_B * HEAD_DIM_B
ROPE_THETA = 500000.0
PEER_HEADS = 8
N_KEYS = 128
PEER_KEY_DIM = 128
PEER_TOPK = 16
PEER_SLOTS = PEER_HEADS * PEER_TOPK
EPS = 1e-6
IN_SIZES = (WIDTH_A, KV_RANK, ROPE_DIM_A, IDX_HEADS * IDX_DIM, IDX_DIM, IDX_HEADS,
            WIDTH_B, WIDTH_B, WIDTH_B, N_HEADS_B)

LANES = 128
NEG_BIG = -1e30
INT_MIN = -2147483648
VMEM_LIMIT = 56 * 1024 * 1024

C_R1 = 0
C_R2 = 128
C_LAT = 256
C_QN = 384
C_MISC = 768
C_QB = 1024
C_KB = 1536
C_VB = 2048
C_TOTAL = 2560

NT_DIMS = (((1,), (1,)), ((), ()))


def _in_perm():
    offs = np.cumsum((0,) + IN_SIZES)
    o_qa, o_lat, o_kr, o_iq, o_ik, o_iw, o_qb, o_kb, o_vb, o_fb = offs[:10]
    perm = -np.ones((C_TOTAL,), np.int64)
    ha, hi = ROPE_DIM_A // 2, IDX_ROPE_DIM // 2
    for h in range(N_HEADS_A):
        for j in range(ha):
            perm[C_R1 + h * ha + j] = o_qa + h * HEAD_DIM_A + j
            perm[C_R1 + 64 + h * ha + j] = o_qa + h * HEAD_DIM_A + ha + j
        for j in range(NOPE_DIM_A):
            perm[C_QN + h * NOPE_DIM_A + j] = o_qa + h * HEAD_DIM_A + ROPE_DIM_A + j
    for h in range(IDX_HEADS):
        for j in range(hi):
            perm[C_R2 + j * IDX_HEADS + h] = o_iq + h * IDX_DIM + j
            perm[C_R2 + 64 + j * IDX_HEADS + h] = o_iq + h * IDX_DIM + hi + j
        for j in range(IDX_DIM - IDX_ROPE_DIM):
            perm[C_MISC + j * IDX_HEADS + h] = o_iq + h * IDX_DIM + IDX_ROPE_DIM + j
    for j in range(ha):
        perm[C_R2 + 32 + j] = o_kr + j
        perm[C_R2 + 64 + 32 + j] = o_kr + ha + j
    for j in range(hi):
        perm[C_R2 + 40 + j] = o_ik + j
        perm[C_R2 + 64 + 40 + j] = o_ik + hi + j
    for j in range(IDX_DIM - IDX_ROPE_DIM):
        perm[C_MISC + 192 + j] = o_ik + IDX_ROPE_DIM + j
    for j in range(IDX_HEADS):
        perm[C_MISC + 216 + j] = o_iw + j
    for j in range(N_HEADS_B):
        perm[C_MISC + 224 + j] = o_fb + j
    perm[C_LAT:C_LAT + KV_RANK] = o_lat + np.arange(KV_RANK)
    perm[C_QB:C_QB + WIDTH_B] = o_qb + np.arange(WIDTH_B)
    perm[C_KB:C_KB + WIDTH_B] = o_kb + np.arange(WIDTH_B)
    perm[C_VB:C_VB + WIDTH_B] = o_vb + np.arange(WIDTH_B)
    return perm


def _params(sem):
    return pltpu.CompilerParams(dimension_semantics=sem, vmem_limit_bytes=VMEM_LIMIT)


def _mod_kernel(c_ref, w_ref, b_ref, o_ref):
    c = c_ref[...]
    ca = c * jax.nn.sigmoid(c)
    o_ref[...] = jnp.dot(ca, w_ref[...], preferred_element_type=f32,
                         precision=lax.Precision.HIGHEST) + b_ref[...]


def _mod_call(c, w, b):
    bsz, d = c.shape
    n = w.shape[1]
    return pl.pallas_call(
        _mod_kernel,
        grid=(n // d,),
        in_specs=[pl.BlockSpec((bsz, d), lambda j: (0, 0)),
                  pl.BlockSpec((d, d), lambda j: (0, j)),
                  pl.BlockSpec((1, d), lambda j: (0, j))],
        out_specs=pl.BlockSpec((bsz, d), lambda j: (0, j)),
        out_shape=jax.ShapeDtypeStruct((bsz, n), f32),
        compiler_params=_params(("arbitrary",)),
        name="mod",
    )(c, w, b.reshape(1, n))


def _inproj_kernel(x_ref, sc_ref, sh_ref, g_ref, w_ref, c1_ref, s1_ref, c2_ref, s2_ref,
                   gkv_ref, o_ref):
    x = x_ref[...]
    ms = jnp.mean(x * x, axis=-1, keepdims=True)
    h = x * lax.rsqrt(ms + EPS) * g_ref[...]
    h = h * (1.0 + sc_ref[0]) + sh_ref[0]
    p = jnp.dot(h.astype(bf16), w_ref[...], preferred_element_type=f32)
    r1 = p[:, C_R1:C_R1 + LANES]
    o_ref[:, C_R1:C_R1 + LANES] = r1 * c1_ref[...] + pltpu.roll(r1, 64, 1) * s1_ref[...]
    r2 = p[:, C_R2:C_R2 + LANES]
    o_ref[:, C_R2:C_R2 + LANES] = r2 * c2_ref[...] + pltpu.roll(r2, 64, 1) * s2_ref[...]
    lat = p[:, C_LAT:C_LAT + KV_RANK]
    lms = jnp.mean(lat * lat, axis=-1, keepdims=True)
    o_ref[:, C_LAT:C_LAT + KV_RANK] = lat * lax.rsqrt(lms + EPS) * gkv_ref[...]
    o_ref[:, C_QN:] = p[:, C_QN:]


def _inproj_call(x2, sc, sh, g, w, c1, s1, c2, s2, gkv, seq, tm):
    n, d = x2.shape
    per_b = seq // tm
    tok = lambda i: (i, 0)
    bat = lambda i: (i // per_b, 0, 0)
    cst = lambda i: (0, 0)
    return pl.pallas_call(
        _inproj_kernel,
        grid=(n // tm,),
        in_specs=[pl.BlockSpec((tm, d), tok),
                  pl.BlockSpec((1, 1, d), bat),
                  pl.BlockSpec((1, 1, d), bat),
                  pl.BlockSpec((1, d), cst),
                  pl.BlockSpec((d, C_TOTAL), cst),
                  pl.BlockSpec((tm, LANES), tok),
                  pl.BlockSpec((tm, LANES), tok),
                  pl.BlockSpec((tm, LANES), tok),
                  pl.BlockSpec((tm, LANES), tok),
                  pl.BlockSpec((1, KV_RANK), cst)],
        out_specs=pl.BlockSpec((tm, C_TOTAL), tok),
        out_shape=jax.ShapeDtypeStruct((n, C_TOTAL), f32),
        compiler_params=_params(("parallel",)),
        name="inproj",
    )(x2, sc, sh, g, w, c1, s1, c2, s2, gkv)


KCAT = 256


def _sortable(x):
    bits = lax.bitcast_convert_type(x, i32)
    return bits ^ ((bits >> 31) & 0x7FFFFFFF)


def _dsa_kernel(r1_ref, r2_ref, qn_ref, misc_ref, kcat_ref, latt_ref, ik_ref, wuk_ref, wuvt_ref,
                o_ref, key_scr, iq_scr, m_scr, l_scr, acc_scr, *, tq, tk, topk, nbits, seq):
    nh = N_HEADS_A
    qi = pl.program_id(1)
    nc = ((qi + 1) * tq + tk - 1) // tk
    qpos = qi * tq + lax.broadcasted_iota(i32, (tk, tq), 1)
    krow = lax.broadcasted_iota(i32, (tk, tq), 0)

    r2t = r2_ref[...].T
    mt = misc_ref[...].T
    iq_scr[...] = jnp.concatenate([r2t[0:32], r2t[64:96], mt[0:192]], axis=0)
    iqt = jnp.concatenate([iq_scr[pl.ds(h, IDX_DIM, stride=IDX_HEADS), :] for h in range(IDX_HEADS)],
                          axis=1).astype(bf16)
    iwt = mt[216:224] * (IDX_DIM ** -0.5 * IDX_HEADS ** -0.5)

    def score_chunk(c, carry):
        start = pl.multiple_of(c * tk, tk)
        d = jnp.dot(ik_ref[pl.ds(start, tk), :], iqt, preferred_element_type=f32)
        sc = jnp.zeros((tk, tq), f32)
        for h in range(IDX_HEADS):
            sc = sc + jnp.maximum(d[:, h * tq:(h + 1) * tq], 0.0) * iwt[h:h + 1, :]
        sc = jnp.where(start + krow <= qpos, sc, -jnp.inf)
        key_scr[c] = _sortable(sc)
        return carry

    lax.fori_loop(0, nc, score_chunk, 0)

    def count(pred):
        def body(c, acc):
            m = pred(key_scr[c], c * tk + krow)
            return acc + jnp.sum(m.reshape(tk // 8, 8, tq), axis=0)
        acc = lax.fori_loop(0, nc, body, jnp.zeros((8, tq), i32))
        return jnp.sum(acc, axis=0, keepdims=True)

    def bit_body(i, thr):
        cand = thr + lax.shift_left(jnp.int32(1), 31 - i)
        cnt = count(lambda k, col: jnp.where(k >= cand, 1, 0))
        return jnp.where(cnt >= topk, cand, thr)

    thr = lax.fori_loop(0, 32, bit_body, jnp.full((1, tq), INT_MIN, i32))
    need = topk - count(lambda k, col: jnp.where(k > thr, 1, 0))
    n_eq = count(lambda k, col: jnp.where(k == thr, 1, 0))

    def tie_search():
        def tie_body(i, last):
            cand = last + lax.shift_left(jnp.int32(1), nbits - 1 - i)
            below = count(lambda k, col: jnp.where(k == thr, jnp.where(col < cand, 1, 0), 0))
            return jnp.where(below < need, cand, last)
        return lax.fori_loop(0, nbits, tie_body, jnp.zeros((1, tq), i32))

    ambiguous = jnp.max(jnp.where(n_eq > need, 1, 0)) > 0
    last = lax.cond(ambiguous, tie_search, lambda: jnp.full((1, tq), seq, i32))

    scale = HEAD_DIM_A ** -0.5
    r1t = r1_ref[...].T
    qnt = qn_ref[...].T.astype(bf16)
    pad = jnp.zeros((KCAT - KV_RANK - ROPE_DIM_A, tq), f32)
    cols = []
    for h in range(nh):
        ql = jnp.dot(wuk_ref[h], qnt[h * NOPE_DIM_A:(h + 1) * NOPE_DIM_A], preferred_element_type=f32)
        cols.append(jnp.concatenate([ql, r1t[h * 8:(h + 1) * 8], r1t[64 + h * 8:64 + (h + 1) * 8], pad],
                                    axis=0))
    qcat = (jnp.concatenate(cols, axis=1) * scale).astype(bf16)

    m_scr[...] = jnp.full(m_scr.shape, NEG_BIG, f32)
    l_scr[...] = jnp.zeros(l_scr.shape, f32)
    acc_scr[...] = jnp.zeros(acc_scr.shape, f32)

    def attend_chunk(c, carry):
        start = pl.multiple_of(c * tk, tk)
        s = jnp.dot(kcat_ref[pl.ds(start, tk), :], qcat, preferred_element_type=f32)
        k = key_scr[c]
        col = start + krow
        tie = jnp.where(k == thr, jnp.where(col <= last, 0.0, NEG_BIG), NEG_BIG)
        bias = jnp.where(col <= qpos, jnp.where(k > thr, 0.0, tie), NEG_BIG)
        s = s + jnp.concatenate([bias] * nh, axis=1)
        m_old = m_scr[...]
        m_new = jnp.maximum(m_old, jnp.max(s, axis=0, keepdims=True))
        p = jnp.exp(s - m_new)
        alpha = jnp.exp(m_old - m_new)
        l_scr[...] = alpha * l_scr[...] + jnp.sum(p, axis=0, keepdims=True)
        acc_scr[...] = alpha * acc_scr[...] + jnp.dot(latt_ref[c], p.astype(bf16),
                                                      preferred_element_type=f32)
        m_scr[...] = m_new
        return carry

    lax.fori_loop(0, nc, attend_chunk, 0)

    ot = (acc_scr[...] / l_scr[...]).astype(bf16)
    outs = [jnp.dot(wuvt_ref[h], ot[:, h * tq:(h + 1) * tq], preferred_element_type=f32)
            for h in range(nh)]
    o_ref[...] = jnp.concatenate(outs, axis=0).T


def _dsa_call(proj, kcat, latt, ik, wuk, wuvt, bsz, seq, tq, tk):
    nh = N_HEADS_A
    topk = min(TOPK_MAX, seq // 4)
    nbits = max(1, (seq - 1).bit_length())
    nq = seq // tq
    kern = functools.partial(_dsa_kernel, tq=tq, tk=tk, topk=topk, nbits=nbits, seq=seq)
    qcol = lambda width, blk: pl.BlockSpec((tq, width), lambda b, i: (b * nq + i, blk))
    return pl.pallas_call(
        kern,
        grid=(bsz, nq),
        in_specs=[qcol(LANES, C_R1 // LANES), qcol(LANES, C_R2 // LANES),
                  qcol(C_MISC - C_QN, C_QN // (C_MISC - C_QN)), qcol(256, C_MISC // 256),
                  pl.BlockSpec((None, seq, KCAT), lambda b, i: (b, 0, 0)),
                  pl.BlockSpec((None, seq // tk, KV_RANK, tk), lambda b, i: (b, 0, 0, 0)),
                  pl.BlockSpec((None, seq, IDX_DIM), lambda b, i: (b, 0, 0)),
                  pl.BlockSpec(wuk.shape, lambda b, i: (0, 0, 0)),
                  pl.BlockSpec(wuvt.shape, lambda b, i: (0, 0, 0))],
        out_specs=pl.BlockSpec((tq, WIDTH_A), lambda b, i: (b * nq + i, 0)),
        out_shape=jax.ShapeDtypeStruct((bsz * seq, WIDTH_A), f32),
        scratch_shapes=[pltpu.VMEM((seq // tk, tk, tq), i32),
                        pltpu.VMEM((IDX_HEADS * IDX_DIM, tq), f32),
                        pltpu.VMEM((1, nh * tq), f32),
                        pltpu.VMEM((1, nh * tq), f32),
                        pltpu.VMEM((KV_RANK, nh * tq), f32)],
        compiler_params=_params(("parallel", "arbitrary")),
        name="dsa",
    )(proj, proj, proj, proj, kcat, latt, ik, wuk, wuvt)


def _fox_kernel(qt_ref, k_ref, vt_ref, cq_ref, ck_ref, o_ref, s_scr, *, tq, ts):
    qi = pl.program_id(2)
    nc = ((qi + 1) * tq + ts - 1) // ts
    qt = (qt_ref[...] * (HEAD_DIM_B ** -0.5)).astype(bf16)
    cq = cq_ref[...]
    qpos = qi * tq + lax.broadcasted_iota(i32, (ts, tq), 1)
    krow = lax.broadcasted_iota(i32, (ts, tq), 0)

    def score(c, m):
        start = pl.multiple_of(c * ts, ts)
        s = jnp.dot(k_ref[pl.ds(start, ts), :], qt, preferred_element_type=f32)
        s = s + cq - ck_ref[pl.ds(start, ts), :]
        s = jnp.where(start + krow <= qpos, s, NEG_BIG)
        s_scr[c] = s
        return jnp.maximum(m, jnp.max(s, axis=0, keepdims=True))

    m = lax.fori_loop(0, nc, score, jnp.full((1, tq), NEG_BIG, f32))

    def attend(c, carry):
        l, acc = carry
        p = jnp.exp(s_scr[c] - m)
        l = l + jnp.sum(p, axis=0, keepdims=True)
        acc = acc + jnp.dot(vt_ref[c], p.astype(bf16), preferred_element_type=f32)
        return l, acc

    l, acc = lax.fori_loop(0, nc, attend, (jnp.zeros((1, tq), f32), jnp.zeros((HEAD_DIM_B, tq), f32)))
    o_ref[...] = acc / l


def _fox_call(qt, k, vt, cq, ck, tq, ts):
    bsz, nh, hd, seq = qt.shape
    kern = functools.partial(_fox_kernel, tq=tq, ts=ts)
    return pl.pallas_call(
        kern,
        grid=(bsz, nh, seq // tq),
        in_specs=[pl.BlockSpec((None, None, hd, tq), lambda b, h, i: (b, h, 0, i)),
                  pl.BlockSpec((None, None, seq, hd), lambda b, h, i: (b, h, 0, 0)),
                  pl.BlockSpec((None, None, seq // ts, hd, ts), lambda b, h, i: (b, h, 0, 0, 0)),
                  pl.BlockSpec((None, None, 1, tq), lambda b, h, i: (b, h, 0, i)),
                  pl.BlockSpec((None, None, seq, 1), lambda b, h, i: (b, h, 0, 0))],
        out_specs=pl.BlockSpec((None, None, hd, tq), lambda b, h, i: (b, h, 0, i)),
        out_shape=jax.ShapeDtypeStruct((bsz, nh, hd, seq), f32),
        scratch_shapes=[pltpu.VMEM((seq // ts, ts, tq), f32)],
        compiler_params=_params(("parallel", "parallel", "arbitrary")),
        name="fox",
    )(qt, k, vt, cq, ck)


def _outproj_kernel(oa_ref, ob_ref, x_ref, g1_ref, ga_ref, gb_ref, wa_ref, wb_ref,
                    gf_ref, sc_ref, sh_ref, wq_ref, x1_ref, h2_ref, qp_ref):
    oa = oa_ref[...]
    ob = ob_ref[...]
    na = oa * lax.rsqrt(jnp.mean(oa * oa, axis=-1, keepdims=True) + EPS) * ga_ref[...]
    nb = ob * lax.rsqrt(jnp.mean(ob * ob, axis=-1, keepdims=True) + EPS) * gb_ref[...]
    res = (jnp.dot(na.astype(bf16), wa_ref[...], preferred_element_type=f32)
           + jnp.dot(nb.astype(bf16), wb_ref[...], preferred_element_type=f32))
    x1 = x_ref[...] + g1_ref[0] * res
    x1_ref[...] = x1
    h2 = x1 * lax.rsqrt(jnp.mean(x1 * x1, axis=-1, keepdims=True) + EPS) * gf_ref[...]
    h2 = h2 * (1.0 + sc_ref[0]) + sh_ref[0]
    h2_ref[...] = h2
    qp_ref[...] = jnp.dot(h2.astype(bf16), wq_ref[...], preferred_element_type=f32)


def _outproj_call(oa, ob, x2, g1, ga, gb, wa, wb, gf, sc, sh, wq, seq, tm):
    n, d = x2.shape
    per_b = seq // tm
    tok = lambda i: (i, 0)
    bat = lambda i: (i // per_b, 0, 0)
    cst = lambda i: (0, 0)
    nq = wq.shape[1]
    return pl.pallas_call(
        _outproj_kernel,
        grid=(n // tm,),
        in_specs=[pl.BlockSpec((tm, WIDTH_A), tok), pl.BlockSpec((tm, WIDTH_B), tok),
                  pl.BlockSpec((tm, d), tok), pl.BlockSpec((1, 1, d), bat),
                  pl.BlockSpec((1, WIDTH_A), cst), pl.BlockSpec((1, WIDTH_B), cst),
                  pl.BlockSpec((WIDTH_A, d), cst), pl.BlockSpec((WIDTH_B, d), cst),
                  pl.BlockSpec((1, d), cst), pl.BlockSpec((1, 1, d), bat),
                  pl.BlockSpec((1, 1, d), bat), pl.BlockSpec((d, nq), cst)],
        out_specs=[pl.BlockSpec((tm, d), tok), pl.BlockSpec((tm, d), tok),
                   pl.BlockSpec((tm, nq), tok)],
        out_shape=[jax.ShapeDtypeStruct((n, d), f32), jax.ShapeDtypeStruct((n, d), f32),
                   jax.ShapeDtypeStruct((n, nq), f32)],
        compiler_params=_params(("parallel",)),
        name="outproj",
    )(oa, ob, x2, g1, ga, gb, wa, wb, gf, sc, sh, wq)


def _topk_rows(x, kk):
    r = x.shape[0]
    iota = lax.broadcasted_iota(i32, x.shape, 0)
    vals, idxs = [], []
    for _ in range(kk):
        m = jnp.max(x, axis=0, keepdims=True)
        pos = jnp.min(jnp.where(x == m, iota, r), axis=0, keepdims=True)
        vals.append(m)
        idxs.append(pos)
        x = jnp.where(iota == pos, -jnp.inf, x)
    return jnp.concatenate(vals, axis=0), jnp.concatenate(idxs, axis=0)


def _route_kernel(qp_ref, k1_ref, k2_ref, ids_ref, g_ref):
    half = PEER_KEY_DIM // 2
    q = qp_ref[...]
    t = q.shape[0]
    hp = lax.Precision.HIGHEST
    s1 = lax.dot_general(k1_ref[...], q[:, :half], NT_DIMS, preferred_element_type=f32, precision=hp)
    s2 = lax.dot_general(k2_ref[...], q[:, half:], NT_DIMS, preferred_element_type=f32, precision=hp)
    v1, i1 = _topk_rows(s1, PEER_TOPK)
    v2, i2 = _topk_rows(s2, PEER_TOPK)
    cand = (v1[:, None, :] + v2[None, :, :]).reshape(PEER_TOPK * PEER_TOPK, t)
    cand_i = (i1[:, None, :] * N_KEYS + i2[None, :, :]).reshape(PEER_TOPK * PEER_TOPK, t)
    iota = lax.broadcasted_iota(i32, cand.shape, 0)
    best, experts = [], []
    for _ in range(PEER_TOPK):
        m = jnp.max(cand, axis=0, keepdims=True)
        pos = jnp.min(jnp.where(cand == m, iota, PEER_TOPK * PEER_TOPK), axis=0, keepdims=True)
        hit = iota == pos
        experts.append(jnp.sum(jnp.where(hit, cand_i, 0), axis=0, keepdims=True))
        best.append(m)
        cand = jnp.where(hit, -jnp.inf, cand)
    best = jnp.concatenate(best, axis=0)
    e = jnp.exp(best - best[0:1])
    g_ref[...] = e / jnp.sum(e, axis=0, keepdims=True)
    ids_ref[...] = jnp.concatenate(experts, axis=0)


def _route_call(qp, k1, k2, tr):
    n = qp.shape[0]
    return pl.pallas_call(
        _route_kernel,
        grid=(n // tr, PEER_HEADS),
        in_specs=[pl.BlockSpec((tr, PEER_KEY_DIM), lambda i, h: (i, h)),
                  pl.BlockSpec(k1.shape, lambda i, h: (0, 0)),
                  pl.BlockSpec(k2.shape, lambda i, h: (0, 0))],
        out_specs=[pl.BlockSpec((PEER_TOPK, tr), lambda i, h: (h, i)),
                   pl.BlockSpec((PEER_TOPK, tr), lambda i, h: (h, i))],
        out_shape=[jax.ShapeDtypeStruct((PEER_SLOTS, n), i32),
                   jax.ShapeDtypeStruct((PEER_SLOTS, n), f32)],
        compiler_params=_params(("parallel", "arbitrary")),
        name="route",
    )(qp, k1, k2)


ROWS_PER_EXPERT = 4
TILE_STRIDE = 136
TOK_UNROLL = 8
HALF = 512


def _pack_table(tab):
    e, d = tab.shape
    bits = lax.bitcast_convert_type(tab.astype(bf16), jnp.uint16).astype(jnp.uint32)
    word = bits[:, :d // 2] | (bits[:, d // 2:] << 16)
    return lax.bitcast_convert_type(word, i32).reshape(e * ROWS_PER_EXPERT, LANES)


def _gather_tile(tab_ref, rows_ref, t, tile_ref):
    tok_rows = rows_ref.at[t]
    for k in range(PEER_SLOTS):
        row = pl.multiple_of(tok_rows[k], ROWS_PER_EXPERT)
        tile_ref[pl.ds(k, ROWS_PER_EXPERT, stride=TILE_STRIDE), :] = tab_ref[pl.ds(row, ROWS_PER_EXPERT), :]
    chunks = [pltpu.bitcast(tile_ref[j * TILE_STRIDE:j * TILE_STRIDE + PEER_SLOTS, :], bf16)
              for j in range(ROWS_PER_EXPERT)]
    return jnp.concatenate(chunks, axis=1)


def _peer_u_kernel(ids_ref, h_ref, g_ref, u_ref, o_ref, tile_scr, act_scr, *, tt):
    even = lax.broadcasted_iota(i32, (1, 2 * PEER_SLOTS), 1) % 2 == 0

    def group(gi, carry):
        rows = []
        for tl in range(TOK_UNROLL):
            t = gi * TOK_UNROLL + tl
            r = _gather_tile(u_ref, ids_ref, t, tile_scr.at[tl % 2])
            h = h_ref[t]
            major = h.astype(bf16)
            minor = (h - major.astype(f32)).astype(bf16)
            out = lax.dot_general(jnp.concatenate([major, minor], axis=0), r, NT_DIMS,
                                  preferred_element_type=f32)
            rows.append(jnp.where(even, out[0:1] + out[2:3], out[1:2] + out[3:4]))
        act_scr[pl.ds(pl.multiple_of(gi * TOK_UNROLL, TOK_UNROLL), TOK_UNROLL), :] = (
            jnp.concatenate(rows, axis=0))
        return carry

    lax.fori_loop(0, tt // TOK_UNROLL, group, 0)
    part = act_scr[...]
    act = part + jnp.where(even, pltpu.roll(part, 2 * PEER_SLOTS - 1, 1), pltpu.roll(part, 1, 1))
    gelu = 0.5 * act * (1.0 + lax.erf(act * (2.0 ** -0.5)))
    o_ref[...] = g_ref[...] * gelu


def _peer_u_call(ids, h3, g2, u_packed, tt):
    n = ids.shape[0]
    kern = functools.partial(_peer_u_kernel, tt=tt)
    return pl.pallas_call(
        kern,
        grid=(n // tt,),
        in_specs=[pl.BlockSpec((tt, PEER_SLOTS), lambda i: (i, 0), memory_space=pltpu.SMEM),
                  pl.BlockSpec((tt, 2, HALF), lambda i: (i, 0, 0)),
                  pl.BlockSpec((tt, 2 * PEER_SLOTS), lambda i: (i, 0)),
                  pl.BlockSpec(u_packed.shape, lambda i: (0, 0), pipeline_mode=pl.Buffered(1))],
        out_specs=pl.BlockSpec((tt, 2 * PEER_SLOTS), lambda i: (i, 0)),
        out_shape=jax.ShapeDtypeStruct((n, 2 * PEER_SLOTS), f32),
        scratch_shapes=[pltpu.VMEM((2, ROWS_PER_EXPERT * TILE_STRIDE, LANES), i32),
                        pltpu.VMEM((tt, 2 * PEER_SLOTS), f32)],
        compiler_params=_params(("arbitrary",)),
        name="peer_u",
    )(ids, h3, g2, u_packed)


def _peer_v_kernel(ids_ref, w_ref, v_ref, o_ref, tile_scr, *, tt):
    even = lax.broadcasted_iota(i32, (1, 2 * PEER_SLOTS), 1) % 2 == 0

    def group(gi, carry):
        w8 = w_ref[pl.ds(pl.multiple_of(gi * TOK_UNROLL, TOK_UNROLL), TOK_UNROLL), :]
        for tl in range(TOK_UNROLL):
            t = gi * TOK_UNROLL + tl
            r = _gather_tile(v_ref, ids_ref, t, tile_scr.at[tl % 2])
            w = w8[tl:tl + 1]
            lhs = jnp.concatenate([jnp.where(even, w, 0.0), jnp.where(even, 0.0, w)], axis=0)
            o_ref[t] = jnp.dot(lhs.astype(bf16), r, preferred_element_type=f32)
        return carry

    lax.fori_loop(0, tt // TOK_UNROLL, group, 0)


def _peer_v_call(ids, wgt2, v_packed, tt):
    n = ids.shape[0]
    kern = functools.partial(_peer_v_kernel, tt=tt)
    return pl.pallas_call(
        kern,
        grid=(n // tt,),
        in_specs=[pl.BlockSpec((tt, PEER_SLOTS), lambda i: (i, 0), memory_space=pltpu.SMEM),
                  pl.BlockSpec((tt, 2 * PEER_SLOTS), lambda i: (i, 0)),
                  pl.BlockSpec(v_packed.shape, lambda i: (0, 0), pipeline_mode=pl.Buffered(1))],
        out_specs=pl.BlockSpec((tt, 2, HALF), lambda i: (i, 0, 0)),
        out_shape=jax.ShapeDtypeStruct((n, 2, HALF), f32),
        scratch_shapes=[pltpu.VMEM((2, ROWS_PER_EXPERT * TILE_STRIDE, LANES), i32)],
        compiler_params=_params(("arbitrary",)),
        name="peer_v",
    )(ids, wgt2, v_packed)


def _final_kernel(x1_ref, p_ref, g2_ref, gf_ref, o_ref):
    x2 = x1_ref[...] + g2_ref[0] * p_ref[...]
    o_ref[...] = x2 * lax.rsqrt(jnp.mean(x2 * x2, axis=-1, keepdims=True) + EPS) * gf_ref[...]


def _final_call(x1, peer, g2, gf, seq, tm):
    n, d = x1.shape
    per_b = seq // tm
    tok = lambda i: (i, 0)
    return pl.pallas_call(
        _final_kernel,
        grid=(n // tm,),
        in_specs=[pl.BlockSpec((tm, d), tok), pl.BlockSpec((tm, d), tok),
                  pl.BlockSpec((1, 1, d), lambda i: (i // per_b, 0, 0)),
                  pl.BlockSpec((1, d), lambda i: (0, 0))],
        out_specs=pl.BlockSpec((tm, d), tok),
        out_shape=jax.ShapeDtypeStruct((n, d), f32),
        compiler_params=_params(("parallel",)),
        name="final",
    )(x1, peer, g2, gf)


def _rope_tables(positions):
    def tab(rot):
        inv = ROPE_THETA ** (-jnp.arange(0, rot, 2, dtype=f32) / rot)
        ang = positions.astype(f32)[..., None] * inv
        return jnp.cos(ang), jnp.sin(ang)

    cos_a, sin_a = tab(ROPE_DIM_A)
    cos_i, sin_i = tab(IDX_ROPE_DIM)
    b, s = positions.shape
    n = b * s
    c_half1 = jnp.tile(cos_a, (1, 1, N_HEADS_A)).reshape(n, 64)
    s_half1 = jnp.tile(sin_a, (1, 1, N_HEADS_A)).reshape(n, 64)
    c1 = jnp.concatenate([c_half1, c_half1], axis=1)
    s1 = jnp.concatenate([-s_half1, s_half1], axis=1)
    pad_c = jnp.ones((n, 20), f32)
    pad_s = jnp.zeros((n, 20), f32)
    c_half2 = jnp.concatenate([jnp.repeat(cos_i, IDX_HEADS, axis=-1).reshape(n, 32),
                               cos_a.reshape(n, 8), cos_i.reshape(n, 4), pad_c], axis=1)
    s_half2 = jnp.concatenate([jnp.repeat(sin_i, IDX_HEADS, axis=-1).reshape(n, 32),
                               sin_a.reshape(n, 8), sin_i.reshape(n, 4), pad_s], axis=1)
    c2 = jnp.concatenate([c_half2, c_half2], axis=1)
    s2 = jnp.concatenate([-s_half2, s_half2], axis=1)
    return c1, s1, c2, s2


def kernel(x, c, positions, w_ada, b_ada, g_mix, w_in, g_kv, w_uk, w_uv, b_forget,
           g_out_a, g_out_b, w_out, g_ffn, w_peer_q, peer_keys1, peer_keys2, peer_u,
           peer_v, g_final):
    b, s, d = x.shape
    n = b * s
    assert w_ada.shape[0] == 1, "single layer supported"
    tm = min(512, s)
    x2 = x.reshape(n, d)

    mod = _mod_call(c, w_ada[0], b_ada[0])
    mod = mod.reshape(b, 6, 1, d)
    shift1, scale1, gate1, shift2, scale2, gate2 = [mod[:, j] for j in range(6)]

    perm = _in_perm()
    w_in_r = jnp.where((perm >= 0)[None, :], w_in[0][:, np.maximum(perm, 0)], 0.0).astype(bf16)
    c1, s1, c2, s2 = _rope_tables(positions)
    proj = _inproj_call(x2, scale1, shift1, g_mix[0].reshape(1, d), w_in_r, c1, s1, c2, s2,
                        g_kv[0].reshape(1, KV_RANK), s, tm)

    r2 = proj[:, C_R2:C_R2 + LANES]
    misc = proj[:, C_MISC:C_MISC + 256]

    tk = min(512, s)
    lat = proj[:, C_LAT:C_LAT + KV_RANK]
    kcat = jnp.concatenate([lat, r2[:, 32:40], r2[:, 96:104],
                            jnp.zeros((n, KCAT - KV_RANK - ROPE_DIM_A), f32)],
                           axis=1).astype(bf16).reshape(b, s, KCAT)
    latt = lat.astype(bf16).reshape(b, s // tk, tk, KV_RANK).transpose(0, 1, 3, 2)
    ik = jnp.concatenate([r2[:, 40:44], r2[:, 104:108], misc[:, 192:216]],
                         axis=-1).astype(bf16).reshape(b, s, IDX_DIM)
    wuk = w_uk[0].transpose(1, 0, 2).astype(bf16)
    wuvt = w_uv[0].transpose(1, 2, 0).astype(bf16)
    o_a = _dsa_call(proj, kcat, latt, ik, wuk, wuvt, b, s, tq=128, tk=tk)

    ts = min(512, s)

    def heads_t(a):
        return a.reshape(b, s, N_HEADS_B, HEAD_DIM_B).transpose(0, 2, 3, 1)

    qbt = heads_t(proj[:, C_QB:C_KB])
    kb = proj[:, C_KB:C_VB].astype(bf16).reshape(b, s, N_HEADS_B, HEAD_DIM_B).transpose(0, 2, 1, 3)
    vbt = (proj[:, C_VB:C_TOTAL].astype(bf16).reshape(b, s // ts, ts, N_HEADS_B, HEAD_DIM_B)
           .transpose(0, 3, 1, 4, 2))
    fl = misc[:, 224:232].reshape(b, s, N_HEADS_B) + b_forget[0]
    cum = jnp.cumsum(jax.nn.log_sigmoid(fl), axis=1).transpose(0, 2, 1)
    o_bt = _fox_call(qbt, kb, vbt, cum[:, :, None, :], cum[..., None], tq=min(256, s), ts=ts)
    o_b = o_bt.transpose(0, 3, 1, 2).reshape(n, WIDTH_B)

    wo = w_out[0].astype(bf16)
    x1, h2, qp = _outproj_call(o_a, o_b, x2, gate1, g_out_a[0].reshape(1, WIDTH_A),
                               g_out_b[0].reshape(1, WIDTH_B), wo[:WIDTH_A], wo[WIDTH_A:],
                               g_ffn[0].reshape(1, d), scale2, shift2,
                               w_peer_q[0].astype(bf16), s, tm)

    ids_t, g_t = _route_call(qp, peer_keys1[0], peer_keys2[0], tr=min(256, n))
    ids = ids_t.T * ROWS_PER_EXPERT
    gates2 = jnp.repeat(g_t.T, 2, axis=1)
    tt = 32
    wgt2 = _peer_u_call(ids, h2.reshape(n, 2, HALF), gates2, _pack_table(peer_u[0]), tt)
    peer = _peer_v_call(ids, wgt2, _pack_table(peer_v[0]), tt).reshape(n, d)

    out = _final_call(x1, peer, gate2, g_final.reshape(1, d), s, tm)
    return out.reshape(b, s, d)
```

```python
import functools

import jax
import jax.numpy as jnp
import numpy as np
from jax import lax
from jax.experimental import pallas as pl
from jax.experimental.pallas import tpu as pltpu

f32 = jnp.float32
bf16 = jnp.bfloat16
i32 = jnp.int32

N_HEADS_A = 8
HEAD_DIM_A = 64
ROPE_DIM_A = 16
NOPE_DIM_A = HEAD_DIM_A - ROPE_DIM_A
KV_RANK = 128
IDX_HEADS = 8
IDX_DIM = 32
IDX_ROPE_DIM = 8
TOPK_MAX = 256
N_HEADS_B = 8
HEAD_DIM_B = 64
WIDTH_A = N_HEADS_A * HEAD_DIM_A
WIDTH_B = N_HEADS_B * HEAD_DIM_B
ROPE_THETA = 500000.0
PEER_HEADS = 8
N_KEYS = 128
PEER_KEY_DIM = 128
PEER_TOPK = 16
PEER_SLOTS = PEER_HEADS * PEER_TOPK
EPS = 1e-6
IN_SIZES = (WIDTH_A, KV_RANK, ROPE_DIM_A, IDX_HEADS * IDX_DIM, IDX_DIM, IDX_HEADS,
            WIDTH_B, WIDTH_B, WIDTH_B, N_HEADS_B)

LANES = 128
NEG_BIG = -1e30
INT_MIN = -2147483648
VMEM_LIMIT = 56 * 1024 * 1024

C_R1 = 0
C_R2 = 128
C_LAT = 256
C_QN = 384
C_MISC = 768
C_QB = 1024
C_KB = 1536
C_VB = 2048
C_TOTAL = 2560

NT_DIMS = (((1,), (1,)), ((), ()))


def _in_perm():
    offs = np.cumsum((0,) + IN_SIZES)
    o_qa, o_lat, o_kr, o_iq, o_ik, o_iw, o_qb, o_kb, o_vb, o_fb = offs[:10]
    perm = -np.ones((C_TOTAL,), np.int64)
    ha, hi = ROPE_DIM_A // 2, IDX_ROPE_DIM // 2
    for h in range(N_HEADS_A):
        for j in range(ha):
            perm[C_R1 + h * ha + j] = o_qa + h * HEAD_DIM_A + j
            perm[C_R1 + 64 + h * ha + j] = o_qa + h * HEAD_DIM_A + ha + j
        for j in range(NOPE_DIM_A):
            perm[C_QN + h * NOPE_DIM_A + j] = o_qa + h * HEAD_DIM_A + ROPE_DIM_A + j
    for h in range(IDX_HEADS):
        for j in range(hi):
            perm[C_R2 + j * IDX_HEADS + h] = o_iq + h * IDX_DIM + j
            perm[C_R2 + 64 + j * IDX_HEADS + h] = o_iq + h * IDX_DIM + hi + j
        for j in range(IDX_DIM - IDX_ROPE_DIM):
            perm[C_MISC + j * IDX_HEADS + h] = o_iq + h * IDX_DIM + IDX_ROPE_DIM + j
    for j in range(ha):
        perm[C_R2 + 32 + j] = o_kr + j
        perm[C_R2 + 64 + 32 + j] = o_kr + ha + j
    for j in range(hi):
        perm[C_R2 + 40 + j] = o_ik + j
        perm[C_R2 + 64 + 40 + j] = o_ik + hi + j
    for j in range(IDX_DIM - IDX_ROPE_DIM):
        perm[C_MISC + 192 + j] = o_ik + IDX_ROPE_DIM + j
    for j in range(IDX_HEADS):
        perm[C_MISC + 216 + j] = o_iw + j
    for j in range(N_HEADS_B):
        perm[C_MISC + 224 + j] = o_fb + j
    perm[C_LAT:C_LAT + KV_RANK] = o_lat + np.arange(KV_RANK)
    perm[C_QB:C_QB + WIDTH_B] = o_qb + np.arange(WIDTH_B)
    perm[C_KB:C_KB + WIDTH_B] = o_kb + np.arange(WIDTH_B)
    perm[C_VB:C_VB + WIDTH_B] = o_vb + np.arange(WIDTH_B)
    return perm


def _params(sem):
    return pltpu.CompilerParams(dimension_semantics=sem, vmem_limit_bytes=VMEM_LIMIT)


def _mod_kernel(c_ref, w_ref, b_ref, o_ref):
    c = c_ref[...]
    ca = c * jax.nn.sigmoid(c)
    o_ref[...] = jnp.dot(ca, w_ref[...], preferred_element_type=f32,
                         precision=lax.Precision.HIGHEST) + b_ref[...]


def _mod_call(c, w, b):
    bsz, d = c.shape
    n = w.shape[1]
    return pl.pallas_call(
        _mod_kernel,
        grid=(n // d,),
        in_specs=[pl.BlockSpec((bsz, d), lambda j: (0, 0)),
                  pl.BlockSpec((d, d), lambda j: (0, j)),
                  pl.BlockSpec((1, d), lambda j: (0, j))],
        out_specs=pl.BlockSpec((bsz, d), lambda j: (0, j)),
        out_shape=jax.ShapeDtypeStruct((bsz, n), f32),
        compiler_params=_params(("arbitrary",)),
        name="mod",
    )(c, w, b.reshape(1, n))


def _inproj_kernel(x_ref, sc_ref, sh_ref, g_ref, w_ref, c1_ref, s1_ref, c2_ref, s2_ref,
                   gkv_ref, o_ref):
    x = x_ref[...]
    ms = jnp.mean(x * x, axis=-1, keepdims=True)
    h = x * lax.rsqrt(ms + EPS) * g_ref[...]
    h = h * (1.0 + sc_ref[0]) + sh_ref[0]
    p = jnp.dot(h.astype(bf16), w_ref[...], preferred_element_type=f32)
    r1 = p[:, C_R1:C_R1 + LANES]
    o_ref[:, C_R1:C_R1 + LANES] = r1 * c1_ref[...] + pltpu.roll(r1, 64, 1) * s1_ref[...]
    r2 = p[:, C_R2:C_R2 + LANES]
    o_ref[:, C_R2:C_R2 + LANES] = r2 * c2_ref[...] + pltpu.roll(r2, 64, 1) * s2_ref[...]
    lat = p[:, C_LAT:C_LAT + KV_RANK]
    lms = jnp.mean(lat * lat, axis=-1, keepdims=True)
    o_ref[:, C_LAT:C_LAT + KV_RANK] = lat * lax.rsqrt(lms + EPS) * gkv_ref[...]
    o_ref[:, C_QN:] = p[:, C_QN:]


def _inproj_call(x2, sc, sh, g, w, c1, s1, c2, s2, gkv, seq, tm):
    n, d = x2.shape
    per_b = seq // tm
    tok = lambda i: (i, 0)
    bat = lambda i: (i // per_b, 0, 0)
    cst = lambda i: (0, 0)
    return pl.pallas_call(
        _inproj_kernel,
        grid=(n // tm,),
        in_specs=[pl.BlockSpec((tm, d), tok),
                  pl.BlockSpec((1, 1, d), bat),
                  pl.BlockSpec((1, 1, d), bat),
                  pl.BlockSpec((1, d), cst),
                  pl.BlockSpec((d, C_TOTAL), cst),
                  pl.BlockSpec((tm, LANES), tok),
                  pl.BlockSpec((tm, LANES), tok),
                  pl.BlockSpec((tm, LANES), tok),
                  pl.BlockSpec((tm, LANES), tok),
                  pl.BlockSpec((1, KV_RANK), cst)],
        out_specs=pl.BlockSpec((tm, C_TOTAL), tok),
        out_shape=jax.ShapeDtypeStruct((n, C_TOTAL), f32),
        compiler_params=_params(("parallel",)),
        name="inproj",
    )(x2, sc, sh, g, w, c1, s1, c2, s2, gkv)


KCAT = 256
ONES_ROWS = 16
LOG2E = 1.4426950408889634


def _with_ones_rows(vt):
    lead = vt.shape[:-2]
    t = vt.shape[-1]
    ones = jnp.ones(lead + (1, t), vt.dtype)
    zeros = jnp.zeros(lead + (ONES_ROWS - 1, t), vt.dtype)
    return jnp.concatenate([vt, ones, zeros], axis=-2)


def _sortable(x):
    bits = lax.bitcast_convert_type(x, i32)
    return bits ^ ((bits >> 31) & 0x7FFFFFFF)


def _dsa_kernel(r1_ref, r2_ref, qn_ref, misc_ref, kcat_ref, latt_ref, ik_ref, wuk_ref, wuvt_ref,
                o_ref, key_scr, iq_scr, m_scr, acc_scr, *, tq, tk, topk, nbits, seq):
    nh = N_HEADS_A
    qi = pl.program_id(1)
    nc = ((qi + 1) * tq + tk - 1) // tk
    qpos = qi * tq + lax.broadcasted_iota(i32, (tk, tq), 1)
    krow = lax.broadcasted_iota(i32, (tk, tq), 0)

    r2t = r2_ref[...].T
    mt = misc_ref[...].T
    iq_scr[...] = jnp.concatenate([r2t[0:32], r2t[64:96], mt[0:192]], axis=0)
    iqt = jnp.concatenate([iq_scr[pl.ds(h, IDX_DIM, stride=IDX_HEADS), :] for h in range(IDX_HEADS)],
                          axis=1).astype(bf16)
    iwt = mt[216:224] * (IDX_DIM ** -0.5 * IDX_HEADS ** -0.5)

    def score_chunk(c, carry):
        start = pl.multiple_of(c * tk, tk)
        d = jnp.dot(ik_ref[pl.ds(start, tk), :], iqt, preferred_element_type=f32)
        sc = jnp.zeros((tk, tq), f32)
        for h in range(IDX_HEADS):
            sc = sc + jnp.maximum(d[:, h * tq:(h + 1) * tq], 0.0) * iwt[h:h + 1, :]
        sc = jnp.where(start + krow <= qpos, sc, -jnp.inf)
        key_scr[c] = _sortable(sc)
        return carry

    lax.fori_loop(0, nc, score_chunk, 0)

    def count(pred):
        def body(c, acc):
            m = pred(key_scr[c], c * tk + krow)
            return acc + jnp.sum(m.reshape(tk // 8, 8, tq), axis=0)
        acc = lax.fori_loop(0, nc, body, jnp.zeros((8, tq), i32))
        return jnp.sum(acc, axis=0, keepdims=True)

    def bit_body(i, thr):
        cand = thr + lax.shift_left(jnp.int32(1), 31 - i)
        cnt = count(lambda k, col: jnp.where(k >= cand, 1, 0))
        return jnp.where(cnt >= topk, cand, thr)

    thr = lax.fori_loop(0, 32, bit_body, jnp.full((1, tq), INT_MIN, i32))
    need = topk - count(lambda k, col: jnp.where(k > thr, 1, 0))
    n_eq = count(lambda k, col: jnp.where(k == thr, 1, 0))

    def tie_search():
        def tie_body(i, last):
            cand = last + lax.shift_left(jnp.int32(1), nbits - 1 - i)
            below = count(lambda k, col: jnp.where(k == thr, jnp.where(col < cand, 1, 0), 0))
            return jnp.where(below < need, cand, last)
        return lax.fori_loop(0, nbits, tie_body, jnp.zeros((1, tq), i32))

    ambiguous = jnp.max(jnp.where(n_eq > need, 1, 0)) > 0
    last = lax.cond(ambiguous, tie_search, lambda: jnp.full((1, tq), seq, i32))

    scale = HEAD_DIM_A ** -0.5 * LOG2E
    r1t = r1_ref[...].T
    qnt = qn_ref[...].T.astype(bf16)
    pad = jnp.zeros((KCAT - KV_RANK - ROPE_DIM_A, tq), f32)
    cols = []
    for h in range(nh):
        ql = jnp.dot(wuk_ref[h], qnt[h * NOPE_DIM_A:(h + 1) * NOPE_DIM_A], preferred_element_type=f32)
        cols.append(jnp.concatenate([ql, r1t[h * 8:(h + 1) * 8], r1t[64 + h * 8:64 + (h + 1) * 8], pad],
                                    axis=0))
    qcat = (jnp.concatenate(cols, axis=1) * scale).astype(bf16)

    m_scr[...] = jnp.full(m_scr.shape, NEG_BIG, f32)
    acc_scr[...] = jnp.zeros(acc_scr.shape, f32)

    def attend_chunk(c, carry):
        start = pl.multiple_of(c * tk, tk)
        s = jnp.dot(kcat_ref[pl.ds(start, tk), :], qcat, preferred_element_type=f32)
        k = key_scr[c]
        col = start + krow
        tie = jnp.where(k == thr, jnp.where(col <= last, 0.0, NEG_BIG), NEG_BIG)
        bias = jnp.where(col <= qpos, jnp.where(k > thr, 0.0, tie), NEG_BIG)
        s = s + jnp.concatenate([bias] * nh, axis=1)
        m_old = m_scr[...]
        m_new = jnp.maximum(m_old, jnp.max(s, axis=0, keepdims=True))
        p = jnp.exp2(s - m_new)
        alpha = jnp.exp2(m_old - m_new)
        acc_scr[...] = alpha * acc_scr[...] + jnp.dot(latt_ref[c], p.astype(bf16),
                                                      preferred_element_type=f32)
        m_scr[...] = m_new
        return carry

    lax.fori_loop(0, nc, attend_chunk, 0)

    ot = (acc_scr[0:KV_RANK, :] / acc_scr[KV_RANK:KV_RANK + 1, :]).astype(bf16)
    outs = [jnp.dot(wuvt_ref[h], ot[:, h * tq:(h + 1) * tq], preferred_element_type=f32)
            for h in range(nh)]
    o_ref[...] = jnp.concatenate(outs, axis=0).T


def _dsa_call(proj, kcat, latt, ik, wuk, wuvt, bsz, seq, tq, tk):
    nh = N_HEADS_A
    topk = min(TOPK_MAX, seq // 4)
    nbits = max(1, (seq - 1).bit_length())
    nq = seq // tq
    kern = functools.partial(_dsa_kernel, tq=tq, tk=tk, topk=topk, nbits=nbits, seq=seq)
    qcol = lambda width, blk: pl.BlockSpec((tq, width), lambda b, i: (b * nq + i, blk))
    return pl.pallas_call(
        kern,
        grid=(bsz, nq),
        in_specs=[qcol(LANES, C_R1 // LANES), qcol(LANES, C_R2 // LANES),
                  qcol(C_MISC - C_QN, C_QN // (C_MISC - C_QN)), qcol(256, C_MISC // 256),
                  pl.BlockSpec((None, seq, KCAT), lambda b, i: (b, 0, 0)),
                  pl.BlockSpec((None, seq // tk, KV_RANK + ONES_ROWS, tk), lambda b, i: (b, 0, 0, 0)),
                  pl.BlockSpec((None, seq, IDX_DIM), lambda b, i: (b, 0, 0)),
                  pl.BlockSpec(wuk.shape, lambda b, i: (0, 0, 0)),
                  pl.BlockSpec(wuvt.shape, lambda b, i: (0, 0, 0))],
        out_specs=pl.BlockSpec((tq, WIDTH_A), lambda b, i: (b * nq + i, 0)),
        out_shape=jax.ShapeDtypeStruct((bsz * seq, WIDTH_A), f32),
        scratch_shapes=[pltpu.VMEM((seq // tk, tk, tq), i32),
                        pltpu.VMEM((IDX_HEADS * IDX_DIM, tq), f32),
                        pltpu.VMEM((1, nh * tq), f32),
                        pltpu.VMEM((KV_RANK + ONES_ROWS, nh * tq), f32)],
        compiler_params=_params(("parallel", "arbitrary")),
        name="dsa",
    )(proj, proj, proj, proj, kcat, latt, ik, wuk, wuvt)


def _fox_kernel(qt_ref, k_ref, vt_ref, cq_ref, ck_ref, o_ref, s_scr, *, tq, ts):
    qi = pl.program_id(2)
    nc = ((qi + 1) * tq + ts - 1) // ts
    qt = (qt_ref[...] * (HEAD_DIM_B ** -0.5 * LOG2E)).astype(bf16)
    cq = cq_ref[...]
    qpos = qi * tq + lax.broadcasted_iota(i32, (ts, tq), 1)
    krow = lax.broadcasted_iota(i32, (ts, tq), 0)

    def score(c, m):
        start = pl.multiple_of(c * ts, ts)
        s = jnp.dot(k_ref[pl.ds(start, ts), :], qt, preferred_element_type=f32)
        s = s + cq - ck_ref[pl.ds(start, ts), :]
        s = jnp.where(start + krow <= qpos, s, NEG_BIG)
        s_scr[c] = s
        return jnp.maximum(m, jnp.max(s, axis=0, keepdims=True))

    m = lax.fori_loop(0, nc, score, jnp.full((1, tq), NEG_BIG, f32))

    def attend(c, acc):
        p = jnp.exp2(s_scr[c] - m)
        return acc + jnp.dot(vt_ref[c], p.astype(bf16), preferred_element_type=f32)

    acc = lax.fori_loop(0, nc, attend, jnp.zeros((HEAD_DIM_B + ONES_ROWS, tq), f32))
    o_ref[...] = acc[0:HEAD_DIM_B] / acc[HEAD_DIM_B:HEAD_DIM_B + 1]


def _fox_call(qt, k, vt, cq, ck, tq, ts):
    bsz, nh, hd, seq = qt.shape
    kern = functools.partial(_fox_kernel, tq=tq, ts=ts)
    return pl.pallas_call(
        kern,
        grid=(bsz, nh, seq // tq),
        in_specs=[pl.BlockSpec((None, None, hd, tq), lambda b, h, i: (b, h, 0, i)),
                  pl.BlockSpec((None, None, seq, hd), lambda b, h, i: (b, h, 0, 0)),
                  pl.BlockSpec((None, None, seq // ts, hd + ONES_ROWS, ts), lambda b, h, i: (b, h, 0, 0, 0)),
                  pl.BlockSpec((None, None, 1, tq), lambda b, h, i: (b, h, 0, i)),
                  pl.BlockSpec((None, None, seq, 1), lambda b, h, i: (b, h, 0, 0))],
        out_specs=pl.BlockSpec((None, None, hd, tq), lambda b, h, i: (b, h, 0, i)),
        out_shape=jax.ShapeDtypeStruct((bsz, nh, hd, seq), f32),
        scratch_shapes=[pltpu.VMEM((seq // ts, ts, tq), f32)],
        compiler_params=_params(("parallel", "parallel", "arbitrary")),
        name="fox",
    )(qt, k, vt, cq, ck)


def _outproj_kernel(oa_ref, ob_ref, x_ref, g1_ref, ga_ref, gb_ref, wa_ref, wb_ref,
                    gf_ref, sc_ref, sh_ref, wq_ref, x1_ref, h2_ref, qp_ref):
    oa = oa_ref[...]
    ob = ob_ref[...]
    na = oa * lax.rsqrt(jnp.mean(oa * oa, axis=-1, keepdims=True) + EPS) * ga_ref[...]
    nb = ob * lax.rsqrt(jnp.mean(ob * ob, axis=-1, keepdims=True) + EPS) * gb_ref[...]
    res = (jnp.dot(na.astype(bf16), wa_ref[...], preferred_element_type=f32)
           + jnp.dot(nb.astype(bf16), wb_ref[...], preferred_element_type=f32))
    x1 = x_ref[...] + g1_ref[0] * res
    x1_ref[...] = x1
    h2 = x1 * lax.rsqrt(jnp.mean(x1 * x1, axis=-1, keepdims=True) + EPS) * gf_ref[...]
    h2 = h2 * (1.0 + sc_ref[0]) + sh_ref[0]
    h2_ref[...] = h2
    qp_ref[...] = jnp.dot(h2.astype(bf16), wq_ref[...], preferred_element_type=f32)


def _outproj_call(oa, ob, x2, g1, ga, gb, wa, wb, gf, sc, sh, wq, seq, tm):
    n, d = x2.shape
    per_b = seq // tm
    tok = lambda i: (i, 0)
    bat = lambda i: (i // per_b, 0, 0)
    cst = lambda i: (0, 0)
    nq = wq.shape[1]
    return pl.pallas_call(
        _outproj_kernel,
        grid=(n // tm,),
        in_specs=[pl.BlockSpec((tm, WIDTH_A), tok), pl.BlockSpec((tm, WIDTH_B), tok),
                  pl.BlockSpec((tm, d), tok), pl.BlockSpec((1, 1, d), bat),
                  pl.BlockSpec((1, WIDTH_A), cst), pl.BlockSpec((1, WIDTH_B), cst),
                  pl.BlockSpec((WIDTH_A, d), cst), pl.BlockSpec((WIDTH_B, d), cst),
                  pl.BlockSpec((1, d), cst), pl.BlockSpec((1, 1, d), bat),
                  pl.BlockSpec((1, 1, d), bat), pl.BlockSpec((d, nq), cst)],
        out_specs=[pl.BlockSpec((tm, d), tok), pl.BlockSpec((tm, d), tok),
                   pl.BlockSpec((tm, nq), tok)],
        out_shape=[jax.ShapeDtypeStruct((n, d), f32), jax.ShapeDtypeStruct((n, d), f32),
                   jax.ShapeDtypeStruct((n, nq), f32)],
        compiler_params=_params(("parallel",)),
        name="outproj",
    )(oa, ob, x2, g1, ga, gb, wa, wb, gf, sc, sh, wq)


def _argmax_rows(x, iota):
    vals = [x[j:j + 8] for j in range(0, x.shape[0], 8)]
    idxs = [iota[j:j + 8] for j in range(0, x.shape[0], 8)]
    while len(vals) > 1:
        nv, ni = [], []
        for a in range(0, len(vals) - 1, 2):
            keep = vals[a] >= vals[a + 1]
            nv.append(jnp.where(keep, vals[a], vals[a + 1]))
            ni.append(jnp.where(keep, idxs[a], idxs[a + 1]))
        if len(vals) % 2:
            nv.append(vals[-1])
            ni.append(idxs[-1])
        vals, idxs = nv, ni
    m = jnp.max(vals[0], axis=0, keepdims=True)
    pos = jnp.min(jnp.where(vals[0] == m, idxs[0], x.shape[0]), axis=0, keepdims=True)
    return m, pos


def _topk_rows(x, kk):
    iota = lax.broadcasted_iota(i32, x.shape, 0)
    vals, idxs = [], []
    for _ in range(kk):
        m, pos = _argmax_rows(x, iota)
        vals.append(m)
        idxs.append(pos)
        x = jnp.where(iota == pos, -jnp.inf, x)
    return jnp.concatenate(vals, axis=0), jnp.concatenate(idxs, axis=0)


CAND_PAIRS = [(a, b) for a in range(PEER_TOPK) for b in range(PEER_TOPK) if (a + 1) * (b + 1) <= PEER_TOPK]
CAND_ROWS = -(-len(CAND_PAIRS) // 8) * 8
ROUTE_UNROLL = 4


def _cand_select():
    sel = np.zeros((2, CAND_ROWS, PEER_TOPK), np.float32)
    for r, (a, b) in enumerate(CAND_PAIRS):
        sel[0, r, a] = 1.0
        sel[1, r, b] = 1.0
    return sel


def _route_kernel(qp_ref, k1_ref, k2_ref, sel_ref, ids_ref, g_ref, *, nl):
    half = PEER_KEY_DIM // 2
    hp = lax.Precision.HIGHEST
    pick = lambda j, v: jnp.dot(sel_ref[j], v, preferred_element_type=f32, precision=hp)
    rows = lax.broadcasted_iota(i32, (CAND_ROWS, LANES), 0)

    def lane_tile(l):
        q = qp_ref[pl.ds(pl.multiple_of(l * LANES, LANES), LANES), :]
        s1 = lax.dot_general(k1_ref[...], q[:, :half], NT_DIMS, preferred_element_type=f32, precision=hp)
        s2 = lax.dot_general(k2_ref[...], q[:, half:], NT_DIMS, preferred_element_type=f32, precision=hp)
        v1, i1 = _topk_rows(s1, PEER_TOPK)
        v2, i2 = _topk_rows(s2, PEER_TOPK)
        cand = jnp.where(rows < len(CAND_PAIRS), pick(0, v1) + pick(1, v2), -jnp.inf)
        cand_e = pick(0, i1.astype(f32)) * N_KEYS + pick(1, i2.astype(f32))
        best, experts = [], []
        for _ in range(PEER_TOPK):
            m, pos = _argmax_rows(cand, rows)
            hit = rows == pos
            experts.append(jnp.sum(jnp.where(hit, cand_e, 0.0), axis=0, keepdims=True))
            best.append(m)
            cand = jnp.where(hit, -jnp.inf, cand)
        best = jnp.concatenate(best, axis=0)
        e = jnp.exp(best - best[0:1])
        g_ref[l] = e / jnp.sum(e, axis=0, keepdims=True)
        ids_ref[l] = jnp.concatenate(experts, axis=0).astype(i32)

    def group(gi, carry):
        for j in range(ROUTE_UNROLL):
            lane_tile(gi * ROUTE_UNROLL + j)
        return carry

    lax.fori_loop(0, nl // ROUTE_UNROLL, group, 0)


def _route_call(qp, k1, k2, tr):
    n = qp.shape[0]
    nl = tr // LANES
    sel = jnp.asarray(_cand_select())
    kern = functools.partial(_route_kernel, nl=nl)
    return pl.pallas_call(
        kern,
        grid=(n // tr, PEER_HEADS),
        in_specs=[pl.BlockSpec((tr, PEER_KEY_DIM), lambda i, h: (i, h)),
                  pl.BlockSpec(k1.shape, lambda i, h: (0, 0)),
                  pl.BlockSpec(k2.shape, lambda i, h: (0, 0)),
                  pl.BlockSpec(sel.shape, lambda i, h: (0, 0, 0))],
        out_specs=[pl.BlockSpec((nl, PEER_TOPK, LANES), lambda i, h: (i, h, 0)),
                   pl.BlockSpec((nl, PEER_TOPK, LANES), lambda i, h: (i, h, 0))],
        out_shape=[jax.ShapeDtypeStruct((n // LANES, PEER_SLOTS, LANES), i32),
                   jax.ShapeDtypeStruct((n // LANES, PEER_SLOTS, LANES), f32)],
        compiler_params=_params(("parallel", "arbitrary")),
        name="route",
    )(qp, k1, k2, sel)


ROWS_PER_EXPERT = 4
TILE_STRIDE = 136
TOK_UNROLL = 8
HALF = 512


def _pack_table(tab):
    e, d = tab.shape
    bits = lax.bitcast_convert_type(tab.astype(bf16), jnp.uint16).astype(jnp.uint32)
    word = bits[:, :d // 2] | (bits[:, d // 2:] << 16)
    return lax.bitcast_convert_type(word, i32).reshape(e * ROWS_PER_EXPERT, LANES)


def _gather_tile(tab_ref, rows_ref, t, tile_ref):
    tok_rows = rows_ref.at[t]
    for k in range(PEER_SLOTS):
        row = pl.multiple_of(tok_rows[k], ROWS_PER_EXPERT)
        tile_ref[pl.ds(k, ROWS_PER_EXPERT, stride=TILE_STRIDE), :] = tab_ref[pl.ds(row, ROWS_PER_EXPERT), :]
    chunks = [pltpu.bitcast(tile_ref[j * TILE_STRIDE:j * TILE_STRIDE + PEER_SLOTS, :], bf16)
              for j in range(ROWS_PER_EXPERT)]
    return jnp.concatenate(chunks, axis=1)


def _peer_u_kernel(ids_ref, h_ref, g_ref, u_ref, o_ref, tile_scr, act_scr, *, tt):
    even = lax.broadcasted_iota(i32, (1, 2 * PEER_SLOTS), 1) % 2 == 0

    def group(gi, carry):
        rows = []
        for tl in range(TOK_UNROLL):
            t = gi * TOK_UNROLL + tl
            r = _gather_tile(u_ref, ids_ref, t, tile_scr.at[tl % 2])
            h = h_ref[t]
            major = h.astype(bf16)
            minor = (h - major.astype(f32)).astype(bf16)
            out = lax.dot_general(jnp.concatenate([major, minor], axis=0), r, NT_DIMS,
                                  preferred_element_type=f32)
            rows.append(jnp.where(even, out[0:1] + out[2:3], out[1:2] + out[3:4]))
        act_scr[pl.ds(pl.multiple_of(gi * TOK_UNROLL, TOK_UNROLL), TOK_UNROLL), :] = (
            jnp.concatenate(rows, axis=0))
        return carry

    lax.fori_loop(0, tt // TOK_UNROLL, group, 0)
    part = act_scr[...]
    act = part + jnp.where(even, pltpu.roll(part, 2 * PEER_SLOTS - 1, 1), pltpu.roll(part, 1, 1))
    gelu = 0.5 * act * (1.0 + lax.erf(act * (2.0 ** -0.5)))
    o_ref[...] = g_ref[...] * gelu


def _peer_u_call(ids, h3, g2, u_packed, tt):
    n = ids.shape[0]
    kern = functools.partial(_peer_u_kernel, tt=tt)
    return pl.pallas_call(
        kern,
        grid=(n // tt,),
        in_specs=[pl.BlockSpec((tt, PEER_SLOTS), lambda i: (i, 0), memory_space=pltpu.SMEM),
                  pl.BlockSpec((tt, 2, HALF), lambda i: (i, 0, 0)),
                  pl.BlockSpec((tt, 2 * PEER_SLOTS), lambda i: (i, 0)),
                  pl.BlockSpec(u_packed.shape, lambda i: (0, 0), pipeline_mode=pl.Buffered(1))],
        out_specs=pl.BlockSpec((tt, 2 * PEER_SLOTS), lambda i: (i, 0)),
        out_shape=jax.ShapeDtypeStruct((n, 2 * PEER_SLOTS), f32),
        scratch_shapes=[pltpu.VMEM((2, ROWS_PER_EXPERT * TILE_STRIDE, LANES), i32),
                        pltpu.VMEM((tt, 2 * PEER_SLOTS), f32)],
        compiler_params=_params(("arbitrary",)),
        name="peer_u",
    )(ids, h3, g2, u_packed)


def _peer_v_kernel(ids_ref, w_ref, v_ref, o_ref, tile_scr, *, tt):
    even = lax.broadcasted_iota(i32, (1, 2 * PEER_SLOTS), 1) % 2 == 0

    def group(gi, carry):
        w8 = w_ref[pl.ds(pl.multiple_of(gi * TOK_UNROLL, TOK_UNROLL), TOK_UNROLL), :]
        for tl in range(TOK_UNROLL):
            t = gi * TOK_UNROLL + tl
            r = _gather_tile(v_ref, ids_ref, t, tile_scr.at[tl % 2])
            w = w8[tl:tl + 1]
            lhs = jnp.concatenate([jnp.where(even, w, 0.0), jnp.where(even, 0.0, w)], axis=0)
            o_ref[t] = jnp.dot(lhs.astype(bf16), r, preferred_element_type=f32)
        return carry

    lax.fori_loop(0, tt // TOK_UNROLL, group, 0)


def _peer_v_call(ids, wgt2, v_packed, tt):
    n = ids.shape[0]
    kern = functools.partial(_peer_v_kernel, tt=tt)
    return pl.pallas_call(
        kern,
        grid=(n // tt,),
        in_specs=[pl.BlockSpec((tt, PEER_SLOTS), lambda i: (i, 0), memory_space=pltpu.SMEM),
                  pl.BlockSpec((tt, 2 * PEER_SLOTS), lambda i: (i, 0)),
                  pl.BlockSpec(v_packed.shape, lambda i: (0, 0), pipeline_mode=pl.Buffered(1))],
        out_specs=pl.BlockSpec((tt, 2, HALF), lambda i: (i, 0, 0)),
        out_shape=jax.ShapeDtypeStruct((n, 2, HALF), f32),
        scratch_shapes=[pltpu.VMEM((2, ROWS_PER_EXPERT * TILE_STRIDE, LANES), i32)],
        compiler_params=_params(("arbitrary",)),
        name="peer_v",
    )(ids, wgt2, v_packed)


def _final_kernel(x1_ref, p_ref, g2_ref, gf_ref, o_ref):
    x2 = x1_ref[...] + g2_ref[0] * p_ref[...]
    o_ref[...] = x2 * lax.rsqrt(jnp.mean(x2 * x2, axis=-1, keepdims=True) + EPS) * gf_ref[...]


def _final_call(x1, peer, g2, gf, seq, tm):
    n, d = x1.shape
    per_b = seq // tm
    tok = lambda i: (i, 0)
    return pl.pallas_call(
        _final_kernel,
        grid=(n // tm,),
        in_specs=[pl.BlockSpec((tm, d), tok), pl.BlockSpec((tm, d), tok),
                  pl.BlockSpec((1, 1, d), lambda i: (i // per_b, 0, 0)),
                  pl.BlockSpec((1, d), lambda i: (0, 0))],
        out_specs=pl.BlockSpec((tm, d), tok),
        out_shape=jax.ShapeDtypeStruct((n, d), f32),
        compiler_params=_params(("parallel",)),
        name="final",
    )(x1, peer, g2, gf)


def _rope_tables(positions):
    def tab(rot):
        inv = ROPE_THETA ** (-jnp.arange(0, rot, 2, dtype=f32) / rot)
        ang = positions.astype(f32)[..., None] * inv
        return jnp.cos(ang), jnp.sin(ang)

    cos_a, sin_a = tab(ROPE_DIM_A)
    cos_i, sin_i = tab(IDX_ROPE_DIM)
    b, s = positions.shape
    n = b * s
    c_half1 = jnp.tile(cos_a, (1, 1, N_HEADS_A)).reshape(n, 64)
    s_half1 = jnp.tile(sin_a, (1, 1, N_HEADS_A)).reshape(n, 64)
    c1 = jnp.concatenate([c_half1, c_half1], axis=1)
    s1 = jnp.concatenate([-s_half1, s_half1], axis=1)
    pad_c = jnp.ones((n, 20), f32)
    pad_s = jnp.zeros((n, 20), f32)
    c_half2 = jnp.concatenate([jnp.repeat(cos_i, IDX_HEADS, axis=-1).reshape(n, 32),
                               cos_a.reshape(n, 8), cos_i.reshape(n, 4), pad_c], axis=1)
    s_half2 = jnp.concatenate([jnp.repeat(sin_i, IDX_HEADS, axis=-1).reshape(n, 32),
                               sin_a.reshape(n, 8), sin_i.reshape(n, 4), pad_s], axis=1)
    c2 = jnp.concatenate([c_half2, c_half2], axis=1)
    s2 = jnp.concatenate([-s_half2, s_half2], axis=1)
    return c1, s1, c2, s2


def kernel(x, c, positions, w_ada, b_ada, g_mix, w_in, g_kv, w_uk, w_uv, b_forget,
           g_out_a, g_out_b, w_out, g_ffn, w_peer_q, peer_keys1, peer_keys2, peer_u,
           peer_v, g_final):
    b, s, d = x.shape
    n = b * s
    assert w_ada.shape[0] == 1, "single layer supported"
    tm = min(512, s)
    x2 = x.reshape(n, d)

    mod = _mod_call(c, w_ada[0], b_ada[0])
    mod = mod.reshape(b, 6, 1, d)
    shift1, scale1, gate1, shift2, scale2, gate2 = [mod[:, j] for j in range(6)]

    perm = _in_perm()
    w_in_r = jnp.where((perm >= 0)[None, :], w_in[0][:, np.maximum(perm, 0)], 0.0).astype(bf16)
    c1, s1, c2, s2 = _rope_tables(positions)
    proj = _inproj_call(x2, scale1, shift1, g_mix[0].reshape(1, d), w_in_r, c1, s1, c2, s2,
                        g_kv[0].reshape(1, KV_RANK), s, tm)

    r2 = proj[:, C_R2:C_R2 + LANES]
    misc = proj[:, C_MISC:C_MISC + 256]

    tk = min(512, s)
    lat = proj[:, C_LAT:C_LAT + KV_RANK]
    kcat = jnp.concatenate([lat, r2[:, 32:40], r2[:, 96:104],
                            jnp.zeros((n, KCAT - KV_RANK - ROPE_DIM_A), f32)],
                           axis=1).astype(bf16).reshape(b, s, KCAT)
    latt = _with_ones_rows(lat.astype(bf16).reshape(b, s // tk, tk, KV_RANK).transpose(0, 1, 3, 2))
    ik = jnp.concatenate([r2[:, 40:44], r2[:, 104:108], misc[:, 192:216]],
                         axis=-1).astype(bf16).reshape(b, s, IDX_DIM)
    wuk = w_uk[0].transpose(1, 0, 2).astype(bf16)
    wuvt = w_uv[0].transpose(1, 2, 0).astype(bf16)
    o_a = _dsa_call(proj, kcat, latt, ik, wuk, wuvt, b, s, tq=128, tk=tk)

    ts = min(512, s)

    def heads_t(a):
        return a.reshape(b, s, N_HEADS_B, HEAD_DIM_B).transpose(0, 2, 3, 1)

    qbt = heads_t(proj[:, C_QB:C_KB])
    kb = proj[:, C_KB:C_VB].astype(bf16).reshape(b, s, N_HEADS_B, HEAD_DIM_B).transpose(0, 2, 1, 3)
    vbt = _with_ones_rows(proj[:, C_VB:C_TOTAL].astype(bf16).reshape(b, s // ts, ts, N_HEADS_B, HEAD_DIM_B)
                          .transpose(0, 3, 1, 4, 2))
    fl = misc[:, 224:232].reshape(b, s, N_HEADS_B) + b_forget[0]
    cum = (jnp.cumsum(jax.nn.log_sigmoid(fl), axis=1) * LOG2E).transpose(0, 2, 1)
    o_bt = _fox_call(qbt, kb, vbt, cum[:, :, None, :], cum[..., None], tq=min(256, s), ts=ts)
    o_b = o_bt.transpose(0, 3, 1, 2).reshape(n, WIDTH_B)

    wo = w_out[0].astype(bf16)
    x1, h2, qp = _outproj_call(o_a, o_b, x2, gate1, g_out_a[0].reshape(1, WIDTH_A),
                               g_out_b[0].reshape(1, WIDTH_B), wo[:WIDTH_A], wo[WIDTH_A:],
                               g_ffn[0].reshape(1, d), scale2, shift2,
                               w_peer_q[0].astype(bf16), s, tm)

    ids_t, g_t = _route_call(qp, peer_keys1[0], peer_keys2[0], tr=min(1024, n))
    token_major = lambda a: a.transpose(0, 2, 1).reshape(n, PEER_SLOTS)
    ids = token_major(ids_t) * ROWS_PER_EXPERT
    gates2 = jnp.repeat(token_major(g_t), 2, axis=1)
    tt = 32
    wgt2 = _peer_u_call(ids, h2.reshape(n, 2, HALF), gates2, _pack_table(peer_u[0]), tt)
    peer = _peer_v_call(ids, wgt2, _pack_table(peer_v[0]), tt).reshape(n, d)

    out = _final_call(x1, peer, gate2, g_final.reshape(1, d), s, tm)
    return out.reshape(b, s, d)
```

```python
import functools

import jax
import jax.numpy as jnp
import numpy as np
from jax import lax
from jax.experimental import pallas as pl
from jax.experimental.pallas import tpu as pltpu

f32 = jnp.float32
bf16 = jnp.bfloat16
i32 = jnp.int32

N_HEADS_A = 8
HEAD_DIM_A = 64
ROPE_DIM_A = 16
NOPE_DIM_A = HEAD_DIM_A - ROPE_DIM_A
KV_RANK = 128
IDX_HEADS = 8
IDX_DIM = 32
IDX_ROPE_DIM = 8
TOPK_MAX = 256
N_HEADS_B = 8
HEAD_DIM_B = 64
WIDTH_A = N_HEADS_A * HEAD_DIM_A
WIDTH_B = N_HEADS_B * HEAD_DIM_B
ROPE_THETA = 500000.0
PEER_HEADS = 8
N_KEYS = 128
PEER_KEY_DIM = 128
PEER_TOPK = 16
PEER_SLOTS = PEER_HEADS * PEER_TOPK
EPS = 1e-6
IN_SIZES = (WIDTH_A, KV_RANK, ROPE_DIM_A, IDX_HEADS * IDX_DIM, IDX_DIM, IDX_HEADS,
            WIDTH_B, WIDTH_B, WIDTH_B, N_HEADS_B)

LANES = 128
NEG_BIG = -1e30
INT_MIN = -2147483648
VMEM_LIMIT = 56 * 1024 * 1024

C_R1 = 0
C_R2 = 128
C_LAT = 256
C_QN = 384
C_MISC = 768
C_QB = 1024
C_KB = 1536
C_VB = 2048
C_TOTAL = 2560

NT_DIMS = (((1,), (1,)), ((), ()))


def _in_perm():
    offs = np.cumsum((0,) + IN_SIZES)
    o_qa, o_lat, o_kr, o_iq, o_ik, o_iw, o_qb, o_kb, o_vb, o_fb = offs[:10]
    perm = -np.ones((C_TOTAL,), np.int64)
    ha, hi = ROPE_DIM_A // 2, IDX_ROPE_DIM // 2
    for h in range(N_HEADS_A):
        for j in range(ha):
            perm[C_R1 + h * ha + j] = o_qa + h * HEAD_DIM_A + j
            perm[C_R1 + 64 + h * ha + j] = o_qa + h * HEAD_DIM_A + ha + j
        for j in range(NOPE_DIM_A):
            perm[C_QN + h * NOPE_DIM_A + j] = o_qa + h * HEAD_DIM_A + ROPE_DIM_A + j
    for h in range(IDX_HEADS):
        for j in range(hi):
            perm[C_R2 + j * IDX_HEADS + h] = o_iq + h * IDX_DIM + j
            perm[C_R2 + 64 + j * IDX_HEADS + h] = o_iq + h * IDX_DIM + hi + j
        for j in range(IDX_DIM - IDX_ROPE_DIM):
            perm[C_MISC + j * IDX_HEADS + h] = o_iq + h * IDX_DIM + IDX_ROPE_DIM + j
    for j in range(ha):
        perm[C_R2 + 32 + j] = o_kr + j
        perm[C_R2 + 64 + 32 + j] = o_kr + ha + j
    for j in range(hi):
        perm[C_R2 + 40 + j] = o_ik + j
        perm[C_R2 + 64 + 40 + j] = o_ik + hi + j
    for j in range(IDX_DIM - IDX_ROPE_DIM):
        perm[C_MISC + 192 + j] = o_ik + IDX_ROPE_DIM + j
    for j in range(IDX_HEADS):
        perm[C_MISC + 216 + j] = o_iw + j
    for j in range(N_HEADS_B):
        perm[C_MISC + 224 + j] = o_fb + j
    perm[C_LAT:C_LAT + KV_RANK] = o_lat + np.arange(KV_RANK)
    perm[C_QB:C_QB + WIDTH_B] = o_qb + np.arange(WIDTH_B)
    perm[C_KB:C_KB + WIDTH_B] = o_kb + np.arange(WIDTH_B)
    perm[C_VB:C_VB + WIDTH_B] = o_vb + np.arange(WIDTH_B)
    return perm


def _params(sem):
    return pltpu.CompilerParams(dimension_semantics=sem, vmem_limit_bytes=VMEM_LIMIT)


def _mod_kernel(c_ref, w_ref, b_ref, o_ref):
    c = c_ref[...]
    ca = c * jax.nn.sigmoid(c)
    o_ref[...] = jnp.dot(ca, w_ref[...], preferred_element_type=f32,
                         precision=lax.Precision.HIGHEST) + b_ref[...]


def _mod_call(c, w, b):
    bsz, d = c.shape
    n = w.shape[1]
    return pl.pallas_call(
        _mod_kernel,
        grid=(n // d,),
        in_specs=[pl.BlockSpec((bsz, d), lambda j: (0, 0)),
                  pl.BlockSpec((d, d), lambda j: (0, j)),
                  pl.BlockSpec((1, d), lambda j: (0, j))],
        out_specs=pl.BlockSpec((bsz, d), lambda j: (0, j)),
        out_shape=jax.ShapeDtypeStruct((bsz, n), f32),
        compiler_params=_params(("arbitrary",)),
        name="mod",
    )(c, w, b.reshape(1, n))


KCAT = 256
ONES_ROWS = 16


def _with_ones_rows(vt):
    lead = vt.shape[:-2]
    t = vt.shape[-1]
    ones = jnp.ones(lead + (1, t), vt.dtype)
    zeros = jnp.zeros(lead + (ONES_ROWS - 1, t), vt.dtype)
    return jnp.concatenate([vt, ones, zeros], axis=-2)


def _lanes_from(lane, pieces):
    out = jnp.zeros(lane.shape, f32)
    for end, src, start in reversed(pieces):
        begin = max([e for e, _, _ in pieces if e < end], default=0)
        out = jnp.where(lane < end, pltpu.roll(src, (begin - start) % LANES, 1), out)
    return out


def _inproj_kernel(x_ref, sc_ref, sh_ref, g_ref, w_ref, c1_ref, s1_ref, c2_ref, s2_ref,
                   gkv_ref, o_ref, kcat_ref, ik_ref, latt_ref, vbt_ref):
    x = x_ref[...]
    ms = jnp.mean(x * x, axis=-1, keepdims=True)
    h = x * lax.rsqrt(ms + EPS) * g_ref[...]
    h = h * (1.0 + sc_ref[0]) + sh_ref[0]
    p = jnp.dot(h.astype(bf16), w_ref[...], preferred_element_type=f32)
    r1 = p[:, C_R1:C_R1 + LANES]
    o_ref[:, C_R1:C_R1 + LANES] = r1 * c1_ref[...] + pltpu.roll(r1, 64, 1) * s1_ref[...]
    r2 = p[:, C_R2:C_R2 + LANES]
    r2 = r2 * c2_ref[...] + pltpu.roll(r2, 64, 1) * s2_ref[...]
    o_ref[:, C_R2:C_R2 + LANES] = r2
    lat = p[:, C_LAT:C_LAT + KV_RANK]
    lms = jnp.mean(lat * lat, axis=-1, keepdims=True)
    lat = lat * lax.rsqrt(lms + EPS) * gkv_ref[...]
    o_ref[:, C_LAT:C_LAT + KV_RANK] = lat
    o_ref[:, C_QN:] = p[:, C_QN:]

    lane = lax.broadcasted_iota(i32, r2.shape, 1)
    kr = _lanes_from(lane, [(8, r2, 32), (16, r2, 96)])
    kcat_ref[...] = jnp.concatenate([lat, kr], axis=1).astype(bf16)
    misc_hi = p[:, C_MISC + LANES:C_MISC + 2 * LANES]
    ik_ref[...] = _lanes_from(lane, [(4, r2, 40), (8, r2, 104), (32, misc_hi, 64)]).astype(bf16)
    latt_ref[...] = _with_ones_rows(lat.T.astype(bf16))
    vt = p[:, C_VB:C_VB + WIDTH_B].T.astype(bf16)
    vbt_ref[...] = _with_ones_rows(vt.reshape(N_HEADS_B, HEAD_DIM_B, vt.shape[1]))


def _inproj_call(x2, sc, sh, g, w, c1, s1, c2, s2, gkv, seq, tm):
    n, d = x2.shape
    per_b = seq // tm
    bsz = n // seq
    tok = lambda i: (i, 0)
    bat = lambda i: (i // per_b, 0, 0)
    cst = lambda i: (0, 0)
    return pl.pallas_call(
        _inproj_kernel,
        grid=(n // tm,),
        in_specs=[pl.BlockSpec((tm, d), tok),
                  pl.BlockSpec((1, 1, d), bat),
                  pl.BlockSpec((1, 1, d), bat),
                  pl.BlockSpec((1, d), cst),
                  pl.BlockSpec((d, C_TOTAL), cst),
                  pl.BlockSpec((tm, LANES), tok),
                  pl.BlockSpec((tm, LANES), tok),
                  pl.BlockSpec((tm, LANES), tok),
                  pl.BlockSpec((tm, LANES), tok),
                  pl.BlockSpec((1, KV_RANK), cst)],
        out_specs=[pl.BlockSpec((tm, C_TOTAL), tok),
                   pl.BlockSpec((tm, KCAT), tok),
                   pl.BlockSpec((tm, LANES), tok),
                   pl.BlockSpec((None, None, KV_RANK + ONES_ROWS, tm),
                                lambda i: (i // per_b, i % per_b, 0, 0)),
                   pl.BlockSpec((None, N_HEADS_B, None, HEAD_DIM_B + ONES_ROWS, tm),
                                lambda i: (i // per_b, 0, i % per_b, 0, 0))],
        out_shape=[jax.ShapeDtypeStruct((n, C_TOTAL), f32),
                   jax.ShapeDtypeStruct((n, KCAT), bf16),
                   jax.ShapeDtypeStruct((n, LANES), bf16),
                   jax.ShapeDtypeStruct((bsz, per_b, KV_RANK + ONES_ROWS, tm), bf16),
                   jax.ShapeDtypeStruct((bsz, N_HEADS_B, per_b, HEAD_DIM_B + ONES_ROWS, tm), bf16)],
        compiler_params=_params(("parallel",)),
        name="inproj",
    )(x2, sc, sh, g, w, c1, s1, c2, s2, gkv)


LOG2E = 1.4426950408889634


def _sortable(x):
    bits = lax.bitcast_convert_type(x, i32)
    return bits ^ ((bits >> 31) & 0x7FFFFFFF)


def _dsa_kernel(r1_ref, r2_ref, qn_ref, misc_ref, kcat_ref, latt_ref, ik_ref, wuk_ref, wuvt_ref,
                o_ref, key_scr, iq_scr, m_scr, acc_scr, *, tq, tk, topk, nbits, seq):
    nh = N_HEADS_A
    qi = pl.program_id(1)
    nc = ((qi + 1) * tq + tk - 1) // tk
    qpos = qi * tq + lax.broadcasted_iota(i32, (tk, tq), 1)
    krow = lax.broadcasted_iota(i32, (tk, tq), 0)

    r2t = r2_ref[...].T
    mt = misc_ref[...].T
    iq_scr[...] = jnp.concatenate([r2t[0:32], r2t[64:96], mt[0:192]], axis=0)
    iqt = jnp.concatenate([iq_scr[pl.ds(h, IDX_DIM, stride=IDX_HEADS), :] for h in range(IDX_HEADS)],
                          axis=1)
    iqt = jnp.concatenate([iqt, jnp.zeros((LANES - IDX_DIM, IDX_HEADS * tq), f32)], axis=0).astype(bf16)
    iwt = mt[216:224] * (IDX_DIM ** -0.5 * IDX_HEADS ** -0.5)

    def score_chunk(c, carry):
        start = pl.multiple_of(c * tk, tk)
        d = jnp.dot(ik_ref[pl.ds(start, tk), :], iqt, preferred_element_type=f32)
        sc = jnp.zeros((tk, tq), f32)
        for h in range(IDX_HEADS):
            sc = sc + jnp.maximum(d[:, h * tq:(h + 1) * tq], 0.0) * iwt[h:h + 1, :]
        sc = jnp.where(start + krow <= qpos, sc, -jnp.inf)
        key_scr[c] = _sortable(sc)
        return carry

    lax.fori_loop(0, nc, score_chunk, 0)

    def count(pred):
        def body(c, acc):
            m = pred(key_scr[c], c * tk + krow)
            return acc + jnp.sum(m.reshape(tk // 8, 8, tq), axis=0)
        acc = lax.fori_loop(0, nc, body, jnp.zeros((8, tq), i32))
        return jnp.sum(acc, axis=0, keepdims=True)

    def bit_body(i, thr):
        cand = thr + lax.shift_left(jnp.int32(1), 31 - i)
        cnt = count(lambda k, col: jnp.where(k >= cand, 1, 0))
        return jnp.where(cnt >= topk, cand, thr)

    thr = lax.fori_loop(0, 32, bit_body, jnp.full((1, tq), INT_MIN, i32))
    need = topk - count(lambda k, col: jnp.where(k > thr, 1, 0))
    n_eq = count(lambda k, col: jnp.where(k == thr, 1, 0))

    def tie_search():
        def tie_body(i, last):
            cand = last + lax.shift_left(jnp.int32(1), nbits - 1 - i)
            below = count(lambda k, col: jnp.where(k == thr, jnp.where(col < cand, 1, 0), 0))
            return jnp.where(below < need, cand, last)
        return lax.fori_loop(0, nbits, tie_body, jnp.zeros((1, tq), i32))

    ambiguous = jnp.max(jnp.where(n_eq > need, 1, 0)) > 0
    last = lax.cond(ambiguous, tie_search, lambda: jnp.full((1, tq), seq, i32))

    scale = HEAD_DIM_A ** -0.5 * LOG2E
    r1t = r1_ref[...].T
    qnt = qn_ref[...].T.astype(bf16)
    pad = jnp.zeros((KCAT - KV_RANK - ROPE_DIM_A, tq), f32)
    cols = []
    for h in range(nh):
        ql = jnp.dot(wuk_ref[h], qnt[h * NOPE_DIM_A:(h + 1) * NOPE_DIM_A], preferred_element_type=f32)
        cols.append(jnp.concatenate([ql, r1t[h * 8:(h + 1) * 8], r1t[64 + h * 8:64 + (h + 1) * 8], pad],
                                    axis=0))
    qcat = (jnp.concatenate(cols, axis=1) * scale).astype(bf16)

    m_scr[...] = jnp.full(m_scr.shape, NEG_BIG, f32)
    acc_scr[...] = jnp.zeros(acc_scr.shape, f32)

    def attend_chunk(c, carry):
        start = pl.multiple_of(c * tk, tk)
        s = jnp.dot(kcat_ref[pl.ds(start, tk), :], qcat, preferred_element_type=f32)
        k = key_scr[c]
        col = start + krow
        tie = jnp.where(k == thr, jnp.where(col <= last, 0.0, NEG_BIG), NEG_BIG)
        bias = jnp.where(col <= qpos, jnp.where(k > thr, 0.0, tie), NEG_BIG)
        s = s + jnp.concatenate([bias] * nh, axis=1)
        m_old = m_scr[...]
        m_new = jnp.maximum(m_old, jnp.max(s, axis=0, keepdims=True))
        p = jnp.exp2(s - m_new)
        alpha = jnp.exp2(m_old - m_new)
        acc_scr[...] = alpha * acc_scr[...] + jnp.dot(latt_ref[c], p.astype(bf16),
                                                      preferred_element_type=f32)
        m_scr[...] = m_new
        return carry

    lax.fori_loop(0, nc, attend_chunk, 0)

    ot = (acc_scr[0:KV_RANK, :] / acc_scr[KV_RANK:KV_RANK + 1, :]).astype(bf16)
    outs = [jnp.dot(wuvt_ref[h], ot[:, h * tq:(h + 1) * tq], preferred_element_type=f32)
            for h in range(nh)]
    o_ref[...] = jnp.concatenate(outs, axis=0).T


def _dsa_call(proj, kcat, latt, ik, wuk, wuvt, bsz, seq, tq, tk):
    nh = N_HEADS_A
    topk = min(TOPK_MAX, seq // 4)
    nbits = max(1, (seq - 1).bit_length())
    nq = seq // tq
    kern = functools.partial(_dsa_kernel, tq=tq, tk=tk, topk=topk, nbits=nbits, seq=seq)
    qcol = lambda width, blk: pl.BlockSpec((tq, width), lambda b, i: (b * nq + i, blk))
    return pl.pallas_call(
        kern,
        grid=(bsz, nq),
        in_specs=[qcol(LANES, C_R1 // LANES), qcol(LANES, C_R2 // LANES),
                  qcol(C_MISC - C_QN, C_QN // (C_MISC - C_QN)), qcol(256, C_MISC // 256),
                  pl.BlockSpec((None, seq, KCAT), lambda b, i: (b, 0, 0)),
                  pl.BlockSpec((None, seq // tk, KV_RANK + ONES_ROWS, tk), lambda b, i: (b, 0, 0, 0)),
                  pl.BlockSpec((None, seq, LANES), lambda b, i: (b, 0, 0)),
                  pl.BlockSpec(wuk.shape, lambda b, i: (0, 0, 0)),
                  pl.BlockSpec(wuvt.shape, lambda b, i: (0, 0, 0))],
        out_specs=pl.BlockSpec((tq, WIDTH_A), lambda b, i: (b * nq + i, 0)),
        out_shape=jax.ShapeDtypeStruct((bsz * seq, WIDTH_A), f32),
        scratch_shapes=[pltpu.VMEM((seq // tk, tk, tq), i32),
                        pltpu.VMEM((IDX_HEADS * IDX_DIM, tq), f32),
                        pltpu.VMEM((1, nh * tq), f32),
                        pltpu.VMEM((KV_RANK + ONES_ROWS, nh * tq), f32)],
        compiler_params=_params(("parallel", "arbitrary")),
        name="dsa",
    )(proj, proj, proj, proj, kcat, latt, ik, wuk, wuvt)


def _fox_kernel(q_ref, k_ref, vt_ref, cq_ref, ck_ref, o_ref, s_scr, *, tq, ts):
    qi = pl.program_id(2)
    nc = ((qi + 1) * tq + ts - 1) // ts
    mine = lax.broadcasted_iota(i32, (LANES, tq), 0) // HEAD_DIM_B == pl.program_id(1) % 2
    qt = jnp.where(mine, q_ref[...].T * (HEAD_DIM_B ** -0.5 * LOG2E), 0.0).astype(bf16)
    cq = cq_ref[...]
    qpos = qi * tq + lax.broadcasted_iota(i32, (ts, tq), 1)
    krow = lax.broadcasted_iota(i32, (ts, tq), 0)

    def score(c, m):
        start = pl.multiple_of(c * ts, ts)
        s = jnp.dot(k_ref[pl.ds(start, ts), :].astype(bf16), qt, preferred_element_type=f32)
        s = s + cq - ck_ref[pl.ds(start, ts), :]
        s = jnp.where(start + krow <= qpos, s, NEG_BIG)
        s_scr[c] = s
        return jnp.maximum(m, jnp.max(s, axis=0, keepdims=True))

    m = lax.fori_loop(0, nc, score, jnp.full((1, tq), NEG_BIG, f32))

    def attend(c, acc):
        p = jnp.exp2(s_scr[c] - m)
        return acc + jnp.dot(vt_ref[c], p.astype(bf16), preferred_element_type=f32)

    acc = lax.fori_loop(0, nc, attend, jnp.zeros((HEAD_DIM_B + ONES_ROWS, tq), f32))
    o_ref[...] = acc[0:HEAD_DIM_B] / acc[HEAD_DIM_B:HEAD_DIM_B + 1]


def _fox_call(proj, vt, cq, ck, tq, ts):
    bsz, nh, _, _, _ = vt.shape
    hd = HEAD_DIM_B
    seq = proj.shape[0] // bsz
    nq = seq // tq
    kern = functools.partial(_fox_kernel, tq=tq, ts=ts)
    return pl.pallas_call(
        kern,
        grid=(bsz, nh, nq),
        in_specs=[pl.BlockSpec((tq, LANES), lambda b, h, i: (b * nq + i, C_QB // LANES + h // 2)),
                  pl.BlockSpec((seq, LANES), lambda b, h, i: (b, C_KB // LANES + h // 2)),
                  pl.BlockSpec((None, None, seq // ts, hd + ONES_ROWS, ts), lambda b, h, i: (b, h, 0, 0, 0)),
                  pl.BlockSpec((None, None, 1, tq), lambda b, h, i: (b, h, 0, i)),
                  pl.BlockSpec((None, None, seq, 1), lambda b, h, i: (b, h, 0, 0))],
        out_specs=pl.BlockSpec((None, None, hd, tq), lambda b, h, i: (b, h, 0, i)),
        out_shape=jax.ShapeDtypeStruct((bsz, nh, hd, seq), f32),
        scratch_shapes=[pltpu.VMEM((seq // ts, ts, tq), f32)],
        compiler_params=_params(("parallel", "parallel", "arbitrary")),
        name="fox",
    )(proj, proj, vt, cq, ck)


def _outproj_kernel(oa_ref, ob_ref, x_ref, g1_ref, ga_ref, gb_ref, wa_ref, wb_ref,
                    gf_ref, sc_ref, sh_ref, wq_ref, x1_ref, h2_ref, qp_ref):
    oa = oa_ref[...]
    ob = ob_ref[...].reshape(WIDTH_B, oa.shape[0]).T
    na = oa * lax.rsqrt(jnp.mean(oa * oa, axis=-1, keepdims=True) + EPS) * ga_ref[...]
    nb = ob * lax.rsqrt(jnp.mean(ob * ob, axis=-1, keepdims=True) + EPS) * gb_ref[...]
    res = (jnp.dot(na.astype(bf16), wa_ref[...], preferred_element_type=f32)
           + jnp.dot(nb.astype(bf16), wb_ref[...], preferred_element_type=f32))
    x1 = x_ref[...] + g1_ref[0] * res
    x1_ref[...] = x1
    h2 = x1 * lax.rsqrt(jnp.mean(x1 * x1, axis=-1, keepdims=True) + EPS) * gf_ref[...]
    h2 = h2 * (1.0 + sc_ref[0]) + sh_ref[0]
    h2_ref[...] = h2
    qp_ref[...] = jnp.dot(h2.astype(bf16), wq_ref[...], preferred_element_type=f32)


def _outproj_call(oa, ob, x2, g1, ga, gb, wa, wb, gf, sc, sh, wq, seq, tm):
    n, d = x2.shape
    per_b = seq // tm
    tok = lambda i: (i, 0)
    bat = lambda i: (i // per_b, 0, 0)
    cst = lambda i: (0, 0)
    nq = wq.shape[1]
    return pl.pallas_call(
        _outproj_kernel,
        grid=(n // tm,),
        in_specs=[pl.BlockSpec((tm, WIDTH_A), tok),
                  pl.BlockSpec((None, N_HEADS_B, HEAD_DIM_B, tm), lambda i: (i // per_b, 0, 0, i % per_b)),
                  pl.BlockSpec((tm, d), tok), pl.BlockSpec((1, 1, d), bat),
                  pl.BlockSpec((1, WIDTH_A), cst), pl.BlockSpec((1, WIDTH_B), cst),
                  pl.BlockSpec((WIDTH_A, d), cst), pl.BlockSpec((WIDTH_B, d), cst),
                  pl.BlockSpec((1, d), cst), pl.BlockSpec((1, 1, d), bat),
                  pl.BlockSpec((1, 1, d), bat), pl.BlockSpec((d, nq), cst)],
        out_specs=[pl.BlockSpec((tm, d), tok), pl.BlockSpec((tm, d), tok),
                   pl.BlockSpec((tm, nq), tok)],
        out_shape=[jax.ShapeDtypeStruct((n, d), f32), jax.ShapeDtypeStruct((n, d), f32),
                   jax.ShapeDtypeStruct((n, nq), f32)],
        compiler_params=_params(("parallel",)),
        name="outproj",
    )(oa, ob, x2, g1, ga, gb, wa, wb, gf, sc, sh, wq)


def _argmax_rows(x, iota):
    vals = [x[j:j + 8] for j in range(0, x.shape[0], 8)]
    idxs = [iota[j:j + 8] for j in range(0, x.shape[0], 8)]
    while len(vals) > 1:
        nv, ni = [], []
        for a in range(0, len(vals) - 1, 2):
            keep = vals[a] >= vals[a + 1]
            nv.append(jnp.where(keep, vals[a], vals[a + 1]))
            ni.append(jnp.where(keep, idxs[a], idxs[a + 1]))
        if len(vals) % 2:
            nv.append(vals[-1])
            ni.append(idxs[-1])
        vals, idxs = nv, ni
    m = jnp.max(vals[0], axis=0, keepdims=True)
    pos = jnp.min(jnp.where(vals[0] == m, idxs[0], x.shape[0]), axis=0, keepdims=True)
    return m, pos


def _topk_rows(x, kk):
    iota = lax.broadcasted_iota(i32, x.shape, 0)
    vals, idxs = [], []
    for _ in range(kk):
        m, pos = _argmax_rows(x, iota)
        vals.append(m)
        idxs.append(pos)
        x = jnp.where(iota == pos, -jnp.inf, x)
    return jnp.concatenate(vals, axis=0), jnp.concatenate(idxs, axis=0)


CAND_PAIRS = [(a, b) for a in range(PEER_TOPK) for b in range(PEER_TOPK) if (a + 1) * (b + 1) <= PEER_TOPK]
CAND_ROWS = -(-len(CAND_PAIRS) // 8) * 8
ROUTE_UNROLL = 4


def _cand_select():
    sel = np.zeros((2, CAND_ROWS, PEER_TOPK), np.float32)
    for r, (a, b) in enumerate(CAND_PAIRS):
        sel[0, r, a] = 1.0
        sel[1, r, b] = 1.0
    return sel


def _route_kernel(qp_ref, k1_ref, k2_ref, sel_ref, ids_ref, g_ref, *, nl):
    half = PEER_KEY_DIM // 2
    hp = lax.Precision.HIGHEST
    pick = lambda j, v: jnp.dot(sel_ref[j], v, preferred_element_type=f32, precision=hp)
    rows = lax.broadcasted_iota(i32, (CAND_ROWS, LANES), 0)

    def lane_tile(l):
        q = qp_ref[pl.ds(pl.multiple_of(l * LANES, LANES), LANES), :]
        s1 = lax.dot_general(k1_ref[...], q[:, :half], NT_DIMS, preferred_element_type=f32, precision=hp)
        s2 = lax.dot_general(k2_ref[...], q[:, half:], NT_DIMS, preferred_element_type=f32, precision=hp)
        v1, i1 = _topk_rows(s1, PEER_TOPK)
        v2, i2 = _topk_rows(s2, PEER_TOPK)
        cand = jnp.where(rows < len(CAND_PAIRS), pick(0, v1) + pick(1, v2), -jnp.inf)
        cand_e = pick(0, i1.astype(f32)) * N_KEYS + pick(1, i2.astype(f32))
        best, experts = [], []
        for _ in range(PEER_TOPK):
            m, pos = _argmax_rows(cand, rows)
            hit = rows == pos
            experts.append(jnp.sum(jnp.where(hit, cand_e, 0.0), axis=0, keepdims=True))
            best.append(m)
            cand = jnp.where(hit, -jnp.inf, cand)
        best = jnp.concatenate(best, axis=0)
        e = jnp.exp(best - best[0:1])
        g_ref[l] = e / jnp.sum(e, axis=0, keepdims=True)
        ids_ref[l] = jnp.concatenate(experts, axis=0).astype(i32)

    def group(gi, carry):
        for j in range(ROUTE_UNROLL):
            lane_tile(gi * ROUTE_UNROLL + j)
        return carry

    lax.fori_loop(0, nl // ROUTE_UNROLL, group, 0)


def _route_call(qp, k1, k2, tr):
    n = qp.shape[0]
    nl = tr // LANES
    sel = jnp.asarray(_cand_select())
    kern = functools.partial(_route_kernel, nl=nl)
    return pl.pallas_call(
        kern,
        grid=(n // tr, PEER_HEADS),
        in_specs=[pl.BlockSpec((tr, PEER_KEY_DIM), lambda i, h: (i, h)),
                  pl.BlockSpec(k1.shape, lambda i, h: (0, 0)),
                  pl.BlockSpec(k2.shape, lambda i, h: (0, 0)),
                  pl.BlockSpec(sel.shape, lambda i, h: (0, 0, 0))],
        out_specs=[pl.BlockSpec((nl, PEER_TOPK, LANES), lambda i, h: (i, h, 0)),
                   pl.BlockSpec((nl, PEER_TOPK, LANES), lambda i, h: (i, h, 0))],
        out_shape=[jax.ShapeDtypeStruct((n // LANES, PEER_SLOTS, LANES), i32),
                   jax.ShapeDtypeStruct((n // LANES, PEER_SLOTS, LANES), f32)],
        compiler_params=_params(("parallel", "arbitrary")),
        name="route",
    )(qp, k1, k2, sel)


ROWS_PER_EXPERT = 4
TILE_STRIDE = 136
TOK_UNROLL = 8
HALF = 512


def _pack_table(tab):
    e, d = tab.shape
    bits = lax.bitcast_convert_type(tab.astype(bf16), jnp.uint16).astype(jnp.uint32)
    word = bits[:, :d // 2] | (bits[:, d // 2:] << 16)
    return lax.bitcast_convert_type(word, i32).reshape(e * ROWS_PER_EXPERT, LANES)


def _gather_tile(tab_ref, rows_ref, t, tile_ref):
    tok_rows = rows_ref.at[t]
    for k in range(PEER_SLOTS):
        row = pl.multiple_of(tok_rows[k], ROWS_PER_EXPERT)
        tile_ref[pl.ds(k, ROWS_PER_EXPERT, stride=TILE_STRIDE), :] = tab_ref[pl.ds(row, ROWS_PER_EXPERT), :]
    chunks = [pltpu.bitcast(tile_ref[j * TILE_STRIDE:j * TILE_STRIDE + PEER_SLOTS, :], bf16)
              for j in range(ROWS_PER_EXPERT)]
    return jnp.concatenate(chunks, axis=1)


def _peer_u_kernel(ids_ref, h_ref, g_ref, u_ref, o_ref, tile_scr, act_scr, *, tt):
    even = lax.broadcasted_iota(i32, (1, 2 * PEER_SLOTS), 1) % 2 == 0

    def group(gi, carry):
        rows = []
        for tl in range(TOK_UNROLL):
            t = gi * TOK_UNROLL + tl
            r = _gather_tile(u_ref, ids_ref, t, tile_scr.at[tl % 2])
            h = h_ref[t]
            major = h.astype(bf16)
            minor = (h - major.astype(f32)).astype(bf16)
            out = lax.dot_general(jnp.concatenate([major, minor], axis=0), r, NT_DIMS,
                                  preferred_element_type=f32)
            rows.append(jnp.where(even, out[0:1] + out[2:3], out[1:2] + out[3:4]))
        act_scr[pl.ds(pl.multiple_of(gi * TOK_UNROLL, TOK_UNROLL), TOK_UNROLL), :] = (
            jnp.concatenate(rows, axis=0))
        return carry

    lax.fori_loop(0, tt // TOK_UNROLL, group, 0)
    part = act_scr[...]
    act = part + jnp.where(even, pltpu.roll(part, 2 * PEER_SLOTS - 1, 1), pltpu.roll(part, 1, 1))
    gelu = 0.5 * act * (1.0 + lax.erf(act * (2.0 ** -0.5)))
    o_ref[...] = g_ref[...] * gelu


def _peer_u_call(ids, h3, g2, u_packed, tt):
    n = ids.shape[0]
    kern = functools.partial(_peer_u_kernel, tt=tt)
    return pl.pallas_call(
        kern,
        grid=(n // tt,),
        in_specs=[pl.BlockSpec((tt, PEER_SLOTS), lambda i: (i, 0), memory_space=pltpu.SMEM),
                  pl.BlockSpec((tt, 2, HALF), lambda i: (i, 0, 0)),
                  pl.BlockSpec((tt, 2 * PEER_SLOTS), lambda i: (i, 0)),
                  pl.BlockSpec(u_packed.shape, lambda i: (0, 0), pipeline_mode=pl.Buffered(1))],
        out_specs=pl.BlockSpec((tt, 2 * PEER_SLOTS), lambda i: (i, 0)),
        out_shape=jax.ShapeDtypeStruct((n, 2 * PEER_SLOTS), f32),
        scratch_shapes=[pltpu.VMEM((2, ROWS_PER_EXPERT * TILE_STRIDE, LANES), i32),
                        pltpu.VMEM((tt, 2 * PEER_SLOTS), f32)],
        compiler_params=_params(("arbitrary",)),
        name="peer_u",
    )(ids, h3, g2, u_packed)


def _peer_v_kernel(ids_ref, w_ref, v_ref, o_ref, tile_scr, *, tt):
    even = lax.broadcasted_iota(i32, (1, 2 * PEER_SLOTS), 1) % 2 == 0

    def group(gi, carry):
        w8 = w_ref[pl.ds(pl.multiple_of(gi * TOK_UNROLL, TOK_UNROLL), TOK_UNROLL), :]
        for tl in range(TOK_UNROLL):
            t = gi * TOK_UNROLL + tl
            r = _gather_tile(v_ref, ids_ref, t, tile_scr.at[tl % 2])
            w = w8[tl:tl + 1]
            lhs = jnp.concatenate([jnp.where(even, w, 0.0), jnp.where(even, 0.0, w)], axis=0)
            o_ref[t] = jnp.dot(lhs.astype(bf16), r, preferred_element_type=f32)
        return carry

    lax.fori_loop(0, tt // TOK_UNROLL, group, 0)


def _peer_v_call(ids, wgt2, v_packed, tt):
    n = ids.shape[0]
    kern = functools.partial(_peer_v_kernel, tt=tt)
    return pl.pallas_call(
        kern,
        grid=(n // tt,),
        in_specs=[pl.BlockSpec((tt, PEER_SLOTS), lambda i: (i, 0), memory_space=pltpu.SMEM),
                  pl.BlockSpec((tt, 2 * PEER_SLOTS), lambda i: (i, 0)),
                  pl.BlockSpec(v_packed.shape, lambda i: (0, 0), pipeline_mode=pl.Buffered(1))],
        out_specs=pl.BlockSpec((tt, 2, HALF), lambda i: (i, 0, 0)),
        out_shape=jax.ShapeDtypeStruct((n, 2, HALF), f32),
        scratch_shapes=[pltpu.VMEM((2, ROWS_PER_EXPERT * TILE_STRIDE, LANES), i32)],
        compiler_params=_params(("arbitrary",)),
        name="peer_v",
    )(ids, wgt2, v_packed)


def _final_kernel(x1_ref, p_ref, g2_ref, gf_ref, o_ref):
    x2 = x1_ref[...] + g2_ref[0] * p_ref[...]
    o_ref[...] = x2 * lax.rsqrt(jnp.mean(x2 * x2, axis=-1, keepdims=True) + EPS) * gf_ref[...]


def _final_call(x1, peer, g2, gf, seq, tm):
    n, d = x1.shape
    per_b = seq // tm
    tok = lambda i: (i, 0)
    return pl.pallas_call(
        _final_kernel,
        grid=(n // tm,),
        in_specs=[pl.BlockSpec((tm, d), tok), pl.BlockSpec((tm, d), tok),
                  pl.BlockSpec((1, 1, d), lambda i: (i // per_b, 0, 0)),
                  pl.BlockSpec((1, d), lambda i: (0, 0))],
        out_specs=pl.BlockSpec((tm, d), tok),
        out_shape=jax.ShapeDtypeStruct((n, d), f32),
        compiler_params=_params(("parallel",)),
        name="final",
    )(x1, peer, g2, gf)


def _rope_tables(positions):
    def tab(rot):
        inv = ROPE_THETA ** (-jnp.arange(0, rot, 2, dtype=f32) / rot)
        ang = positions.astype(f32)[..., None] * inv
        return jnp.cos(ang), jnp.sin(ang)

    cos_a, sin_a = tab(ROPE_DIM_A)
    cos_i, sin_i = tab(IDX_ROPE_DIM)
    b, s = positions.shape
    n = b * s
    c_half1 = jnp.tile(cos_a, (1, 1, N_HEADS_A)).reshape(n, 64)
    s_half1 = jnp.tile(sin_a, (1, 1, N_HEADS_A)).reshape(n, 64)
    c1 = jnp.concatenate([c_half1, c_half1], axis=1)
    s1 = jnp.concatenate([-s_half1, s_half1], axis=1)
    pad_c = jnp.ones((n, 20), f32)
    pad_s = jnp.zeros((n, 20), f32)
    c_half2 = jnp.concatenate([jnp.repeat(cos_i, IDX_HEADS, axis=-1).reshape(n, 32),
                               cos_a.reshape(n, 8), cos_i.reshape(n, 4), pad_c], axis=1)
    s_half2 = jnp.concatenate([jnp.repeat(sin_i, IDX_HEADS, axis=-1).reshape(n, 32),
                               sin_a.reshape(n, 8), sin_i.reshape(n, 4), pad_s], axis=1)
    c2 = jnp.concatenate([c_half2, c_half2], axis=1)
    s2 = jnp.concatenate([-s_half2, s_half2], axis=1)
    return c1, s1, c2, s2


def kernel(x, c, positions, w_ada, b_ada, g_mix, w_in, g_kv, w_uk, w_uv, b_forget,
           g_out_a, g_out_b, w_out, g_ffn, w_peer_q, peer_keys1, peer_keys2, peer_u,
           peer_v, g_final):
    b, s, d = x.shape
    n = b * s
    assert w_ada.shape[0] == 1, "single layer supported"
    tm = min(512, s)
    x2 = x.reshape(n, d)

    mod = _mod_call(c, w_ada[0], b_ada[0])
    mod = mod.reshape(b, 6, 1, d)
    shift1, scale1, gate1, shift2, scale2, gate2 = [mod[:, j] for j in range(6)]

    perm = _in_perm()
    w_in_r = jnp.where((perm >= 0)[None, :], w_in[0][:, np.maximum(perm, 0)], 0.0).astype(bf16)
    c1, s1, c2, s2 = _rope_tables(positions)
    proj, kcat, ik, latt, vbt = _inproj_call(x2, scale1, shift1, g_mix[0].reshape(1, d), w_in_r,
                                             c1, s1, c2, s2, g_kv[0].reshape(1, KV_RANK), s, tm)

    wuk = w_uk[0].transpose(1, 0, 2).astype(bf16)
    wuvt = w_uv[0].transpose(1, 2, 0).astype(bf16)
    o_a = _dsa_call(proj, kcat.reshape(b, s, KCAT), latt, ik.reshape(b, s, LANES), wuk, wuvt,
                    b, s, tq=128, tk=tm)

    fl = proj[:, C_MISC + 224:C_MISC + 232].reshape(b, s, N_HEADS_B) + b_forget[0]
    cum = (jnp.cumsum(jax.nn.log_sigmoid(fl), axis=1) * LOG2E).transpose(0, 2, 1)
    o_bt = _fox_call(proj, vbt, cum[:, :, None, :], cum[..., None], tq=min(512, s), ts=tm)

    wo = w_out[0].astype(bf16)
    x1, h2, qp = _outproj_call(o_a, o_bt, x2, gate1, g_out_a[0].reshape(1, WIDTH_A),
                               g_out_b[0].reshape(1, WIDTH_B), wo[:WIDTH_A], wo[WIDTH_A:],
                               g_ffn[0].reshape(1, d), scale2, shift2,
                               w_peer_q[0].astype(bf16), s, tm)

    ids_t, g_t = _route_call(qp, peer_keys1[0], peer_keys2[0], tr=min(1024, n))
    token_major = lambda a: a.transpose(0, 2, 1).reshape(n, PEER_SLOTS)
    ids = token_major(ids_t) * ROWS_PER_EXPERT
    gates2 = jnp.repeat(token_major(g_t), 2, axis=1)
    tt = 32
    wgt2 = _peer_u_call(ids, h2.reshape(n, 2, HALF), gates2, _pack_table(peer_u[0]), tt)
    peer = _peer_v_call(ids, wgt2, _pack_table(peer_v[0]), tt).reshape(n, d)

    out = _final_call(x1, peer, gate2, g_final.reshape(1, d), s, tm)
    return out.reshape(b, s, d)
```

```python
import functools

import jax
import jax.numpy as jnp
import numpy as np
from jax import lax
from jax.experimental import pallas as pl
from jax.experimental.pallas import tpu as pltpu

f32 = jnp.float32
bf16 = jnp.bfloat16
i32 = jnp.int32

N_HEADS_A = 8
HEAD_DIM_A = 64
ROPE_DIM_A = 16
NOPE_DIM_A = HEAD_DIM_A - ROPE_DIM_A
KV_RANK = 128
IDX_HEADS = 8
IDX_DIM = 32
IDX_ROPE_DIM = 8
TOPK_MAX = 256
N_HEADS_B = 8
HEAD_DIM_B = 64
WIDTH_A = N_HEADS_A * HEAD_DIM_A
WIDTH_B = N_HEADS_B * HEAD_DIM_B
ROPE_THETA = 500000.0
PEER_HEADS = 8
N_KEYS = 128
PEER_KEY_DIM = 128
PEER_TOPK = 16
PEER_SLOTS = PEER_HEADS * PEER_TOPK
EPS = 1e-6
IN_SIZES = (WIDTH_A, KV_RANK, ROPE_DIM_A, IDX_HEADS * IDX_DIM, IDX_DIM, IDX_HEADS,
            WIDTH_B, WIDTH_B, WIDTH_B, N_HEADS_B)

LANES = 128
NEG_BIG = -1e30
INT_MIN = -2147483648
VMEM_LIMIT = 56 * 1024 * 1024

C_R1 = 0
C_R2 = 128
C_LAT = 256
C_QN = 384
C_MISC = 768
C_QB = 1024
C_KB = 1536
C_VB = 2048
C_TOTAL = 2560

NT_DIMS = (((1,), (1,)), ((), ()))


def _in_perm():
    offs = np.cumsum((0,) + IN_SIZES)
    o_qa, o_lat, o_kr, o_iq, o_ik, o_iw, o_qb, o_kb, o_vb, o_fb = offs[:10]
    perm = -np.ones((C_TOTAL,), np.int64)
    ha, hi = ROPE_DIM_A // 2, IDX_ROPE_DIM // 2
    for h in range(N_HEADS_A):
        for j in range(ha):
            perm[C_R1 + h * ha + j] = o_qa + h * HEAD_DIM_A + j
            perm[C_R1 + 64 + h * ha + j] = o_qa + h * HEAD_DIM_A + ha + j
        for j in range(NOPE_DIM_A):
            perm[C_QN + h * NOPE_DIM_A + j] = o_qa + h * HEAD_DIM_A + ROPE_DIM_A + j
    for h in range(IDX_HEADS):
        for j in range(hi):
            perm[C_R2 + j * IDX_HEADS + h] = o_iq + h * IDX_DIM + j
            perm[C_R2 + 64 + j * IDX_HEADS + h] = o_iq + h * IDX_DIM + hi + j
        for j in range(IDX_DIM - IDX_ROPE_DIM):
            perm[C_MISC + j * IDX_HEADS + h] = o_iq + h * IDX_DIM + IDX_ROPE_DIM + j
    for j in range(ha):
        perm[C_R2 + 32 + j] = o_kr + j
        perm[C_R2 + 64 + 32 + j] = o_kr + ha + j
    for j in range(hi):
        perm[C_R2 + 40 + j] = o_ik + j
        perm[C_R2 + 64 + 40 + j] = o_ik + hi + j
    for j in range(IDX_DIM - IDX_ROPE_DIM):
        perm[C_MISC + 192 + j] = o_ik + IDX_ROPE_DIM + j
    for j in range(IDX_HEADS):
        perm[C_MISC + 216 + j] = o_iw + j
    for j in range(N_HEADS_B):
        perm[C_MISC + 224 + j] = o_fb + j
    perm[C_LAT:C_LAT + KV_RANK] = o_lat + np.arange(KV_RANK)
    perm[C_QB:C_QB + WIDTH_B] = o_qb + np.arange(WIDTH_B)
    perm[C_KB:C_KB + WIDTH_B] = o_kb + np.arange(WIDTH_B)
    perm[C_VB:C_VB + WIDTH_B] = o_vb + np.arange(WIDTH_B)
    return perm


def _params(sem):
    return pltpu.CompilerParams(dimension_semantics=sem, vmem_limit_bytes=VMEM_LIMIT)


def _mod_kernel(c_ref, w_ref, b_ref, o_ref):
    c = c_ref[...]
    ca = c * jax.nn.sigmoid(c)
    o_ref[...] = jnp.dot(ca, w_ref[...], preferred_element_type=f32,
                         precision=lax.Precision.HIGHEST) + b_ref[...]


def _mod_call(c, w, b):
    bsz, d = c.shape
    n = w.shape[1]
    return pl.pallas_call(
        _mod_kernel,
        grid=(n // d,),
        in_specs=[pl.BlockSpec((bsz, d), lambda j: (0, 0)),
                  pl.BlockSpec((d, d), lambda j: (0, j)),
                  pl.BlockSpec((1, d), lambda j: (0, j))],
        out_specs=pl.BlockSpec((bsz, d), lambda j: (0, j)),
        out_shape=jax.ShapeDtypeStruct((bsz, n), f32),
        compiler_params=_params(("arbitrary",)),
        name="mod",
    )(c, w, b.reshape(1, n))


KCAT = 256
ONES_ROWS = 16


def _with_ones_rows(vt):
    lead = vt.shape[:-2]
    t = vt.shape[-1]
    ones = jnp.ones(lead + (1, t), vt.dtype)
    zeros = jnp.zeros(lead + (ONES_ROWS - 1, t), vt.dtype)
    return jnp.concatenate([vt, ones, zeros], axis=-2)


def _lanes_from(lane, pieces):
    out = jnp.zeros(lane.shape, f32)
    for end, src, start in reversed(pieces):
        begin = max([e for e, _, _ in pieces if e < end], default=0)
        out = jnp.where(lane < end, pltpu.roll(src, (begin - start) % LANES, 1), out)
    return out


TRIG = 32


def _trig_expand():
    e = np.zeros((TRIG, 4 * LANES), np.float32)
    cos_a, cos_i, one, sin_a, sin_i = 0, 8, 12, 16, 24
    for lane in range(LANES):
        l, sign = lane % 64, (-1.0 if lane < 64 else 1.0)
        e[cos_a + l % 8, lane] = 1.0
        e[sin_a + l % 8, LANES + lane] = sign
        if l < 32:
            e[cos_i + l // IDX_HEADS, 2 * LANES + lane] = 1.0
            e[sin_i + l // IDX_HEADS, 3 * LANES + lane] = sign
        elif l < 40:
            e[cos_a + l - 32, 2 * LANES + lane] = 1.0
            e[sin_a + l - 32, 3 * LANES + lane] = sign
        elif l < 44:
            e[cos_i + l - 40, 2 * LANES + lane] = 1.0
            e[sin_i + l - 40, 3 * LANES + lane] = sign
        else:
            e[one, 2 * LANES + lane] = 1.0
    return e


def _inproj_kernel(x_ref, sc_ref, sh_ref, g_ref, w_ref, trig_ref, exp_ref,
                   gkv_ref, o_ref, kcat_ref, ik_ref, latt_ref, vbt_ref):
    x = x_ref[...]
    ms = jnp.mean(x * x, axis=-1, keepdims=True)
    h = x * lax.rsqrt(ms + EPS) * g_ref[...]
    h = h * (1.0 + sc_ref[0]) + sh_ref[0]
    p = jnp.dot(h.astype(bf16), w_ref[...], preferred_element_type=f32)
    tab = jnp.dot(trig_ref[...], exp_ref[...], preferred_element_type=f32,
                  precision=lax.Precision.HIGHEST)
    r1 = p[:, C_R1:C_R1 + LANES]
    o_ref[:, C_R1:C_R1 + LANES] = (r1 * tab[:, 0:LANES]
                                   + pltpu.roll(r1, 64, 1) * tab[:, LANES:2 * LANES])
    r2 = p[:, C_R2:C_R2 + LANES]
    r2 = r2 * tab[:, 2 * LANES:3 * LANES] + pltpu.roll(r2, 64, 1) * tab[:, 3 * LANES:]
    o_ref[:, C_R2:C_R2 + LANES] = r2
    lat = p[:, C_LAT:C_LAT + KV_RANK]
    lms = jnp.mean(lat * lat, axis=-1, keepdims=True)
    lat = lat * lax.rsqrt(lms + EPS) * gkv_ref[...]
    o_ref[:, C_LAT:C_LAT + KV_RANK] = lat
    o_ref[:, C_QN:] = p[:, C_QN:]

    lane = lax.broadcasted_iota(i32, r2.shape, 1)
    kr = _lanes_from(lane, [(8, r2, 32), (16, r2, 96)])
    kcat_ref[...] = jnp.concatenate([lat, kr], axis=1).astype(bf16)
    misc_hi = p[:, C_MISC + LANES:C_MISC + 2 * LANES]
    ik_ref[...] = _lanes_from(lane, [(4, r2, 40), (8, r2, 104), (32, misc_hi, 64)]).astype(bf16)
    latt_ref[...] = _with_ones_rows(lat.T.astype(bf16))
    vt = p[:, C_VB:C_VB + WIDTH_B].T.astype(bf16)
    vbt_ref[...] = _with_ones_rows(vt.reshape(N_HEADS_B, HEAD_DIM_B, vt.shape[1]))


def _inproj_call(x2, sc, sh, g, w, trig, gkv, seq, tm):
    n, d = x2.shape
    per_b = seq // tm
    bsz = n // seq
    tok = lambda i: (i, 0)
    bat = lambda i: (i // per_b, 0, 0)
    cst = lambda i: (0, 0)
    return pl.pallas_call(
        _inproj_kernel,
        grid=(n // tm,),
        in_specs=[pl.BlockSpec((tm, d), tok),
                  pl.BlockSpec((1, 1, d), bat),
                  pl.BlockSpec((1, 1, d), bat),
                  pl.BlockSpec((1, d), cst),
                  pl.BlockSpec((d, C_TOTAL), cst),
                  pl.BlockSpec((tm, TRIG), tok),
                  pl.BlockSpec((TRIG, 4 * LANES), cst),
                  pl.BlockSpec((1, KV_RANK), cst)],
        out_specs=[pl.BlockSpec((tm, C_TOTAL), tok),
                   pl.BlockSpec((tm, KCAT), tok),
                   pl.BlockSpec((tm, LANES), tok),
                   pl.BlockSpec((None, None, KV_RANK + ONES_ROWS, tm),
                                lambda i: (i // per_b, i % per_b, 0, 0)),
                   pl.BlockSpec((None, N_HEADS_B, None, HEAD_DIM_B + ONES_ROWS, tm),
                                lambda i: (i // per_b, 0, i % per_b, 0, 0))],
        out_shape=[jax.ShapeDtypeStruct((n, C_TOTAL), f32),
                   jax.ShapeDtypeStruct((n, KCAT), bf16),
                   jax.ShapeDtypeStruct((n, LANES), bf16),
                   jax.ShapeDtypeStruct((bsz, per_b, KV_RANK + ONES_ROWS, tm), bf16),
                   jax.ShapeDtypeStruct((bsz, N_HEADS_B, per_b, HEAD_DIM_B + ONES_ROWS, tm), bf16)],
        compiler_params=_params(("parallel",)),
        name="inproj",
    )(x2, sc, sh, g, w, trig, jnp.asarray(_trig_expand()), gkv)


LOG2E = 1.4426950408889634


def _sortable(x):
    bits = lax.bitcast_convert_type(x, i32)
    return bits ^ ((bits >> 31) & 0x7FFFFFFF)


def _dsa_kernel(r1_ref, r2_ref, qn_ref, misc_ref, kcat_ref, latt_ref, ik_ref, wuk_ref, wuvt_ref,
                o_ref, key_scr, iq_scr, m_scr, acc_scr, *, tq, tk, topk, nbits, seq):
    nh = N_HEADS_A
    qi = pl.program_id(1)
    nc = ((qi + 1) * tq + tk - 1) // tk
    qpos = qi * tq + lax.broadcasted_iota(i32, (tk, tq), 1)
    krow = lax.broadcasted_iota(i32, (tk, tq), 0)

    r2t = r2_ref[...].T
    mt = misc_ref[...].T
    iq_scr[...] = jnp.concatenate([r2t[0:32], r2t[64:96], mt[0:192]], axis=0)
    iqt = jnp.concatenate([iq_scr[pl.ds(h, IDX_DIM, stride=IDX_HEADS), :] for h in range(IDX_HEADS)],
                          axis=1)
    iqt = jnp.concatenate([iqt, jnp.zeros((LANES - IDX_DIM, IDX_HEADS * tq), f32)], axis=0).astype(bf16)
    iwt = mt[216:224] * (IDX_DIM ** -0.5 * IDX_HEADS ** -0.5)

    def score_chunk(c, carry):
        start = pl.multiple_of(c * tk, tk)
        d = jnp.dot(ik_ref[pl.ds(start, tk), :], iqt, preferred_element_type=f32)
        sc = jnp.zeros((tk, tq), f32)
        for h in range(IDX_HEADS):
            sc = sc + jnp.maximum(d[:, h * tq:(h + 1) * tq], 0.0) * iwt[h:h + 1, :]
        sc = jnp.where(start + krow <= qpos, sc, -jnp.inf)
        key_scr[c] = _sortable(sc)
        return carry

    lax.fori_loop(0, nc, score_chunk, 0)

    def count(pred):
        def body(c, acc):
            m = pred(key_scr[c], c * tk + krow)
            return acc + jnp.sum(m.reshape(tk // 8, 8, tq), axis=0)
        acc = lax.fori_loop(0, nc, body, jnp.zeros((8, tq), i32))
        return jnp.sum(acc, axis=0, keepdims=True)

    def bit_body(i, thr):
        cand = thr + lax.shift_left(jnp.int32(1), 31 - i)
        cnt = count(lambda k, col: jnp.where(k >= cand, 1, 0))
        return jnp.where(cnt >= topk, cand, thr)

    thr = lax.fori_loop(0, 32, bit_body, jnp.full((1, tq), INT_MIN, i32))
    need = topk - count(lambda k, col: jnp.where(k > thr, 1, 0))
    n_eq = count(lambda k, col: jnp.where(k == thr, 1, 0))

    def tie_search():
        def tie_body(i, last):
            cand = last + lax.shift_left(jnp.int32(1), nbits - 1 - i)
            below = count(lambda k, col: jnp.where(k == thr, jnp.where(col < cand, 1, 0), 0))
            return jnp.where(below < need, cand, last)
        return lax.fori_loop(0, nbits, tie_body, jnp.zeros((1, tq), i32))

    ambiguous = jnp.max(jnp.where(n_eq > need, 1, 0)) > 0
    last = lax.cond(ambiguous, tie_search, lambda: jnp.full((1, tq), seq, i32))

    scale = HEAD_DIM_A ** -0.5 * LOG2E
    r1t = r1_ref[...].T
    qnt = qn_ref[...].T.astype(bf16)
    pad = jnp.zeros((KCAT - KV_RANK - ROPE_DIM_A, tq), f32)
    cols = []
    for h in range(nh):
        ql = jnp.dot(wuk_ref[h], qnt[h * NOPE_DIM_A:(h + 1) * NOPE_DIM_A], preferred_element_type=f32)
        cols.append(jnp.concatenate([ql, r1t[h * 8:(h + 1) * 8], r1t[64 + h * 8:64 + (h + 1) * 8], pad],
                                    axis=0))
    qcat = (jnp.concatenate(cols, axis=1) * scale).astype(bf16)

    m_scr[...] = jnp.full(m_scr.shape, NEG_BIG, f32)
    acc_scr[...] = jnp.zeros(acc_scr.shape, f32)

    def attend_chunk(c, carry):
        start = pl.multiple_of(c * tk, tk)
        s = jnp.dot(kcat_ref[pl.ds(start, tk), :], qcat, preferred_element_type=f32)
        k = key_scr[c]
        col = start + krow
        tie = jnp.where(k == thr, jnp.where(col <= last, 0.0, NEG_BIG), NEG_BIG)
        bias = jnp.where(col <= qpos, jnp.where(k > thr, 0.0, tie), NEG_BIG)
        s = s + jnp.concatenate([bias] * nh, axis=1)
        m_old = m_scr[...]
        m_new = jnp.maximum(m_old, jnp.max(s, axis=0, keepdims=True))
        p = jnp.exp2(s - m_new)
        alpha = jnp.exp2(m_old - m_new)
        acc_scr[...] = alpha * acc_scr[...] + jnp.dot(latt_ref[c], p.astype(bf16),
                                                      preferred_element_type=f32)
        m_scr[...] = m_new
        return carry

    lax.fori_loop(0, nc, attend_chunk, 0)

    ot = (acc_scr[0:KV_RANK, :] / acc_scr[KV_RANK:KV_RANK + 1, :]).astype(bf16)
    outs = [jnp.dot(wuvt_ref[h], ot[:, h * tq:(h + 1) * tq], preferred_element_type=f32)
            for h in range(nh)]
    o_ref[...] = jnp.concatenate(outs, axis=0).T


def _dsa_call(proj, kcat, latt, ik, wuk, wuvt, bsz, seq, tq, tk):
    nh = N_HEADS_A
    topk = min(TOPK_MAX, seq // 4)
    nbits = max(1, (seq - 1).bit_length())
    nq = seq // tq
    kern = functools.partial(_dsa_kernel, tq=tq, tk=tk, topk=topk, nbits=nbits, seq=seq)
    qcol = lambda width, blk: pl.BlockSpec((tq, width), lambda b, i: (b * nq + i, blk))
    return pl.pallas_call(
        kern,
        grid=(bsz, nq),
        in_specs=[qcol(LANES, C_R1 // LANES), qcol(LANES, C_R2 // LANES),
                  qcol(C_MISC - C_QN, C_QN // (C_MISC - C_QN)), qcol(256, C_MISC // 256),
                  pl.BlockSpec((None, seq, KCAT), lambda b, i: (b, 0, 0)),
                  pl.BlockSpec((None, seq // tk, KV_RANK + ONES_ROWS, tk), lambda b, i: (b, 0, 0, 0)),
                  pl.BlockSpec((None, seq, LANES), lambda b, i: (b, 0, 0)),
                  pl.BlockSpec(wuk.shape, lambda b, i: (0, 0, 0)),
                  pl.BlockSpec(wuvt.shape, lambda b, i: (0, 0, 0))],
        out_specs=pl.BlockSpec((tq, WIDTH_A), lambda b, i: (b * nq + i, 0)),
        out_shape=jax.ShapeDtypeStruct((bsz * seq, WIDTH_A), f32),
        scratch_shapes=[pltpu.VMEM((seq // tk, tk, tq), i32),
                        pltpu.VMEM((IDX_HEADS * IDX_DIM, tq), f32),
                        pltpu.VMEM((1, nh * tq), f32),
                        pltpu.VMEM((KV_RANK + ONES_ROWS, nh * tq), f32)],
        compiler_params=_params(("parallel", "arbitrary")),
        name="dsa",
    )(proj, proj, proj, proj, kcat, latt, ik, wuk, wuvt)


def _fox_kernel(q_ref, k_ref, vt_ref, cq_ref, ck_ref, o_ref, s_scr, *, tq, ts):
    qi = pl.program_id(2)
    nc = ((qi + 1) * tq + ts - 1) // ts
    mine = lax.broadcasted_iota(i32, (LANES, tq), 0) // HEAD_DIM_B == pl.program_id(1) % 2
    qt = jnp.where(mine, q_ref[...].T * (HEAD_DIM_B ** -0.5 * LOG2E), 0.0).astype(bf16)
    cq = cq_ref[...]
    qpos = qi * tq + lax.broadcasted_iota(i32, (ts, tq), 1)
    krow = lax.broadcasted_iota(i32, (ts, tq), 0)

    def score(c, m, masked):
        start = pl.multiple_of(c * ts, ts)
        s = jnp.dot(k_ref[pl.ds(start, ts), :].astype(bf16), qt, preferred_element_type=f32)
        s = s + cq - ck_ref[pl.ds(start, ts), :]
        if masked:
            s = jnp.where(start + krow <= qpos, s, NEG_BIG)
        s_scr[c] = s
        return jnp.maximum(m, jnp.max(s, axis=0, keepdims=True))

    nfull = (qi * tq) // ts
    m = lax.fori_loop(0, nfull, lambda c, m: score(c, m, False), jnp.full((1, tq), NEG_BIG, f32))
    m = lax.fori_loop(nfull, nc, lambda c, m: score(c, m, True), m)

    def attend(c, acc):
        p = jnp.exp2(s_scr[c] - m)
        return acc + jnp.dot(vt_ref[c], p.astype(bf16), preferred_element_type=f32)

    acc = lax.fori_loop(0, nc, attend, jnp.zeros((HEAD_DIM_B + ONES_ROWS, tq), f32))
    o_ref[...] = acc[0:HEAD_DIM_B] / acc[HEAD_DIM_B:HEAD_DIM_B + 1]


def _fox_call(proj, vt, cq, ck, tq, ts):
    bsz, nh, _, _, _ = vt.shape
    hd = HEAD_DIM_B
    seq = proj.shape[0] // bsz
    nq = seq // tq
    kern = functools.partial(_fox_kernel, tq=tq, ts=ts)
    return pl.pallas_call(
        kern,
        grid=(bsz, nh, nq),
        in_specs=[pl.BlockSpec((tq, LANES), lambda b, h, i: (b * nq + i, C_QB // LANES + h // 2)),
                  pl.BlockSpec((seq, LANES), lambda b, h, i: (b, C_KB // LANES + h // 2)),
                  pl.BlockSpec((None, None, seq // ts, hd + ONES_ROWS, ts), lambda b, h, i: (b, h, 0, 0, 0)),
                  pl.BlockSpec((None, None, 1, tq), lambda b, h, i: (b, h, 0, i)),
                  pl.BlockSpec((None, None, seq, 1), lambda b, h, i: (b, h, 0, 0))],
        out_specs=pl.BlockSpec((None, None, hd, tq), lambda b, h, i: (b, h, 0, i)),
        out_shape=jax.ShapeDtypeStruct((bsz, nh, hd, seq), f32),
        scratch_shapes=[pltpu.VMEM((seq // ts, ts, tq), f32)],
        compiler_params=_params(("parallel", "parallel", "arbitrary")),
        name="fox",
    )(proj, proj, vt, cq, ck)


def _outproj_kernel(oa_ref, ob_ref, x_ref, g1_ref, ga_ref, gb_ref, wa_ref, wb_ref,
                    gf_ref, sc_ref, sh_ref, wq_ref, x1_ref, h2_ref, qp_ref):
    oa = oa_ref[...]
    ob = ob_ref[...].reshape(WIDTH_B, oa.shape[0]).T
    na = oa * lax.rsqrt(jnp.mean(oa * oa, axis=-1, keepdims=True) + EPS) * ga_ref[...]
    nb = ob * lax.rsqrt(jnp.mean(ob * ob, axis=-1, keepdims=True) + EPS) * gb_ref[...]
    res = (jnp.dot(na.astype(bf16), wa_ref[...], preferred_element_type=f32)
           + jnp.dot(nb.astype(bf16), wb_ref[...], preferred_element_type=f32))
    x1 = x_ref[...] + g1_ref[0] * res
    x1_ref[...] = x1
    h2 = x1 * lax.rsqrt(jnp.mean(x1 * x1, axis=-1, keepdims=True) + EPS) * gf_ref[...]
    h2 = h2 * (1.0 + sc_ref[0]) + sh_ref[0]
    h2_ref[...] = h2
    qp_ref[...] = jnp.dot(h2.astype(bf16), wq_ref[...], preferred_element_type=f32)


def _outproj_call(oa, ob, x2, g1, ga, gb, wa, wb, gf, sc, sh, wq, seq, tm):
    n, d = x2.shape
    per_b = seq // tm
    tok = lambda i: (i, 0)
    bat = lambda i: (i // per_b, 0, 0)
    cst = lambda i: (0, 0)
    nq = wq.shape[1]
    return pl.pallas_call(
        _outproj_kernel,
        grid=(n // tm,),
        in_specs=[pl.BlockSpec((tm, WIDTH_A), tok),
                  pl.BlockSpec((None, N_HEADS_B, HEAD_DIM_B, tm), lambda i: (i // per_b, 0, 0, i % per_b)),
                  pl.BlockSpec((tm, d), tok), pl.BlockSpec((1, 1, d), bat),
                  pl.BlockSpec((1, WIDTH_A), cst), pl.BlockSpec((1, WIDTH_B), cst),
                  pl.BlockSpec((WIDTH_A, d), cst), pl.BlockSpec((WIDTH_B, d), cst),
                  pl.BlockSpec((1, d), cst), pl.BlockSpec((1, 1, d), bat),
                  pl.BlockSpec((1, 1, d), bat), pl.BlockSpec((d, nq), cst)],
        out_specs=[pl.BlockSpec((tm, d), tok), pl.BlockSpec((tm, d), tok),
                   pl.BlockSpec((tm, nq), tok)],
        out_shape=[jax.ShapeDtypeStruct((n, d), f32), jax.ShapeDtypeStruct((n, d), f32),
                   jax.ShapeDtypeStruct((n, nq), f32)],
        compiler_params=_params(("parallel",)),
        name="outproj",
    )(oa, ob, x2, g1, ga, gb, wa, wb, gf, sc, sh, wq)


def _argmax_rows(x, iota):
    vals = [x[j:j + 8] for j in range(0, x.shape[0], 8)]
    idxs = [iota[j:j + 8] for j in range(0, x.shape[0], 8)]
    while len(vals) > 1:
        nv, ni = [], []
        for a in range(0, len(vals) - 1, 2):
            keep = vals[a] >= vals[a + 1]
            nv.append(jnp.where(keep, vals[a], vals[a + 1]))
            ni.append(jnp.where(keep, idxs[a], idxs[a + 1]))
        if len(vals) % 2:
            nv.append(vals[-1])
            ni.append(idxs[-1])
        vals, idxs = nv, ni
    m = jnp.max(vals[0], axis=0, keepdims=True)
    pos = jnp.min(jnp.where(vals[0] == m, idxs[0], x.shape[0]), axis=0, keepdims=True)
    return m, pos


def _topk_rows(x, kk):
    iota = lax.broadcasted_iota(i32, x.shape, 0)
    vals, idxs = [], []
    for _ in range(kk):
        m, pos = _argmax_rows(x, iota)
        vals.append(m)
        idxs.append(pos)
        x = jnp.where(iota == pos, -jnp.inf, x)
    return jnp.concatenate(vals, axis=0), jnp.concatenate(idxs, axis=0)


CAND_PAIRS = [(a, b) for a in range(PEER_TOPK) for b in range(PEER_TOPK) if (a + 1) * (b + 1) <= PEER_TOPK]
CAND_ROWS = -(-len(CAND_PAIRS) // 8) * 8


def _cand_select():
    sel = np.zeros((2, CAND_ROWS, PEER_TOPK), np.float32)
    for r, (a, b) in enumerate(CAND_PAIRS):
        sel[0, r, a] = 1.0
        sel[1, r, b] = 1.0
    return sel


def _route_kernel(qp_ref, k1_ref, k2_ref, sel_ref, dup_ref, rows_ref, g_ref, *, nl):
    half = PEER_KEY_DIM // 2
    hp = lax.Precision.HIGHEST
    pick = lambda j, v: jnp.dot(sel_ref[j], v, preferred_element_type=f32, precision=hp)
    rows = lax.broadcasted_iota(i32, (CAND_ROWS, LANES), 0)

    def head(q):
        s1 = lax.dot_general(k1_ref[...], q[:, :half], NT_DIMS, preferred_element_type=f32, precision=hp)
        s2 = lax.dot_general(k2_ref[...], q[:, half:], NT_DIMS, preferred_element_type=f32, precision=hp)
        v1, i1 = _topk_rows(s1, PEER_TOPK)
        v2, i2 = _topk_rows(s2, PEER_TOPK)
        cand = jnp.where(rows < len(CAND_PAIRS), pick(0, v1) + pick(1, v2), -jnp.inf)
        cand_e = pick(0, i1.astype(f32)) * N_KEYS + pick(1, i2.astype(f32))
        best, experts = [], []
        for _ in range(PEER_TOPK):
            m, pos = _argmax_rows(cand, rows)
            hit = rows == pos
            experts.append(jnp.sum(jnp.where(hit, cand_e, 0.0), axis=0, keepdims=True))
            best.append(m)
            cand = jnp.where(hit, -jnp.inf, cand)
        best = jnp.concatenate(best, axis=0)
        e = jnp.exp(best - best[0:1])
        return jnp.concatenate(experts, axis=0), e / jnp.sum(e, axis=0, keepdims=True)

    def lane_tile(l, carry):
        start = pl.multiple_of(l * LANES, LANES)
        outs = [head(qp_ref[pl.ds(start, LANES), h * PEER_KEY_DIM:(h + 1) * PEER_KEY_DIM])
                for h in range(PEER_HEADS)]
        ids = jnp.concatenate([o[0] for o in outs], axis=0)
        gates = jnp.concatenate([o[1] for o in outs], axis=0)
        rows_ref[pl.ds(start, LANES), :] = (ids.T * ROWS_PER_EXPERT).astype(i32)
        g_ref[pl.ds(start, LANES), :] = jnp.dot(dup_ref[...], gates, preferred_element_type=f32,
                                                precision=hp).T
        return carry

    lax.fori_loop(0, nl, lane_tile, 0)


def _route_call(qp, k1, k2, tr):
    n = qp.shape[0]
    sel = jnp.asarray(_cand_select())
    dup = jnp.asarray(np.repeat(np.eye(PEER_SLOTS, dtype=np.float32), 2, axis=0))
    kern = functools.partial(_route_kernel, nl=tr // LANES)
    cst = lambda i: (0, 0)
    return pl.pallas_call(
        kern,
        grid=(n // tr,),
        in_specs=[pl.BlockSpec((tr, PEER_HEADS * PEER_KEY_DIM), lambda i: (i, 0)),
                  pl.BlockSpec(k1.shape, cst),
                  pl.BlockSpec(k2.shape, cst),
                  pl.BlockSpec(sel.shape, lambda i: (0, 0, 0)),
                  pl.BlockSpec(dup.shape, cst)],
        out_specs=[pl.BlockSpec((tr, PEER_SLOTS), lambda i: (i, 0)),
                   pl.BlockSpec((tr, 2 * PEER_SLOTS), lambda i: (i, 0))],
        out_shape=[jax.ShapeDtypeStruct((n, PEER_SLOTS), i32),
                   jax.ShapeDtypeStruct((n, 2 * PEER_SLOTS), f32)],
        compiler_params=_params(("parallel",)),
        name="route",
    )(qp, k1, k2, sel, dup)


ROWS_PER_EXPERT = 4
TILE_STRIDE = 136
TOK_UNROLL = 8
HALF = 512


def _pack_table(tab):
    e, d = tab.shape
    bits = lax.bitcast_convert_type(tab.astype(bf16), jnp.uint16).astype(jnp.uint32)
    word = bits[:, :d // 2] | (bits[:, d // 2:] << 16)
    return lax.bitcast_convert_type(word, i32).reshape(e * ROWS_PER_EXPERT, LANES)


def _gather_tile(tab_ref, rows_ref, t, tile_ref):
    tok_rows = rows_ref.at[t]
    for k in range(PEER_SLOTS):
        row = pl.multiple_of(tok_rows[k], ROWS_PER_EXPERT)
        tile_ref[pl.ds(k, ROWS_PER_EXPERT, stride=TILE_STRIDE), :] = tab_ref[pl.ds(row, ROWS_PER_EXPERT), :]
    chunks = [pltpu.bitcast(tile_ref[j * TILE_STRIDE:j * TILE_STRIDE + PEER_SLOTS, :], bf16)
              for j in range(ROWS_PER_EXPERT)]
    return jnp.concatenate(chunks, axis=1)


def _peer_u_kernel(ids_ref, h_ref, g_ref, u_ref, o_ref, tile_scr, act_scr, *, tt):
    even = lax.broadcasted_iota(i32, (1, 2 * PEER_SLOTS), 1) % 2 == 0

    def group(gi, carry):
        rows = []
        for tl in range(TOK_UNROLL):
            t = gi * TOK_UNROLL + tl
            r = _gather_tile(u_ref, ids_ref, t, tile_scr.at[tl % 2])
            h = h_ref[t]
            major = h.astype(bf16)
            minor = (h - major.astype(f32)).astype(bf16)
            out = lax.dot_general(jnp.concatenate([major, minor], axis=0), r, NT_DIMS,
                                  preferred_element_type=f32)
            rows.append(jnp.where(even, out[0:1] + out[2:3], out[1:2] + out[3:4]))
        act_scr[pl.ds(pl.multiple_of(gi * TOK_UNROLL, TOK_UNROLL), TOK_UNROLL), :] = (
            jnp.concatenate(rows, axis=0))
        return carry

    lax.fori_loop(0, tt // TOK_UNROLL, group, 0)
    part = act_scr[...]
    act = part + jnp.where(even, pltpu.roll(part, 2 * PEER_SLOTS - 1, 1), pltpu.roll(part, 1, 1))
    gelu = 0.5 * act * (1.0 + lax.erf(act * (2.0 ** -0.5)))
    o_ref[...] = g_ref[...] * gelu


def _peer_u_call(ids, h3, g2, u_packed, tt):
    n = ids.shape[0]
    kern = functools.partial(_peer_u_kernel, tt=tt)
    return pl.pallas_call(
        kern,
        grid=(n // tt,),
        in_specs=[pl.BlockSpec((tt, PEER_SLOTS), lambda i: (i, 0), memory_space=pltpu.SMEM),
                  pl.BlockSpec((tt, 2, HALF), lambda i: (i, 0, 0)),
                  pl.BlockSpec((tt, 2 * PEER_SLOTS), lambda i: (i, 0)),
                  pl.BlockSpec(u_packed.shape, lambda i: (0, 0), pipeline_mode=pl.Buffered(1))],
        out_specs=pl.BlockSpec((tt, 2 * PEER_SLOTS), lambda i: (i, 0)),
        out_shape=jax.ShapeDtypeStruct((n, 2 * PEER_SLOTS), f32),
        scratch_shapes=[pltpu.VMEM((2, ROWS_PER_EXPERT * TILE_STRIDE, LANES), i32),
                        pltpu.VMEM((tt, 2 * PEER_SLOTS), f32)],
        compiler_params=_params(("arbitrary",)),
        name="peer_u",
    )(ids, h3, g2, u_packed)


def _peer_v_kernel(ids_ref, w_ref, v_ref, o_ref, tile_scr, *, tt):
    even = lax.broadcasted_iota(i32, (1, 2 * PEER_SLOTS), 1) % 2 == 0

    def group(gi, carry):
        w8 = w_ref[pl.ds(pl.multiple_of(gi * TOK_UNROLL, TOK_UNROLL), TOK_UNROLL), :]
        for tl in range(TOK_UNROLL):
            t = gi * TOK_UNROLL + tl
            r = _gather_tile(v_ref, ids_ref, t, tile_scr.at[tl % 2])
            w = w8[tl:tl + 1]
            lhs = jnp.concatenate([jnp.where(even, w, 0.0), jnp.where(even, 0.0, w)], axis=0)
            o_ref[t] = jnp.dot(lhs.astype(bf16), r, preferred_element_type=f32)
        return carry

    lax.fori_loop(0, tt // TOK_UNROLL, group, 0)


def _peer_v_call(ids, wgt2, v_packed, tt):
    n = ids.shape[0]
    kern = functools.partial(_peer_v_kernel, tt=tt)
    return pl.pallas_call(
        kern,
        grid=(n // tt,),
        in_specs=[pl.BlockSpec((tt, PEER_SLOTS), lambda i: (i, 0), memory_space=pltpu.SMEM),
                  pl.BlockSpec((tt, 2 * PEER_SLOTS), lambda i: (i, 0)),
                  pl.BlockSpec(v_packed.shape, lambda i: (0, 0), pipeline_mode=pl.Buffered(1))],
        out_specs=pl.BlockSpec((tt, 2, HALF), lambda i: (i, 0, 0)),
        out_shape=jax.ShapeDtypeStruct((n, 2, HALF), f32),
        scratch_shapes=[pltpu.VMEM((2, ROWS_PER_EXPERT * TILE_STRIDE, LANES), i32)],
        compiler_params=_params(("arbitrary",)),
        name="peer_v",
    )(ids, wgt2, v_packed)


def _final_kernel(x1_ref, p_ref, g2_ref, gf_ref, o_ref):
    x2 = x1_ref[...] + g2_ref[0] * p_ref[...]
    o_ref[...] = x2 * lax.rsqrt(jnp.mean(x2 * x2, axis=-1, keepdims=True) + EPS) * gf_ref[...]


def _final_call(x1, peer, g2, gf, seq, tm):
    n, d = x1.shape
    per_b = seq // tm
    tok = lambda i: (i, 0)
    return pl.pallas_call(
        _final_kernel,
        grid=(n // tm,),
        in_specs=[pl.BlockSpec((tm, d), tok), pl.BlockSpec((tm, d), tok),
                  pl.BlockSpec((1, 1, d), lambda i: (i // per_b, 0, 0)),
                  pl.BlockSpec((1, d), lambda i: (0, 0))],
        out_specs=pl.BlockSpec((tm, d), tok),
        out_shape=jax.ShapeDtypeStruct((n, d), f32),
        compiler_params=_params(("parallel",)),
        name="final",
    )(x1, peer, g2, gf)


def _rope_tables(positions):
    def tab(rot):
        inv = ROPE_THETA ** (-jnp.arange(0, rot, 2, dtype=f32) / rot)
        ang = positions.astype(f32)[..., None] * inv
        return jnp.cos(ang), jnp.sin(ang)

    cos_a, sin_a = tab(ROPE_DIM_A)
    cos_i, sin_i = tab(IDX_ROPE_DIM)
    b, s = positions.shape
    one = jnp.ones((b, s, 1), f32)
    zero = jnp.zeros((b, s, 4), f32)
    trig = jnp.concatenate([cos_a, cos_i, one, zero[..., :3], sin_a, sin_i, zero], axis=-1)
    return trig.reshape(b * s, TRIG)


def kernel(x, c, positions, w_ada, b_ada, g_mix, w_in, g_kv, w_uk, w_uv, b_forget,
           g_out_a, g_out_b, w_out, g_ffn, w_peer_q, peer_keys1, peer_keys2, peer_u,
           peer_v, g_final):
    b, s, d = x.shape
    n = b * s
    assert w_ada.shape[0] == 1, "single layer supported"
    tm = min(512, s)
    x2 = x.reshape(n, d)

    mod = _mod_call(c, w_ada[0], b_ada[0])
    mod = mod.reshape(b, 6, 1, d)
    shift1, scale1, gate1, shift2, scale2, gate2 = [mod[:, j] for j in range(6)]

    perm = _in_perm()
    w_in_r = jnp.where((perm >= 0)[None, :], w_in[0][:, np.maximum(perm, 0)], 0.0).astype(bf16)
    proj, kcat, ik, latt, vbt = _inproj_call(x2, scale1, shift1, g_mix[0].reshape(1, d), w_in_r,
                                             _rope_tables(positions), g_kv[0].reshape(1, KV_RANK), s, tm)

    wuk = w_uk[0].transpose(1, 0, 2).astype(bf16)
    wuvt = w_uv[0].transpose(1, 2, 0).astype(bf16)
    o_a = _dsa_call(proj, kcat.reshape(b, s, KCAT), latt, ik.reshape(b, s, LANES), wuk, wuvt,
                    b, s, tq=128, tk=tm)

    fl = proj[:, C_MISC + 224:C_MISC + 232].reshape(b, s, N_HEADS_B) + b_forget[0]
    cum = (jnp.cumsum(jax.nn.log_sigmoid(fl), axis=1) * LOG2E).transpose(0, 2, 1)
    o_bt = _fox_call(proj, vbt, cum[:, :, None, :], cum[..., None], tq=min(512, s), ts=tm)

    wo = w_out[0].astype(bf16)
    x1, h2, qp = _outproj_call(o_a, o_bt, x2, gate1, g_out_a[0].reshape(1, WIDTH_A),
                               g_out_b[0].reshape(1, WIDTH_B), wo[:WIDTH_A], wo[WIDTH_A:],
                               g_ffn[0].reshape(1, d), scale2, shift2,
                               w_peer_q[0].astype(bf16), s, tm)

    ids, gates2 = _route_call(qp, peer_keys1[0], peer_keys2[0], tr=min(1024, n))
    tt = 64
    wgt2 = _peer_u_call(ids, h2.reshape(n, 2, HALF), gates2, _pack_table(peer_u[0]), tt)
    peer = _peer_v_call(ids, wgt2, _pack_table(peer_v[0]), tt).reshape(n, d)

    out = _final_call(x1, peer, gate2, g_final.reshape(1, d), s, tm)
    return out.reshape(b, s, d)
```

```python
import functools

import jax
import jax.numpy as jnp
import numpy as np
from jax import lax
from jax.experimental import pallas as pl
from jax.experimental.pallas import tpu as pltpu

f32 = jnp.float32
bf16 = jnp.bfloat16
i32 = jnp.int32

N_HEADS_A = 8
HEAD_DIM_A = 64
ROPE_DIM_A = 16
NOPE_DIM_A = HEAD_DIM_A - ROPE_DIM_A
KV_RANK = 128
IDX_HEADS = 8
IDX_DIM = 32
IDX_ROPE_DIM = 8
TOPK_MAX = 256
N_HEADS_B = 8
HEAD_DIM_B = 64
WIDTH_A = N_HEADS_A * HEAD_DIM_A
WIDTH_B = N_HEADS_B * HEAD_DIM_B
ROPE_THETA = 500000.0
PEER_HEADS = 8
N_KEYS = 128
PEER_KEY_DIM = 128
PEER_TOPK = 16
PEER_SLOTS = PEER_HEADS * PEER_TOPK
EPS = 1e-6
IN_SIZES = (WIDTH_A, KV_RANK, ROPE_DIM_A, IDX_HEADS * IDX_DIM, IDX_DIM, IDX_HEADS,
            WIDTH_B, WIDTH_B, WIDTH_B, N_HEADS_B)

LANES = 128
NEG_BIG = -1e30
INT_MIN = -2147483648
VMEM_LIMIT = 56 * 1024 * 1024

C_R1 = 0
C_R2 = 128
C_LAT = 256
C_QN = 384
C_MISC = 768
C_QB = 1024
C_KB = 1536
C_VB = 2048
C_TOTAL = 2560

NT_DIMS = (((1,), (1,)), ((), ()))


def _in_perm():
    offs = np.cumsum((0,) + IN_SIZES)
    o_qa, o_lat, o_kr, o_iq, o_ik, o_iw, o_qb, o_kb, o_vb, o_fb = offs[:10]
    perm = -np.ones((C_TOTAL,), np.int64)
    ha, hi = ROPE_DIM_A // 2, IDX_ROPE_DIM // 2
    for h in range(N_HEADS_A):
        for j in range(ha):
            perm[C_R1 + h * ha + j] = o_qa + h * HEAD_DIM_A + j
            perm[C_R1 + 64 + h * ha + j] = o_qa + h * HEAD_DIM_A + ha + j
        for j in range(NOPE_DIM_A):
            perm[C_QN + h * NOPE_DIM_A + j] = o_qa + h * HEAD_DIM_A + ROPE_DIM_A + j
    for h in range(IDX_HEADS):
        for j in range(hi):
            perm[C_R2 + j * IDX_HEADS + h] = o_iq + h * IDX_DIM + j
            perm[C_R2 + 64 + j * IDX_HEADS + h] = o_iq + h * IDX_DIM + hi + j
        for j in range(IDX_DIM - IDX_ROPE_DIM):
            perm[C_MISC + j * IDX_HEADS + h] = o_iq + h * IDX_DIM + IDX_ROPE_DIM + j
    for j in range(ha):
        perm[C_R2 + 32 + j] = o_kr + j
        perm[C_R2 + 64 + 32 + j] = o_kr + ha + j
    for j in range(hi):
        perm[C_R2 + 40 + j] = o_ik + j
        perm[C_R2 + 64 + 40 + j] = o_ik + hi + j
    for j in range(IDX_DIM - IDX_ROPE_DIM):
        perm[C_MISC + 192 + j] = o_ik + IDX_ROPE_DIM + j
    for j in range(IDX_HEADS):
        perm[C_MISC + 216 + j] = o_iw + j
    for j in range(N_HEADS_B):
        perm[C_MISC + 224 + j] = o_fb + j
    perm[C_LAT:C_LAT + KV_RANK] = o_lat + np.arange(KV_RANK)
    perm[C_QB:C_QB + WIDTH_B] = o_qb + np.arange(WIDTH_B)
    perm[C_KB:C_KB + WIDTH_B] = o_kb + np.arange(WIDTH_B)
    perm[C_VB:C_VB + WIDTH_B] = o_vb + np.arange(WIDTH_B)
    return perm


def _params(sem):
    return pltpu.CompilerParams(dimension_semantics=sem, vmem_limit_bytes=VMEM_LIMIT)


def _mod_kernel(c_ref, w_ref, b_ref, o_ref):
    c = c_ref[...]
    ca = c * jax.nn.sigmoid(c)
    o_ref[...] = jnp.dot(ca, w_ref[...], preferred_element_type=f32,
                         precision=lax.Precision.HIGHEST) + b_ref[...]


def _mod_call(c, w, b):
    bsz, d = c.shape
    n = w.shape[1]
    return pl.pallas_call(
        _mod_kernel,
        grid=(n // d,),
        in_specs=[pl.BlockSpec((bsz, d), lambda j: (0, 0)),
                  pl.BlockSpec((d, d), lambda j: (0, j)),
                  pl.BlockSpec((1, d), lambda j: (0, j))],
        out_specs=pl.BlockSpec((bsz, d), lambda j: (0, j)),
        out_shape=jax.ShapeDtypeStruct((bsz, n), f32),
        compiler_params=_params(("arbitrary",)),
        name="mod",
    )(c, w, b.reshape(1, n))


KCAT = 256
ONES_ROWS = 16


def _with_ones_rows(vt):
    lead = vt.shape[:-2]
    t = vt.shape[-1]
    ones = jnp.ones(lead + (1, t), vt.dtype)
    zeros = jnp.zeros(lead + (ONES_ROWS - 1, t), vt.dtype)
    return jnp.concatenate([vt, ones, zeros], axis=-2)


def _lanes_from(lane, pieces):
    out = jnp.zeros(lane.shape, f32)
    for end, src, start in reversed(pieces):
        begin = max([e for e, _, _ in pieces if e < end], default=0)
        out = jnp.where(lane < end, pltpu.roll(src, (begin - start) % LANES, 1), out)
    return out


TRIG = 32


def _trig_expand():
    e = np.zeros((TRIG, 4 * LANES), np.float32)
    cos_a, cos_i, one, sin_a, sin_i = 0, 8, 12, 16, 24
    for lane in range(LANES):
        l, sign = lane % 64, (-1.0 if lane < 64 else 1.0)
        e[cos_a + l % 8, lane] = 1.0
        e[sin_a + l % 8, LANES + lane] = sign
        if l < 32:
            e[cos_i + l // IDX_HEADS, 2 * LANES + lane] = 1.0
            e[sin_i + l // IDX_HEADS, 3 * LANES + lane] = sign
        elif l < 40:
            e[cos_a + l - 32, 2 * LANES + lane] = 1.0
            e[sin_a + l - 32, 3 * LANES + lane] = sign
        elif l < 44:
            e[cos_i + l - 40, 2 * LANES + lane] = 1.0
            e[sin_i + l - 40, 3 * LANES + lane] = sign
        else:
            e[one, 2 * LANES + lane] = 1.0
    return e


def _inproj_kernel(x_ref, sc_ref, sh_ref, g_ref, w_ref, trig_ref, exp_ref,
                   gkv_ref, o_ref, kcat_ref, ik_ref, latt_ref, vbt_ref):
    x = x_ref[...]
    ms = jnp.mean(x * x, axis=-1, keepdims=True)
    h = x * lax.rsqrt(ms + EPS) * g_ref[...]
    h = h * (1.0 + sc_ref[0]) + sh_ref[0]
    p = jnp.dot(h.astype(bf16), w_ref[...], preferred_element_type=f32)
    tab = jnp.dot(trig_ref[...], exp_ref[...], preferred_element_type=f32,
                  precision=lax.Precision.HIGHEST)
    r1 = p[:, C_R1:C_R1 + LANES]
    o_ref[:, C_R1:C_R1 + LANES] = (r1 * tab[:, 0:LANES]
                                   + pltpu.roll(r1, 64, 1) * tab[:, LANES:2 * LANES])
    r2 = p[:, C_R2:C_R2 + LANES]
    r2 = r2 * tab[:, 2 * LANES:3 * LANES] + pltpu.roll(r2, 64, 1) * tab[:, 3 * LANES:]
    o_ref[:, C_R2:C_R2 + LANES] = r2
    lat = p[:, C_LAT:C_LAT + KV_RANK]
    lms = jnp.mean(lat * lat, axis=-1, keepdims=True)
    lat = lat * lax.rsqrt(lms + EPS) * gkv_ref[...]
    o_ref[:, C_LAT:C_LAT + KV_RANK] = lat
    o_ref[:, C_QN:] = p[:, C_QN:]

    lane = lax.broadcasted_iota(i32, r2.shape, 1)
    kr = _lanes_from(lane, [(8, r2, 32), (16, r2, 96)])
    kcat_ref[...] = jnp.concatenate([lat, kr], axis=1).astype(bf16)
    misc_hi = p[:, C_MISC + LANES:C_MISC + 2 * LANES]
    ik_ref[...] = _lanes_from(lane, [(4, r2, 40), (8, r2, 104), (32, misc_hi, 64)]).astype(bf16)
    latt_ref[...] = _with_ones_rows(lat.T.astype(bf16))
    vt = p[:, C_VB:C_VB + WIDTH_B].T.astype(bf16)
    vbt_ref[...] = _with_ones_rows(vt.reshape(N_HEADS_B, HEAD_DIM_B, vt.shape[1]))


def _inproj_call(x2, sc, sh, g, w, trig, gkv, seq, tm):
    n, d = x2.shape
    per_b = seq // tm
    bsz = n // seq
    tok = lambda i: (i, 0)
    bat = lambda i: (i // per_b, 0, 0)
    cst = lambda i: (0, 0)
    return pl.pallas_call(
        _inproj_kernel,
        grid=(n // tm,),
        in_specs=[pl.BlockSpec((tm, d), tok),
                  pl.BlockSpec((1, 1, d), bat),
                  pl.BlockSpec((1, 1, d), bat),
                  pl.BlockSpec((1, d), cst),
                  pl.BlockSpec((d, C_TOTAL), cst),
                  pl.BlockSpec((tm, TRIG), tok),
                  pl.BlockSpec((TRIG, 4 * LANES), cst),
                  pl.BlockSpec((1, KV_RANK), cst)],
        out_specs=[pl.BlockSpec((tm, C_TOTAL), tok),
                   pl.BlockSpec((tm, KCAT), tok),
                   pl.BlockSpec((tm, LANES), tok),
                   pl.BlockSpec((None, None, KV_RANK + ONES_ROWS, tm),
                                lambda i: (i // per_b, i % per_b, 0, 0)),
                   pl.BlockSpec((None, N_HEADS_B, None, HEAD_DIM_B + ONES_ROWS, tm),
                                lambda i: (i // per_b, 0, i % per_b, 0, 0))],
        out_shape=[jax.ShapeDtypeStruct((n, C_TOTAL), f32),
                   jax.ShapeDtypeStruct((n, KCAT), bf16),
                   jax.ShapeDtypeStruct((n, LANES), bf16),
                   jax.ShapeDtypeStruct((bsz, per_b, KV_RANK + ONES_ROWS, tm), bf16),
                   jax.ShapeDtypeStruct((bsz, N_HEADS_B, per_b, HEAD_DIM_B + ONES_ROWS, tm), bf16)],
        compiler_params=_params(("parallel",)),
        name="inproj",
    )(x2, sc, sh, g, w, trig, jnp.asarray(_trig_expand()), gkv)


LOG2E = 1.4426950408889634


def _sortable(x):
    bits = lax.bitcast_convert_type(x, i32)
    return bits ^ ((bits >> 31) & 0x7FFFFFFF)


def _dsa_kernel(r1_ref, r2_ref, qn_ref, misc_ref, kcat_ref, latt_ref, ik_ref, wuk_ref, wuvt_ref,
                o_ref, key_scr, iq_scr, m_scr, acc_scr, *, tq, tk, topk, nbits, seq):
    nh = N_HEADS_A
    qi = pl.program_id(1)
    nc = ((qi + 1) * tq + tk - 1) // tk
    qpos = qi * tq + lax.broadcasted_iota(i32, (tk, tq), 1)
    krow = lax.broadcasted_iota(i32, (tk, tq), 0)

    r2t = r2_ref[...].T
    mt = misc_ref[...].T
    iq_scr[...] = jnp.concatenate([r2t[0:32], r2t[64:96], mt[0:192]], axis=0)
    iqt = jnp.concatenate([iq_scr[pl.ds(h, IDX_DIM, stride=IDX_HEADS), :] for h in range(IDX_HEADS)],
                          axis=1)
    iqt = jnp.concatenate([iqt, jnp.zeros((LANES - IDX_DIM, IDX_HEADS * tq), f32)], axis=0).astype(bf16)
    iwt = mt[216:224] * (IDX_DIM ** -0.5 * IDX_HEADS ** -0.5)

    def score_chunk(c, carry):
        start = pl.multiple_of(c * tk, tk)
        d = jnp.dot(ik_ref[pl.ds(start, tk), :], iqt, preferred_element_type=f32)
        sc = jnp.zeros((tk, tq), f32)
        for h in range(IDX_HEADS):
            sc = sc + jnp.maximum(d[:, h * tq:(h + 1) * tq], 0.0) * iwt[h:h + 1, :]
        sc = jnp.where(start + krow <= qpos, sc, -jnp.inf)
        key_scr[c] = _sortable(sc)
        return carry

    lax.fori_loop(0, nc, score_chunk, 0)

    def count(pred):
        def body(c, acc):
            m = pred(key_scr[c], c * tk + krow)
            return acc + jnp.sum(m.reshape(tk // 8, 8, tq), axis=0)
        acc = lax.fori_loop(0, nc, body, jnp.zeros((8, tq), i32))
        return jnp.sum(acc, axis=0, keepdims=True)

    def bit_body(i, thr):
        cand = thr + lax.shift_left(jnp.int32(1), 31 - i)
        cnt = count(lambda k, col: jnp.where(k >= cand, 1, 0))
        return jnp.where(cnt >= topk, cand, thr)

    thr = lax.fori_loop(0, 32, bit_body, jnp.full((1, tq), INT_MIN, i32))
    need = topk - count(lambda k, col: jnp.where(k > thr, 1, 0))
    n_eq = count(lambda k, col: jnp.where(k == thr, 1, 0))

    def tie_search():
        def tie_body(i, last):
            cand = last + lax.shift_left(jnp.int32(1), nbits - 1 - i)
            below = count(lambda k, col: jnp.where(k == thr, jnp.where(col < cand, 1, 0), 0))
            return jnp.where(below < need, cand, last)
        return lax.fori_loop(0, nbits, tie_body, jnp.zeros((1, tq), i32))

    ambiguous = jnp.max(jnp.where(n_eq > need, 1, 0)) > 0
    last = lax.cond(ambiguous, tie_search, lambda: jnp.full((1, tq), seq, i32))

    scale = HEAD_DIM_A ** -0.5 * LOG2E
    r1t = r1_ref[...].T
    qnt = qn_ref[...].T.astype(bf16)
    pad = jnp.zeros((KCAT - KV_RANK - ROPE_DIM_A, tq), f32)
    cols = []
    for h in range(nh):
        ql = jnp.dot(wuk_ref[h], qnt[h * NOPE_DIM_A:(h + 1) * NOPE_DIM_A], preferred_element_type=f32)
        cols.append(jnp.concatenate([ql, r1t[h * 8:(h + 1) * 8], r1t[64 + h * 8:64 + (h + 1) * 8], pad],
                                    axis=0))
    qcat = (jnp.concatenate(cols, axis=1) * scale).astype(bf16)

    m_scr[...] = jnp.full(m_scr.shape, NEG_BIG, f32)
    acc_scr[...] = jnp.zeros(acc_scr.shape, f32)

    def attend_chunk(c, carry):
        start = pl.multiple_of(c * tk, tk)
        s = jnp.dot(kcat_ref[pl.ds(start, tk), :], qcat, preferred_element_type=f32)
        k = key_scr[c]
        col = start + krow
        tie = jnp.where(k == thr, jnp.where(col <= last, 0.0, NEG_BIG), NEG_BIG)
        bias = jnp.where(col <= qpos, jnp.where(k > thr, 0.0, tie), NEG_BIG)
        s = s + jnp.concatenate([bias] * nh, axis=1)
        m_old = m_scr[...]
        m_new = jnp.maximum(m_old, jnp.max(s, axis=0, keepdims=True))
        p = jnp.exp2(s - m_new)
        alpha = jnp.exp2(m_old - m_new)
        acc_scr[...] = alpha * acc_scr[...] + jnp.dot(latt_ref[c], p.astype(bf16),
                                                      preferred_element_type=f32)
        m_scr[...] = m_new
        return carry

    lax.fori_loop(0, nc, attend_chunk, 0)

    ot = (acc_scr[0:KV_RANK, :] / acc_scr[KV_RANK:KV_RANK + 1, :]).astype(bf16)
    outs = [jnp.dot(wuvt_ref[h], ot[:, h * tq:(h + 1) * tq], preferred_element_type=f32)
            for h in range(nh)]
    o_ref[...] = jnp.concatenate(outs, axis=0).T


def _dsa_call(proj, kcat, latt, ik, wuk, wuvt, bsz, seq, tq, tk):
    nh = N_HEADS_A
    topk = min(TOPK_MAX, seq // 4)
    nbits = max(1, (seq - 1).bit_length())
    nq = seq // tq
    kern = functools.partial(_dsa_kernel, tq=tq, tk=tk, topk=topk, nbits=nbits, seq=seq)
    qcol = lambda width, blk: pl.BlockSpec((tq, width), lambda b, i: (b * nq + i, blk))
    return pl.pallas_call(
        kern,
        grid=(bsz, nq),
        in_specs=[qcol(LANES, C_R1 // LANES), qcol(LANES, C_R2 // LANES),
                  qcol(C_MISC - C_QN, C_QN // (C_MISC - C_QN)), qcol(256, C_MISC // 256),
                  pl.BlockSpec((None, seq, KCAT), lambda b, i: (b, 0, 0)),
                  pl.BlockSpec((None, seq // tk, KV_RANK + ONES_ROWS, tk), lambda b, i: (b, 0, 0, 0)),
                  pl.BlockSpec((None, seq, LANES), lambda b, i: (b, 0, 0)),
                  pl.BlockSpec(wuk.shape, lambda b, i: (0, 0, 0)),
                  pl.BlockSpec(wuvt.shape, lambda b, i: (0, 0, 0))],
        out_specs=pl.BlockSpec((tq, WIDTH_A), lambda b, i: (b * nq + i, 0)),
        out_shape=jax.ShapeDtypeStruct((bsz * seq, WIDTH_A), f32),
        scratch_shapes=[pltpu.VMEM((seq // tk, tk, tq), i32),
                        pltpu.VMEM((IDX_HEADS * IDX_DIM, tq), f32),
                        pltpu.VMEM((1, nh * tq), f32),
                        pltpu.VMEM((KV_RANK + ONES_ROWS, nh * tq), f32)],
        compiler_params=_params(("parallel", "arbitrary")),
        name="dsa",
    )(proj, proj, proj, proj, kcat, latt, ik, wuk, wuvt)


def _fox_kernel(q_ref, k_ref, vt_ref, cq_ref, ck_ref, o_ref, s_scr, *, tq, ts):
    qi = pl.program_id(2)
    nc = ((qi + 1) * tq + ts - 1) // ts
    mine = lax.broadcasted_iota(i32, (LANES, tq), 0) // HEAD_DIM_B == pl.program_id(1) % 2
    qt = jnp.where(mine, q_ref[...].T * (HEAD_DIM_B ** -0.5 * LOG2E), 0.0).astype(bf16)
    cq = cq_ref[...]
    qpos = qi * tq + lax.broadcasted_iota(i32, (ts, tq), 1)
    krow = lax.broadcasted_iota(i32, (ts, tq), 0)

    def score(c, m, masked):
        start = pl.multiple_of(c * ts, ts)
        s = jnp.dot(k_ref[pl.ds(start, ts), :].astype(bf16), qt, preferred_element_type=f32)
        s = s + cq - ck_ref[pl.ds(start, ts), :]
        if masked:
            s = jnp.where(start + krow <= qpos, s, NEG_BIG)
        s_scr[c] = s
        return jnp.maximum(m, jnp.max(s, axis=0, keepdims=True))

    nfull = (qi * tq) // ts
    m = lax.fori_loop(0, nfull, lambda c, m: score(c, m, False), jnp.full((1, tq), NEG_BIG, f32))
    m = lax.fori_loop(nfull, nc, lambda c, m: score(c, m, True), m)

    def attend(c, acc):
        p = jnp.exp2(s_scr[c] - m)
        return acc + jnp.dot(vt_ref[c], p.astype(bf16), preferred_element_type=f32)

    acc = lax.fori_loop(0, nc, attend, jnp.zeros((HEAD_DIM_B + ONES_ROWS, tq), f32))
    o_ref[...] = acc[0:HEAD_DIM_B] / acc[HEAD_DIM_B:HEAD_DIM_B + 1]


def _fox_call(proj, vt, cq, ck, tq, ts):
    bsz, nh, _, _, _ = vt.shape
    hd = HEAD_DIM_B
    seq = proj.shape[0] // bsz
    nq = seq // tq
    kern = functools.partial(_fox_kernel, tq=tq, ts=ts)
    return pl.pallas_call(
        kern,
        grid=(bsz, nh, nq),
        in_specs=[pl.BlockSpec((tq, LANES), lambda b, h, i: (b * nq + i, C_QB // LANES + h // 2)),
                  pl.BlockSpec((seq, LANES), lambda b, h, i: (b, C_KB // LANES + h // 2)),
                  pl.BlockSpec((None, None, seq // ts, hd + ONES_ROWS, ts), lambda b, h, i: (b, h, 0, 0, 0)),
                  pl.BlockSpec((None, None, 1, tq), lambda b, h, i: (b, h, 0, i)),
                  pl.BlockSpec((None, None, seq, 1), lambda b, h, i: (b, h, 0, 0))],
        out_specs=pl.BlockSpec((None, None, hd, tq), lambda b, h, i: (b, h, 0, i)),
        out_shape=jax.ShapeDtypeStruct((bsz, nh, hd, seq), f32),
        scratch_shapes=[pltpu.VMEM((seq // ts, ts, tq), f32)],
        compiler_params=_params(("parallel", "parallel", "arbitrary")),
        name="fox",
    )(proj, proj, vt, cq, ck)


def _outproj_kernel(oa_ref, ob_ref, x_ref, g1_ref, ga_ref, gb_ref, wa_ref, wb_ref,
                    gf_ref, sc_ref, sh_ref, wq_ref, x1_ref, h2_ref, qp_ref):
    oa = oa_ref[...]
    ob = ob_ref[...].reshape(WIDTH_B, oa.shape[0]).T
    na = oa * lax.rsqrt(jnp.mean(oa * oa, axis=-1, keepdims=True) + EPS) * ga_ref[...]
    nb = ob * lax.rsqrt(jnp.mean(ob * ob, axis=-1, keepdims=True) + EPS) * gb_ref[...]
    res = (jnp.dot(na.astype(bf16), wa_ref[...], preferred_element_type=f32)
           + jnp.dot(nb.astype(bf16), wb_ref[...], preferred_element_type=f32))
    x1 = x_ref[...] + g1_ref[0] * res
    x1_ref[...] = x1
    h2 = x1 * lax.rsqrt(jnp.mean(x1 * x1, axis=-1, keepdims=True) + EPS) * gf_ref[...]
    h2 = h2 * (1.0 + sc_ref[0]) + sh_ref[0]
    h2_ref[...] = h2
    qp_ref[...] = jnp.dot(h2.astype(bf16), wq_ref[...], preferred_element_type=f32)


def _outproj_call(oa, ob, x2, g1, ga, gb, wa, wb, gf, sc, sh, wq, seq, tm):
    n, d = x2.shape
    per_b = seq // tm
    tok = lambda i: (i, 0)
    bat = lambda i: (i // per_b, 0, 0)
    cst = lambda i: (0, 0)
    nq = wq.shape[1]
    return pl.pallas_call(
        _outproj_kernel,
        grid=(n // tm,),
        in_specs=[pl.BlockSpec((tm, WIDTH_A), tok),
                  pl.BlockSpec((None, N_HEADS_B, HEAD_DIM_B, tm), lambda i: (i // per_b, 0, 0, i % per_b)),
                  pl.BlockSpec((tm, d), tok), pl.BlockSpec((1, 1, d), bat),
                  pl.BlockSpec((1, WIDTH_A), cst), pl.BlockSpec((1, WIDTH_B), cst),
                  pl.BlockSpec((WIDTH_A, d), cst), pl.BlockSpec((WIDTH_B, d), cst),
                  pl.BlockSpec((1, d), cst), pl.BlockSpec((1, 1, d), bat),
                  pl.BlockSpec((1, 1, d), bat), pl.BlockSpec((d, nq), cst)],
        out_specs=[pl.BlockSpec((tm, d), tok), pl.BlockSpec((tm, d), tok),
                   pl.BlockSpec((tm, nq), tok)],
        out_shape=[jax.ShapeDtypeStruct((n, d), f32), jax.ShapeDtypeStruct((n, d), f32),
                   jax.ShapeDtypeStruct((n, nq), f32)],
        compiler_params=_params(("parallel",)),
        name="outproj",
    )(oa, ob, x2, g1, ga, gb, wa, wb, gf, sc, sh, wq)


def _argmax_rows(x, iota):
    vals = [x[j:j + 8] for j in range(0, x.shape[0], 8)]
    idxs = [iota[j:j + 8] for j in range(0, x.shape[0], 8)]
    while len(vals) > 1:
        nv, ni = [], []
        for a in range(0, len(vals) - 1, 2):
            keep = vals[a] >= vals[a + 1]
            nv.append(jnp.where(keep, vals[a], vals[a + 1]))
            ni.append(jnp.where(keep, idxs[a], idxs[a + 1]))
        if len(vals) % 2:
            nv.append(vals[-1])
            ni.append(idxs[-1])
        vals, idxs = nv, ni
    m = jnp.max(vals[0], axis=0, keepdims=True)
    pos = jnp.min(jnp.where(vals[0] == m, idxs[0], x.shape[0]), axis=0, keepdims=True)
    return m, pos


def _topk_rows(x, kk):
    iota = lax.broadcasted_iota(i32, x.shape, 0)
    vals, idxs = [], []
    for _ in range(kk):
        m, pos = _argmax_rows(x, iota)
        vals.append(m)
        idxs.append(pos)
        x = jnp.where(iota == pos, -jnp.inf, x)
    return jnp.concatenate(vals, axis=0), jnp.concatenate(idxs, axis=0)


CAND_PAIRS = [(a, b) for a in range(PEER_TOPK) for b in range(PEER_TOPK) if (a + 1) * (b + 1) <= PEER_TOPK]
CAND_ROWS = -(-len(CAND_PAIRS) // 8) * 8


def _cand_select():
    sel = np.zeros((2, CAND_ROWS, PEER_TOPK), np.float32)
    for r, (a, b) in enumerate(CAND_PAIRS):
        sel[0, r, a] = 1.0
        sel[1, r, b] = 1.0
    return sel


def _route_kernel(qp_ref, k1_ref, k2_ref, sel_ref, dup_ref, rows_ref, g_ref, *, nl):
    half = PEER_KEY_DIM // 2
    hp = lax.Precision.HIGHEST
    pick = lambda j, v: jnp.dot(sel_ref[j], v, preferred_element_type=f32, precision=hp)
    rows = lax.broadcasted_iota(i32, (CAND_ROWS, LANES), 0)

    def head(q):
        s1 = lax.dot_general(k1_ref[...], q[:, :half], NT_DIMS, preferred_element_type=f32, precision=hp)
        s2 = lax.dot_general(k2_ref[...], q[:, half:], NT_DIMS, preferred_element_type=f32, precision=hp)
        v1, i1 = _topk_rows(s1, PEER_TOPK)
        v2, i2 = _topk_rows(s2, PEER_TOPK)
        cand = jnp.where(rows < len(CAND_PAIRS), pick(0, v1) + pick(1, v2), -jnp.inf)
        cand_e = pick(0, i1.astype(f32)) * N_KEYS + pick(1, i2.astype(f32))
        best, experts = [], []
        for _ in range(PEER_TOPK):
            m, pos = _argmax_rows(cand, rows)
            hit = rows == pos
            experts.append(jnp.sum(jnp.where(hit, cand_e, 0.0), axis=0, keepdims=True))
            best.append(m)
            cand = jnp.where(hit, -jnp.inf, cand)
        best = jnp.concatenate(best, axis=0)
        e = jnp.exp(best - best[0:1])
        return jnp.concatenate(experts, axis=0), e / jnp.sum(e, axis=0, keepdims=True)

    def lane_tile(l, carry):
        start = pl.multiple_of(l * LANES, LANES)
        outs = [head(qp_ref[pl.ds(start, LANES), h * PEER_KEY_DIM:(h + 1) * PEER_KEY_DIM])
                for h in range(PEER_HEADS)]
        ids = jnp.concatenate([o[0] for o in outs], axis=0)
        gates = jnp.concatenate([o[1] for o in outs], axis=0)
        rows_ref[pl.ds(start, LANES), :] = (ids.T * ROWS_PER_EXPERT).astype(i32)
        g_ref[pl.ds(start, LANES), :] = jnp.dot(dup_ref[...], gates, preferred_element_type=f32,
                                                precision=hp).T
        return carry

    lax.fori_loop(0, nl, lane_tile, 0)


def _route_call(qp, k1, k2, tr):
    n = qp.shape[0]
    sel = jnp.asarray(_cand_select())
    dup = jnp.asarray(np.repeat(np.eye(PEER_SLOTS, dtype=np.float32), 2, axis=0))
    kern = functools.partial(_route_kernel, nl=tr // LANES)
    cst = lambda i: (0, 0)
    return pl.pallas_call(
        kern,
        grid=(n // tr,),
        in_specs=[pl.BlockSpec((tr, PEER_HEADS * PEER_KEY_DIM), lambda i: (i, 0)),
                  pl.BlockSpec(k1.shape, cst),
                  pl.BlockSpec(k2.shape, cst),
                  pl.BlockSpec(sel.shape, lambda i: (0, 0, 0)),
                  pl.BlockSpec(dup.shape, cst)],
        out_specs=[pl.BlockSpec((tr, PEER_SLOTS), lambda i: (i, 0)),
                   pl.BlockSpec((tr, 2 * PEER_SLOTS), lambda i: (i, 0))],
        out_shape=[jax.ShapeDtypeStruct((n, PEER_SLOTS), i32),
                   jax.ShapeDtypeStruct((n, 2 * PEER_SLOTS), f32)],
        compiler_params=_params(("parallel",)),
        name="route",
    )(qp, k1, k2, sel, dup)


ROWS_PER_EXPERT = 4
TILE_STRIDE = 136
TOK_UNROLL = 8
HALF = 512


def _pack_kernel(t_ref, o_ref):
    te = t_ref.shape[0]
    bits = lax.bitcast_convert_type(t_ref[...].astype(bf16).astype(f32), i32)
    word = (bits[:, HALF:] & jnp.int32(-65536)) | lax.shift_right_logical(bits[:, :HALF], 16)
    for r in range(ROWS_PER_EXPERT):
        o_ref[pl.ds(r, te, stride=ROWS_PER_EXPERT), :] = word[:, r * LANES:(r + 1) * LANES]


def _pack_table(tab, te=256):
    e, d = tab.shape
    return pl.pallas_call(
        _pack_kernel,
        grid=(e // te,),
        in_specs=[pl.BlockSpec((te, d), lambda i: (i, 0))],
        out_specs=pl.BlockSpec((te * ROWS_PER_EXPERT, LANES), lambda i: (i, 0)),
        out_shape=jax.ShapeDtypeStruct((e * ROWS_PER_EXPERT, LANES), i32),
        compiler_params=_params(("parallel",)),
        name="pack",
    )(tab)


def _gather_tile(tab_ref, rows_ref, t, tile_ref):
    tok_rows = rows_ref.at[t]
    for k in range(PEER_SLOTS):
        row = pl.multiple_of(tok_rows[k], ROWS_PER_EXPERT)
        tile_ref[pl.ds(k, ROWS_PER_EXPERT, stride=TILE_STRIDE), :] = tab_ref[pl.ds(row, ROWS_PER_EXPERT), :]
    chunks = [pltpu.bitcast(tile_ref[j * TILE_STRIDE:j * TILE_STRIDE + PEER_SLOTS, :], bf16)
              for j in range(ROWS_PER_EXPERT)]
    return jnp.concatenate(chunks, axis=1)


def _peer_u_kernel(ids_ref, h_ref, g_ref, u_ref, o_ref, tile_scr, act_scr, *, tt):
    even = lax.broadcasted_iota(i32, (1, 2 * PEER_SLOTS), 1) % 2 == 0

    def group(gi, carry):
        rows = []
        for tl in range(TOK_UNROLL):
            t = gi * TOK_UNROLL + tl
            r = _gather_tile(u_ref, ids_ref, t, tile_scr.at[tl % 2])
            row = h_ref[pl.ds(t, 1), :]
            h = jnp.concatenate([row[:, :HALF], row[:, HALF:]], axis=0)
            major = h.astype(bf16)
            minor = (h - major.astype(f32)).astype(bf16)
            out = lax.dot_general(jnp.concatenate([major, minor], axis=0), r, NT_DIMS,
                                  preferred_element_type=f32)
            rows.append(jnp.where(even, out[0:1] + out[2:3], out[1:2] + out[3:4]))
        act_scr[pl.ds(pl.multiple_of(gi * TOK_UNROLL, TOK_UNROLL), TOK_UNROLL), :] = (
            jnp.concatenate(rows, axis=0))
        return carry

    lax.fori_loop(0, tt // TOK_UNROLL, group, 0)
    part = act_scr[...]
    act = part + jnp.where(even, pltpu.roll(part, 2 * PEER_SLOTS - 1, 1), pltpu.roll(part, 1, 1))
    gelu = 0.5 * act * (1.0 + lax.erf(act * (2.0 ** -0.5)))
    o_ref[...] = g_ref[...] * gelu


def _peer_u_call(ids, h3, g2, u_packed, tt):
    n = ids.shape[0]
    kern = functools.partial(_peer_u_kernel, tt=tt)
    return pl.pallas_call(
        kern,
        grid=(n // tt,),
        in_specs=[pl.BlockSpec((tt, PEER_SLOTS), lambda i: (i, 0), memory_space=pltpu.SMEM),
                  pl.BlockSpec((tt, 2 * HALF), lambda i: (i, 0)),
                  pl.BlockSpec((tt, 2 * PEER_SLOTS), lambda i: (i, 0)),
                  pl.BlockSpec(u_packed.shape, lambda i: (0, 0), pipeline_mode=pl.Buffered(1))],
        out_specs=pl.BlockSpec((tt, 2 * PEER_SLOTS), lambda i: (i, 0)),
        out_shape=jax.ShapeDtypeStruct((n, 2 * PEER_SLOTS), f32),
        scratch_shapes=[pltpu.VMEM((2, ROWS_PER_EXPERT * TILE_STRIDE, LANES), i32),
                        pltpu.VMEM((tt, 2 * PEER_SLOTS), f32)],
        compiler_params=_params(("arbitrary",)),
        name="peer_u",
    )(ids, h3, g2, u_packed)


def _peer_v_kernel(ids_ref, w_ref, v_ref, o_ref, tile_scr, *, tt):
    even = lax.broadcasted_iota(i32, (1, 2 * PEER_SLOTS), 1) % 2 == 0

    def group(gi, carry):
        base = pl.multiple_of(gi * TOK_UNROLL, TOK_UNROLL)
        w8 = w_ref[pl.ds(base, TOK_UNROLL), :]
        rows = []
        for tl in range(TOK_UNROLL):
            t = gi * TOK_UNROLL + tl
            r = _gather_tile(v_ref, ids_ref, t, tile_scr.at[tl % 2])
            w = w8[tl:tl + 1]
            lhs = jnp.concatenate([jnp.where(even, w, 0.0), jnp.where(even, 0.0, w)], axis=0)
            out = jnp.dot(lhs.astype(bf16), r, preferred_element_type=f32)
            rows.append(jnp.concatenate([out[0:1], out[1:2]], axis=1))
        o_ref[pl.ds(base, TOK_UNROLL), :] = jnp.concatenate(rows, axis=0)
        return carry

    lax.fori_loop(0, tt // TOK_UNROLL, group, 0)


def _peer_v_call(ids, wgt2, v_packed, tt):
    n = ids.shape[0]
    kern = functools.partial(_peer_v_kernel, tt=tt)
    return pl.pallas_call(
        kern,
        grid=(n // tt,),
        in_specs=[pl.BlockSpec((tt, PEER_SLOTS), lambda i: (i, 0), memory_space=pltpu.SMEM),
                  pl.BlockSpec((tt, 2 * PEER_SLOTS), lambda i: (i, 0)),
                  pl.BlockSpec(v_packed.shape, lambda i: (0, 0), pipeline_mode=pl.Buffered(1))],
        out_specs=pl.BlockSpec((tt, 2 * HALF), lambda i: (i, 0)),
        out_shape=jax.ShapeDtypeStruct((n, 2 * HALF), f32),
        scratch_shapes=[pltpu.VMEM((2, ROWS_PER_EXPERT * TILE_STRIDE, LANES), i32)],
        compiler_params=_params(("arbitrary",)),
        name="peer_v",
    )(ids, wgt2, v_packed)


def _final_kernel(x1_ref, p_ref, g2_ref, gf_ref, o_ref):
    x2 = x1_ref[...] + g2_ref[0] * p_ref[...]
    o_ref[...] = x2 * lax.rsqrt(jnp.mean(x2 * x2, axis=-1, keepdims=True) + EPS) * gf_ref[...]


def _final_call(x1, peer, g2, gf, seq, tm):
    n, d = x1.shape
    per_b = seq // tm
    tok = lambda i: (i, 0)
    return pl.pallas_call(
        _final_kernel,
        grid=(n // tm,),
        in_specs=[pl.BlockSpec((tm, d), tok), pl.BlockSpec((tm, d), tok),
                  pl.BlockSpec((1, 1, d), lambda i: (i // per_b, 0, 0)),
                  pl.BlockSpec((1, d), lambda i: (0, 0))],
        out_specs=pl.BlockSpec((tm, d), tok),
        out_shape=jax.ShapeDtypeStruct((n, d), f32),
        compiler_params=_params(("parallel",)),
        name="final",
    )(x1, peer, g2, gf)


def _rope_tables(positions):
    def tab(rot):
        inv = ROPE_THETA ** (-jnp.arange(0, rot, 2, dtype=f32) / rot)
        ang = positions.astype(f32)[..., None] * inv
        return jnp.cos(ang), jnp.sin(ang)

    cos_a, sin_a = tab(ROPE_DIM_A)
    cos_i, sin_i = tab(IDX_ROPE_DIM)
    b, s = positions.shape
    one = jnp.ones((b, s, 1), f32)
    zero = jnp.zeros((b, s, 4), f32)
    trig = jnp.concatenate([cos_a, cos_i, one, zero[..., :3], sin_a, sin_i, zero], axis=-1)
    return trig.reshape(b * s, TRIG)


def kernel(x, c, positions, w_ada, b_ada, g_mix, w_in, g_kv, w_uk, w_uv, b_forget,
           g_out_a, g_out_b, w_out, g_ffn, w_peer_q, peer_keys1, peer_keys2, peer_u,
           peer_v, g_final):
    b, s, d = x.shape
    n = b * s
    assert w_ada.shape[0] == 1, "single layer supported"
    tm = min(512, s)
    x2 = x.reshape(n, d)

    mod = _mod_call(c, w_ada[0], b_ada[0])
    mod = mod.reshape(b, 6, 1, d)
    shift1, scale1, gate1, shift2, scale2, gate2 = [mod[:, j] for j in range(6)]

    perm = _in_perm()
    w_in_r = jnp.where((perm >= 0)[None, :], w_in[0][:, np.maximum(perm, 0)], 0.0).astype(bf16)
    proj, kcat, ik, latt, vbt = _inproj_call(x2, scale1, shift1, g_mix[0].reshape(1, d), w_in_r,
                                             _rope_tables(positions), g_kv[0].reshape(1, KV_RANK), s, tm)

    wuk = w_uk[0].transpose(1, 0, 2).astype(bf16)
    wuvt = w_uv[0].transpose(1, 2, 0).astype(bf16)
    o_a = _dsa_call(proj, kcat.reshape(b, s, KCAT), latt, ik.reshape(b, s, LANES), wuk, wuvt,
                    b, s, tq=128, tk=tm)

    fl = proj[:, C_MISC + 224:C_MISC + 232].reshape(b, s, N_HEADS_B) + b_forget[0]
    cum = (jnp.cumsum(jax.nn.log_sigmoid(fl), axis=1) * LOG2E).transpose(0, 2, 1)
    o_bt = _fox_call(proj, vbt, cum[:, :, None, :], cum[..., None], tq=min(512, s), ts=tm)

    wo = w_out[0].astype(bf16)
    x1, h2, qp = _outproj_call(o_a, o_bt, x2, gate1, g_out_a[0].reshape(1, WIDTH_A),
                               g_out_b[0].reshape(1, WIDTH_B), wo[:WIDTH_A], wo[WIDTH_A:],
                               g_ffn[0].reshape(1, d), scale2, shift2,
                               w_peer_q[0].astype(bf16), s, tm)

    ids, gates2 = _route_call(qp, peer_keys1[0], peer_keys2[0], tr=min(1024, n))
    tt = 64
    wgt2 = _peer_u_call(ids, h2, gates2, _pack_table(peer_u[0]), tt)
    peer = _peer_v_call(ids, wgt2, _pack_table(peer_v[0]), tt)

    out = _final_call(x1, peer, gate2, g_final.reshape(1, d), s, tm)
    return out.reshape(b, s, d)
```

```python
import functools

import jax
import jax.numpy as jnp
import numpy as np
from jax import lax
from jax.experimental import pallas as pl
from jax.experimental.pallas import tpu as pltpu

f32 = jnp.float32
bf16 = jnp.bfloat16
i32 = jnp.int32

N_HEADS_A = 8
HEAD_DIM_A = 64
ROPE_DIM_A = 16
NOPE_DIM_A = HEAD_DIM_A - ROPE_DIM_A
KV_RANK = 128
IDX_HEADS = 8
IDX_DIM = 32
IDX_ROPE_DIM = 8
TOPK_MAX = 256
N_HEADS_B = 8
HEAD_DIM_B = 64
WIDTH_A = N_HEADS_A * HEAD_DIM_A
WIDTH_B = N_HEADS_B * HEAD_DIM_B
ROPE_THETA = 500000.0
PEER_HEADS = 8
N_KEYS = 128
PEER_KEY_DIM = 128
PEER_TOPK = 16
PEER_SLOTS = PEER_HEADS * PEER_TOPK
EPS = 1e-6
IN_SIZES = (WIDTH_A, KV_RANK, ROPE_DIM_A, IDX_HEADS * IDX_DIM, IDX_DIM, IDX_HEADS,
            WIDTH_B, WIDTH_B, WIDTH_B, N_HEADS_B)

LANES = 128
NEG_BIG = -1e30
INT_MIN = -2147483648
VMEM_LIMIT = 56 * 1024 * 1024

C_R1 = 0
C_R2 = 128
C_LAT = 256
C_QN = 384
C_MISC = 768
C_QB = 1024
C_KB = 1536
C_VB = 2048
C_TOTAL = 2560

NT_DIMS = (((1,), (1,)), ((), ()))


def _in_perm():
    offs = np.cumsum((0,) + IN_SIZES)
    o_qa, o_lat, o_kr, o_iq, o_ik, o_iw, o_qb, o_kb, o_vb, o_fb = offs[:10]
    perm = -np.ones((C_TOTAL,), np.int64)
    ha, hi = ROPE_DIM_A // 2, IDX_ROPE_DIM // 2
    for h in range(N_HEADS_A):
        for j in range(ha):
            perm[C_R1 + h * ha + j] = o_qa + h * HEAD_DIM_A + j
            perm[C_R1 + 64 + h * ha + j] = o_qa + h * HEAD_DIM_A + ha + j
        for j in range(NOPE_DIM_A):
            perm[C_QN + h * NOPE_DIM_A + j] = o_qa + h * HEAD_DIM_A + ROPE_DIM_A + j
    for h in range(IDX_HEADS):
        for j in range(hi):
            perm[C_R2 + j * IDX_HEADS + h] = o_iq + h * IDX_DIM + j
            perm[C_R2 + 64 + j * IDX_HEADS + h] = o_iq + h * IDX_DIM + hi + j
        for j in range(IDX_DIM - IDX_ROPE_DIM):
            perm[C_MISC + j * IDX_HEADS + h] = o_iq + h * IDX_DIM + IDX_ROPE_DIM + j
    for j in range(ha):
        perm[C_R2 + 32 + j] = o_kr + j
        perm[C_R2 + 64 + 32 + j] = o_kr + ha + j
    for j in range(hi):
        perm[C_R2 + 40 + j] = o_ik + j
        perm[C_R2 + 64 + 40 + j] = o_ik + hi + j
    for j in range(IDX_DIM - IDX_ROPE_DIM):
        perm[C_MISC + 192 + j] = o_ik + IDX_ROPE_DIM + j
    for j in range(IDX_HEADS):
        perm[C_MISC + 216 + j] = o_iw + j
    for j in range(N_HEADS_B):
        perm[C_MISC + 224 + j] = o_fb + j
    perm[C_LAT:C_LAT + KV_RANK] = o_lat + np.arange(KV_RANK)
    perm[C_QB:C_QB + WIDTH_B] = o_qb + np.arange(WIDTH_B)
    perm[C_KB:C_KB + WIDTH_B] = o_kb + np.arange(WIDTH_B)
    perm[C_VB:C_VB + WIDTH_B] = o_vb + np.arange(WIDTH_B)
    return perm


def _params(sem):
    return pltpu.CompilerParams(dimension_semantics=sem, vmem_limit_bytes=VMEM_LIMIT)


def _mod_kernel(c_ref, w_ref, b_ref, o_ref):
    c = c_ref[...]
    ca = c * jax.nn.sigmoid(c)
    o_ref[...] = jnp.dot(ca, w_ref[...], preferred_element_type=f32,
                         precision=lax.Precision.HIGHEST) + b_ref[...]


def _mod_call(c, w, b):
    bsz, d = c.shape
    n = w.shape[1]
    return pl.pallas_call(
        _mod_kernel,
        grid=(n // d,),
        in_specs=[pl.BlockSpec((bsz, d), lambda j: (0, 0)),
                  pl.BlockSpec((d, d), lambda j: (0, j)),
                  pl.BlockSpec((1, d), lambda j: (0, j))],
        out_specs=pl.BlockSpec((bsz, d), lambda j: (0, j)),
        out_shape=jax.ShapeDtypeStruct((bsz, n), f32),
        compiler_params=_params(("arbitrary",)),
        name="mod",
    )(c, w, b.reshape(1, n))


KCAT = 256
ONES_ROWS = 16


def _with_ones_rows(vt):
    lead = vt.shape[:-2]
    t = vt.shape[-1]
    ones = jnp.ones(lead + (1, t), vt.dtype)
    zeros = jnp.zeros(lead + (ONES_ROWS - 1, t), vt.dtype)
    return jnp.concatenate([vt, ones, zeros], axis=-2)


def _lanes_from(lane, pieces):
    out = jnp.zeros(lane.shape, f32)
    for end, src, start in reversed(pieces):
        begin = max([e for e, _, _ in pieces if e < end], default=0)
        out = jnp.where(lane < end, pltpu.roll(src, (begin - start) % LANES, 1), out)
    return out


TRIG = 32


def _trig_expand():
    e = np.zeros((TRIG, 4 * LANES), np.float32)
    cos_a, cos_i, one, sin_a, sin_i = 0, 8, 12, 16, 24
    for lane in range(LANES):
        l, sign = lane % 64, (-1.0 if lane < 64 else 1.0)
        e[cos_a + l % 8, lane] = 1.0
        e[sin_a + l % 8, LANES + lane] = sign
        if l < 32:
            e[cos_i + l // IDX_HEADS, 2 * LANES + lane] = 1.0
            e[sin_i + l // IDX_HEADS, 3 * LANES + lane] = sign
        elif l < 40:
            e[cos_a + l - 32, 2 * LANES + lane] = 1.0
            e[sin_a + l - 32, 3 * LANES + lane] = sign
        elif l < 44:
            e[cos_i + l - 40, 2 * LANES + lane] = 1.0
            e[sin_i + l - 40, 3 * LANES + lane] = sign
        else:
            e[one, 2 * LANES + lane] = 1.0
    return e


def _inproj_kernel(x_ref, sc_ref, sh_ref, g_ref, w_ref, trig_ref, exp_ref,
                   gkv_ref, o_ref, kcat_ref, ik_ref, latt_ref, vbt_ref):
    x = x_ref[...]
    ms = jnp.mean(x * x, axis=-1, keepdims=True)
    h = x * lax.rsqrt(ms + EPS) * g_ref[...]
    h = h * (1.0 + sc_ref[0]) + sh_ref[0]
    p = jnp.dot(h.astype(bf16), w_ref[...], preferred_element_type=f32)
    tab = jnp.dot(trig_ref[...], exp_ref[...], preferred_element_type=f32,
                  precision=lax.Precision.HIGHEST)
    r1 = p[:, C_R1:C_R1 + LANES]
    o_ref[:, C_R1:C_R1 + LANES] = (r1 * tab[:, 0:LANES]
                                   + pltpu.roll(r1, 64, 1) * tab[:, LANES:2 * LANES])
    r2 = p[:, C_R2:C_R2 + LANES]
    r2 = r2 * tab[:, 2 * LANES:3 * LANES] + pltpu.roll(r2, 64, 1) * tab[:, 3 * LANES:]
    o_ref[:, C_R2:C_R2 + LANES] = r2
    lat = p[:, C_LAT:C_LAT + KV_RANK]
    lms = jnp.mean(lat * lat, axis=-1, keepdims=True)
    lat = lat * lax.rsqrt(lms + EPS) * gkv_ref[...]
    o_ref[:, C_LAT:C_LAT + KV_RANK] = lat
    o_ref[:, C_QN:] = p[:, C_QN:]

    lane = lax.broadcasted_iota(i32, r2.shape, 1)
    kr = _lanes_from(lane, [(8, r2, 32), (16, r2, 96)])
    kcat_ref[...] = jnp.concatenate([lat, kr], axis=1).astype(bf16)
    misc_hi = p[:, C_MISC + LANES:C_MISC + 2 * LANES]
    ik_ref[...] = _lanes_from(lane, [(4, r2, 40), (8, r2, 104), (32, misc_hi, 64)]).astype(bf16)
    latt_ref[...] = _with_ones_rows(lat.T.astype(bf16))
    vt = p[:, C_VB:C_VB + WIDTH_B].T.astype(bf16)
    vbt_ref[...] = _with_ones_rows(vt.reshape(N_HEADS_B, HEAD_DIM_B, vt.shape[1]))


def _inproj_call(x2, sc, sh, g, w, trig, gkv, seq, tm):
    n, d = x2.shape
    per_b = seq // tm
    bsz = n // seq
    tok = lambda i: (i, 0)
    bat = lambda i: (i // per_b, 0, 0)
    cst = lambda i: (0, 0)
    return pl.pallas_call(
        _inproj_kernel,
        grid=(n // tm,),
        in_specs=[pl.BlockSpec((tm, d), tok),
                  pl.BlockSpec((1, 1, d), bat),
                  pl.BlockSpec((1, 1, d), bat),
                  pl.BlockSpec((1, d), cst),
                  pl.BlockSpec((d, C_TOTAL), cst),
                  pl.BlockSpec((tm, TRIG), tok),
                  pl.BlockSpec((TRIG, 4 * LANES), cst),
                  pl.BlockSpec((1, KV_RANK), cst)],
        out_specs=[pl.BlockSpec((tm, C_TOTAL), tok),
                   pl.BlockSpec((tm, KCAT), tok),
                   pl.BlockSpec((tm, LANES), tok),
                   pl.BlockSpec((None, None, KV_RANK + ONES_ROWS, tm),
                                lambda i: (i // per_b, i % per_b, 0, 0)),
                   pl.BlockSpec((None, N_HEADS_B, None, HEAD_DIM_B + ONES_ROWS, tm),
                                lambda i: (i // per_b, 0, i % per_b, 0, 0))],
        out_shape=[jax.ShapeDtypeStruct((n, C_TOTAL), f32),
                   jax.ShapeDtypeStruct((n, KCAT), bf16),
                   jax.ShapeDtypeStruct((n, LANES), bf16),
                   jax.ShapeDtypeStruct((bsz, per_b, KV_RANK + ONES_ROWS, tm), bf16),
                   jax.ShapeDtypeStruct((bsz, N_HEADS_B, per_b, HEAD_DIM_B + ONES_ROWS, tm), bf16)],
        compiler_params=_params(("parallel",)),
        name="inproj",
    )(x2, sc, sh, g, w, trig, jnp.asarray(_trig_expand()), gkv)


LOG2E = 1.4426950408889634


def _sortable(x):
    bits = lax.bitcast_convert_type(x, i32)
    return bits ^ ((bits >> 31) & 0x7FFFFFFF)


def _dsa_kernel(r1_ref, r2_ref, qn_ref, misc_ref, kcat_ref, latt_ref, ik_ref, wuk_ref, wuvt_ref,
                o_ref, key_scr, iq_scr, m_scr, acc_scr, *, tq, tk, topk, nbits, seq):
    nh = N_HEADS_A
    qi = pl.program_id(1)
    nc = ((qi + 1) * tq + tk - 1) // tk
    qpos = qi * tq + lax.broadcasted_iota(i32, (tk, tq), 1)
    krow = lax.broadcasted_iota(i32, (tk, tq), 0)

    r2t = r2_ref[...].T
    mt = misc_ref[...].T
    iq_scr[...] = jnp.concatenate([r2t[0:32], r2t[64:96], mt[0:192]], axis=0)
    iqt = jnp.concatenate([iq_scr[pl.ds(h, IDX_DIM, stride=IDX_HEADS), :] for h in range(IDX_HEADS)],
                          axis=1)
    iqt = jnp.concatenate([iqt, jnp.zeros((LANES - IDX_DIM, IDX_HEADS * tq), f32)], axis=0).astype(bf16)
    iwt = mt[216:224] * (IDX_DIM ** -0.5 * IDX_HEADS ** -0.5)

    def score_chunk(c, carry):
        start = pl.multiple_of(c * tk, tk)
        d = jnp.dot(ik_ref[pl.ds(start, tk), :], iqt, preferred_element_type=f32)
        sc = jnp.zeros((tk, tq), f32)
        for h in range(IDX_HEADS):
            sc = sc + jnp.maximum(d[:, h * tq:(h + 1) * tq], 0.0) * iwt[h:h + 1, :]
        sc = jnp.where(start + krow <= qpos, sc, -jnp.inf)
        key_scr[c] = _sortable(sc)
        return carry

    lax.fori_loop(0, nc, score_chunk, 0)

    def count(pred):
        def body(c, acc):
            m = pred(key_scr[c], c * tk + krow)
            return acc + jnp.sum(m.reshape(tk // 8, 8, tq), axis=0)
        acc = lax.fori_loop(0, nc, body, jnp.zeros((8, tq), i32))
        return jnp.sum(acc, axis=0, keepdims=True)

    def bit_body(i, thr):
        cand = thr + lax.shift_left(jnp.int32(1), 31 - i)
        cnt = count(lambda k, col: jnp.where(k >= cand, 1, 0))
        return jnp.where(cnt >= topk, cand, thr)

    thr = lax.fori_loop(0, 32, bit_body, jnp.full((1, tq), INT_MIN, i32))
    need = topk - count(lambda k, col: jnp.where(k > thr, 1, 0))
    n_eq = count(lambda k, col: jnp.where(k == thr, 1, 0))

    def tie_search():
        def tie_body(i, last):
            cand = last + lax.shift_left(jnp.int32(1), nbits - 1 - i)
            below = count(lambda k, col: jnp.where(k == thr, jnp.where(col < cand, 1, 0), 0))
            return jnp.where(below < need, cand, last)
        return lax.fori_loop(0, nbits, tie_body, jnp.zeros((1, tq), i32))

    ambiguous = jnp.max(jnp.where(n_eq > need, 1, 0)) > 0
    last = lax.cond(ambiguous, tie_search, lambda: jnp.full((1, tq), seq, i32))

    scale = HEAD_DIM_A ** -0.5 * LOG2E
    r1t = r1_ref[...].T
    qnt = qn_ref[...].T.astype(bf16)
    pad = jnp.zeros((KCAT - KV_RANK - ROPE_DIM_A, tq), f32)
    cols = []
    for h in range(nh):
        ql = jnp.dot(wuk_ref[h], qnt[h * NOPE_DIM_A:(h + 1) * NOPE_DIM_A], preferred_element_type=f32)
        cols.append(jnp.concatenate([ql, r1t[h * 8:(h + 1) * 8], r1t[64 + h * 8:64 + (h + 1) * 8], pad],
                                    axis=0))
    qcat = (jnp.concatenate(cols, axis=1) * scale).astype(bf16)

    m_scr[...] = jnp.full(m_scr.shape, NEG_BIG, f32)
    acc_scr[...] = jnp.zeros(acc_scr.shape, f32)

    def attend_chunk(c, carry):
        start = pl.multiple_of(c * tk, tk)
        s = jnp.dot(kcat_ref[pl.ds(start, tk), :], qcat, preferred_element_type=f32)
        k = key_scr[c]
        col = start + krow
        tie = jnp.where(k == thr, jnp.where(col <= last, 0.0, NEG_BIG), NEG_BIG)
        bias = jnp.where(col <= qpos, jnp.where(k > thr, 0.0, tie), NEG_BIG)
        s = s + jnp.concatenate([bias] * nh, axis=1)
        m_old = m_scr[...]
        m_new = jnp.maximum(m_old, jnp.max(s, axis=0, keepdims=True))
        p = jnp.exp2(s - m_new)
        alpha = jnp.exp2(m_old - m_new)
        acc_scr[...] = alpha * acc_scr[...] + jnp.dot(latt_ref[c], p.astype(bf16),
                                                      preferred_element_type=f32)
        m_scr[...] = m_new
        return carry

    lax.fori_loop(0, nc, attend_chunk, 0)

    ot = (acc_scr[0:KV_RANK, :] / acc_scr[KV_RANK:KV_RANK + 1, :]).astype(bf16)
    outs = [jnp.dot(wuvt_ref[h], ot[:, h * tq:(h + 1) * tq], preferred_element_type=f32)
            for h in range(nh)]
    o_ref[...] = jnp.concatenate(outs, axis=0).T


def _dsa_call(proj, kcat, latt, ik, wuk, wuvt, bsz, seq, tq, tk):
    nh = N_HEADS_A
    topk = min(TOPK_MAX, seq // 4)
    nbits = max(1, (seq - 1).bit_length())
    nq = seq // tq
    kern = functools.partial(_dsa_kernel, tq=tq, tk=tk, topk=topk, nbits=nbits, seq=seq)
    qcol = lambda width, blk: pl.BlockSpec((tq, width), lambda b, i: (b * nq + i, blk))
    return pl.pallas_call(
        kern,
        grid=(bsz, nq),
        in_specs=[qcol(LANES, C_R1 // LANES), qcol(LANES, C_R2 // LANES),
                  qcol(C_MISC - C_QN, C_QN // (C_MISC - C_QN)), qcol(256, C_MISC // 256),
                  pl.BlockSpec((None, seq, KCAT), lambda b, i: (b, 0, 0)),
                  pl.BlockSpec((None, seq // tk, KV_RANK + ONES_ROWS, tk), lambda b, i: (b, 0, 0, 0)),
                  pl.BlockSpec((None, seq, LANES), lambda b, i: (b, 0, 0)),
                  pl.BlockSpec(wuk.shape, lambda b, i: (0, 0, 0)),
                  pl.BlockSpec(wuvt.shape, lambda b, i: (0, 0, 0))],
        out_specs=pl.BlockSpec((tq, WIDTH_A), lambda b, i: (b * nq + i, 0)),
        out_shape=jax.ShapeDtypeStruct((bsz * seq, WIDTH_A), f32),
        scratch_shapes=[pltpu.VMEM((seq // tk, tk, tq), i32),
                        pltpu.VMEM((IDX_HEADS * IDX_DIM, tq), f32),
                        pltpu.VMEM((1, nh * tq), f32),
                        pltpu.VMEM((KV_RANK + ONES_ROWS, nh * tq), f32)],
        compiler_params=_params(("parallel", "arbitrary")),
        name="dsa",
    )(proj, proj, proj, proj, kcat, latt, ik, wuk, wuvt)


C_FB = C_MISC + 224
CUM_CHUNK = 512


def _cum_kernel(p_ref, b_ref, o_ref):
    seq = p_ref.shape[0]
    lane0 = C_FB % LANES
    x = p_ref[...].T[lane0:lane0 + N_HEADS_B] + b_ref[...]
    ls = jnp.minimum(x, 0.0) - jnp.log(1.0 + jnp.exp(-jnp.abs(x)))
    row = lax.broadcasted_iota(i32, (CUM_CHUNK, CUM_CHUNK), 0)
    col = lax.broadcasted_iota(i32, (CUM_CHUNK, CUM_CHUNK), 1)
    upper = jnp.where(row <= col, 1.0, 0.0)
    carry = jnp.zeros((N_HEADS_B, 1), f32)
    for c in range(seq // CUM_CHUNK):
        part = ls[:, c * CUM_CHUNK:(c + 1) * CUM_CHUNK]
        cs = jnp.dot(part, upper, preferred_element_type=f32, precision=lax.Precision.HIGHEST) + carry
        o_ref[:, c * CUM_CHUNK:(c + 1) * CUM_CHUNK] = cs * LOG2E
        carry = cs[:, CUM_CHUNK - 1:CUM_CHUNK]


def _cum_call(proj, b_forget, bsz, seq):
    assert seq % CUM_CHUNK == 0
    return pl.pallas_call(
        _cum_kernel,
        grid=(bsz,),
        in_specs=[pl.BlockSpec((seq, LANES), lambda b: (b, C_FB // LANES)),
                  pl.BlockSpec((N_HEADS_B, 1), lambda b: (0, 0))],
        out_specs=pl.BlockSpec((None, N_HEADS_B, seq), lambda b: (b, 0, 0)),
        out_shape=jax.ShapeDtypeStruct((bsz, N_HEADS_B, seq), f32),
        compiler_params=_params(("parallel",)),
        name="cum",
    )(proj, b_forget.reshape(N_HEADS_B, 1))


def _fox_kernel(q_ref, k_ref, vt_ref, cq_ref, ck_ref, o_ref, s_scr, *, tq, ts):
    qi = pl.program_id(2)
    nc = ((qi + 1) * tq + ts - 1) // ts
    mine = lax.broadcasted_iota(i32, (LANES, tq), 0) // HEAD_DIM_B == pl.program_id(1) % 2
    qt = jnp.where(mine, q_ref[...].T * (HEAD_DIM_B ** -0.5 * LOG2E), 0.0).astype(bf16)
    cq = cq_ref[...]
    qpos = qi * tq + lax.broadcasted_iota(i32, (ts, tq), 1)
    krow = lax.broadcasted_iota(i32, (ts, tq), 0)

    def score(c, m, masked):
        start = pl.multiple_of(c * ts, ts)
        s = jnp.dot(k_ref[pl.ds(start, ts), :].astype(bf16), qt, preferred_element_type=f32)
        ck = jnp.broadcast_to(ck_ref[c], (LANES, ts)).T
        s = s + cq - jnp.concatenate([ck] * (tq // LANES), axis=1)
        if masked:
            s = jnp.where(start + krow <= qpos, s, NEG_BIG)
        s_scr[c] = s
        return jnp.maximum(m, jnp.max(s, axis=0, keepdims=True))

    nfull = (qi * tq) // ts
    m = lax.fori_loop(0, nfull, lambda c, m: score(c, m, False), jnp.full((1, tq), NEG_BIG, f32))
    m = lax.fori_loop(nfull, nc, lambda c, m: score(c, m, True), m)

    def attend(c, acc):
        p = jnp.exp2(s_scr[c] - m)
        return acc + jnp.dot(vt_ref[c], p.astype(bf16), preferred_element_type=f32)

    acc = lax.fori_loop(0, nc, attend, jnp.zeros((HEAD_DIM_B + ONES_ROWS, tq), f32))
    o_ref[...] = acc[0:HEAD_DIM_B] / acc[HEAD_DIM_B:HEAD_DIM_B + 1]


def _fox_call(proj, vt, cq, ck, tq, ts):
    bsz, nh, _, _, _ = vt.shape
    hd = HEAD_DIM_B
    seq = proj.shape[0] // bsz
    nq = seq // tq
    kern = functools.partial(_fox_kernel, tq=tq, ts=ts)
    return pl.pallas_call(
        kern,
        grid=(bsz, nh, nq),
        in_specs=[pl.BlockSpec((tq, LANES), lambda b, h, i: (b * nq + i, C_QB // LANES + h // 2)),
                  pl.BlockSpec((seq, LANES), lambda b, h, i: (b, C_KB // LANES + h // 2)),
                  pl.BlockSpec((None, None, seq // ts, hd + ONES_ROWS, ts), lambda b, h, i: (b, h, 0, 0, 0)),
                  pl.BlockSpec((None, None, 1, tq), lambda b, h, i: (b, h, 0, i)),
                  pl.BlockSpec((None, None, seq // ts, 1, ts), lambda b, h, i: (b, h, 0, 0, 0))],
        out_specs=pl.BlockSpec((None, None, hd, tq), lambda b, h, i: (b, h, 0, i)),
        out_shape=jax.ShapeDtypeStruct((bsz, nh, hd, seq), f32),
        scratch_shapes=[pltpu.VMEM((seq // ts, ts, tq), f32)],
        compiler_params=_params(("parallel", "parallel", "arbitrary")),
        name="fox",
    )(proj, proj, vt, cq, ck)


def _outproj_kernel(oa_ref, ob_ref, x_ref, g1_ref, ga_ref, gb_ref, wa_ref, wb_ref,
                    gf_ref, sc_ref, sh_ref, wq_ref, x1_ref, h2_ref, qp_ref):
    oa = oa_ref[...]
    ob = ob_ref[...].reshape(WIDTH_B, oa.shape[0]).T
    na = oa * lax.rsqrt(jnp.mean(oa * oa, axis=-1, keepdims=True) + EPS) * ga_ref[...]
    nb = ob * lax.rsqrt(jnp.mean(ob * ob, axis=-1, keepdims=True) + EPS) * gb_ref[...]
    res = (jnp.dot(na.astype(bf16), wa_ref[...], preferred_element_type=f32)
           + jnp.dot(nb.astype(bf16), wb_ref[...], preferred_element_type=f32))
    x1 = x_ref[...] + g1_ref[0] * res
    x1_ref[...] = x1
    h2 = x1 * lax.rsqrt(jnp.mean(x1 * x1, axis=-1, keepdims=True) + EPS) * gf_ref[...]
    h2 = h2 * (1.0 + sc_ref[0]) + sh_ref[0]
    h2_ref[...] = h2
    qp_ref[...] = jnp.dot(h2.astype(bf16), wq_ref[...], preferred_element_type=f32)


def _outproj_call(oa, ob, x2, g1, ga, gb, wa, wb, gf, sc, sh, wq, seq, tm):
    n, d = x2.shape
    per_b = seq // tm
    tok = lambda i: (i, 0)
    bat = lambda i: (i // per_b, 0, 0)
    cst = lambda i: (0, 0)
    nq = wq.shape[1]
    return pl.pallas_call(
        _outproj_kernel,
        grid=(n // tm,),
        in_specs=[pl.BlockSpec((tm, WIDTH_A), tok),
                  pl.BlockSpec((None, N_HEADS_B, HEAD_DIM_B, tm), lambda i: (i // per_b, 0, 0, i % per_b)),
                  pl.BlockSpec((tm, d), tok), pl.BlockSpec((1, 1, d), bat),
                  pl.BlockSpec((1, WIDTH_A), cst), pl.BlockSpec((1, WIDTH_B), cst),
                  pl.BlockSpec((WIDTH_A, d), cst), pl.BlockSpec((WIDTH_B, d), cst),
                  pl.BlockSpec((1, d), cst), pl.BlockSpec((1, 1, d), bat),
                  pl.BlockSpec((1, 1, d), bat), pl.BlockSpec((d, nq), cst)],
        out_specs=[pl.BlockSpec((tm, d), tok), pl.BlockSpec((tm, d), tok),
                   pl.BlockSpec((tm, nq), tok)],
        out_shape=[jax.ShapeDtypeStruct((n, d), f32), jax.ShapeDtypeStruct((n, d), f32),
                   jax.ShapeDtypeStruct((n, nq), f32)],
        compiler_params=_params(("parallel",)),
        name="outproj",
    )(oa, ob, x2, g1, ga, gb, wa, wb, gf, sc, sh, wq)


def _argmax_rows(x, iota):
    vals = [x[j:j + 8] for j in range(0, x.shape[0], 8)]
    idxs = [iota[j:j + 8] for j in range(0, x.shape[0], 8)]
    while len(vals) > 1:
        nv, ni = [], []
        for a in range(0, len(vals) - 1, 2):
            keep = vals[a] >= vals[a + 1]
            nv.append(jnp.where(keep, vals[a], vals[a + 1]))
            ni.append(jnp.where(keep, idxs[a], idxs[a + 1]))
        if len(vals) % 2:
            nv.append(vals[-1])
            ni.append(idxs[-1])
        vals, idxs = nv, ni
    m = jnp.max(vals[0], axis=0, keepdims=True)
    pos = jnp.min(jnp.where(vals[0] == m, idxs[0], x.shape[0]), axis=0, keepdims=True)
    return m, pos


def _topk_rows(x, kk):
    iota = lax.broadcasted_iota(i32, x.shape, 0)
    vals, idxs = [], []
    for _ in range(kk):
        m, pos = _argmax_rows(x, iota)
        vals.append(m)
        idxs.append(pos)
        x = jnp.where(iota == pos, -jnp.inf, x)
    return jnp.concatenate(vals, axis=0), jnp.concatenate(idxs, axis=0)


CAND_PAIRS = [(a, b) for a in range(PEER_TOPK) for b in range(PEER_TOPK) if (a + 1) * (b + 1) <= PEER_TOPK]
CAND_ROWS = -(-len(CAND_PAIRS) // 8) * 8


def _cand_select():
    sel = np.zeros((2, CAND_ROWS, PEER_TOPK), np.float32)
    for r, (a, b) in enumerate(CAND_PAIRS):
        sel[0, r, a] = 1.0
        sel[1, r, b] = 1.0
    return sel


def _route_kernel(qp_ref, k1_ref, k2_ref, sel_ref, dup_ref, rows_ref, g_ref, *, nl):
    half = PEER_KEY_DIM // 2
    hp = lax.Precision.HIGHEST
    pick = lambda j, v: jnp.dot(sel_ref[j], v, preferred_element_type=f32, precision=hp)
    rows = lax.broadcasted_iota(i32, (CAND_ROWS, LANES), 0)

    def head(q):
        s1 = lax.dot_general(k1_ref[...], q[:, :half], NT_DIMS, preferred_element_type=f32, precision=hp)
        s2 = lax.dot_general(k2_ref[...], q[:, half:], NT_DIMS, preferred_element_type=f32, precision=hp)
        v1, i1 = _topk_rows(s1, PEER_TOPK)
        v2, i2 = _topk_rows(s2, PEER_TOPK)
        cand = jnp.where(rows < len(CAND_PAIRS), pick(0, v1) + pick(1, v2), -jnp.inf)
        cand_e = pick(0, i1.astype(f32)) * N_KEYS + pick(1, i2.astype(f32))
        best, experts = [], []
        for _ in range(PEER_TOPK):
            m, pos = _argmax_rows(cand, rows)
            hit = rows == pos
            experts.append(jnp.sum(jnp.where(hit, cand_e, 0.0), axis=0, keepdims=True))
            best.append(m)
            cand = jnp.where(hit, -jnp.inf, cand)
        best = jnp.concatenate(best, axis=0)
        e = jnp.exp(best - best[0:1])
        return jnp.concatenate(experts, axis=0), e / jnp.sum(e, axis=0, keepdims=True)

    def lane_tile(l, carry):
        start = pl.multiple_of(l * LANES, LANES)
        outs = [head(qp_ref[pl.ds(start, LANES), h * PEER_KEY_DIM:(h + 1) * PEER_KEY_DIM])
                for h in range(PEER_HEADS)]
        ids = jnp.concatenate([o[0] for o in outs], axis=0)
        gates = jnp.concatenate([o[1] for o in outs], axis=0)
        rows_ref[pl.ds(start, LANES), :] = (ids.T * ROWS_PER_EXPERT).astype(i32)
        g_ref[pl.ds(start, LANES), :] = jnp.dot(dup_ref[...], gates, preferred_element_type=f32,
                                                precision=hp).T
        return carry

    lax.fori_loop(0, nl, lane_tile, 0)


def _route_call(qp, k1, k2, tr):
    n = qp.shape[0]
    sel = jnp.asarray(_cand_select())
    dup = jnp.asarray(np.repeat(np.eye(PEER_SLOTS, dtype=np.float32), 2, axis=0))
    kern = functools.partial(_route_kernel, nl=tr // LANES)
    cst = lambda i: (0, 0)
    return pl.pallas_call(
        kern,
        grid=(n // tr,),
        in_specs=[pl.BlockSpec((tr, PEER_HEADS * PEER_KEY_DIM), lambda i: (i, 0)),
                  pl.BlockSpec(k1.shape, cst),
                  pl.BlockSpec(k2.shape, cst),
                  pl.BlockSpec(sel.shape, lambda i: (0, 0, 0)),
                  pl.BlockSpec(dup.shape, cst)],
        out_specs=[pl.BlockSpec((tr, PEER_SLOTS), lambda i: (i, 0)),
                   pl.BlockSpec((tr, 2 * PEER_SLOTS), lambda i: (i, 0))],
        out_shape=[jax.ShapeDtypeStruct((n, PEER_SLOTS), i32),
                   jax.ShapeDtypeStruct((n, 2 * PEER_SLOTS), f32)],
        compiler_params=_params(("parallel",)),
        name="route",
    )(qp, k1, k2, sel, dup)


ROWS_PER_EXPERT = 4
TILE_STRIDE = 136
TOK_UNROLL = 8
HALF = 512


def _pack_kernel(t_ref, o_ref):
    te = t_ref.shape[0]
    bits = lax.bitcast_convert_type(t_ref[...].astype(bf16).astype(f32), i32)
    word = (bits[:, HALF:] & jnp.int32(-65536)) | lax.shift_right_logical(bits[:, :HALF], 16)
    for r in range(ROWS_PER_EXPERT):
        o_ref[pl.ds(r, te, stride=ROWS_PER_EXPERT), :] = word[:, r * LANES:(r + 1) * LANES]


def _pack_table(tab, te=256):
    e, d = tab.shape
    return pl.pallas_call(
        _pack_kernel,
        grid=(e // te,),
        in_specs=[pl.BlockSpec((te, d), lambda i: (i, 0))],
        out_specs=pl.BlockSpec((te * ROWS_PER_EXPERT, LANES), lambda i: (i, 0)),
        out_shape=jax.ShapeDtypeStruct((e * ROWS_PER_EXPERT, LANES), i32),
        compiler_params=_params(("parallel",)),
        name="pack",
    )(tab)


def _gather_tile(tab_ref, rows_ref, t, tile_ref):
    tok_rows = rows_ref.at[t]
    for k in range(PEER_SLOTS):
        row = pl.multiple_of(tok_rows[k], ROWS_PER_EXPERT)
        tile_ref[pl.ds(k, ROWS_PER_EXPERT, stride=TILE_STRIDE), :] = tab_ref[pl.ds(row, ROWS_PER_EXPERT), :]
    chunks = [pltpu.bitcast(tile_ref[j * TILE_STRIDE:j * TILE_STRIDE + PEER_SLOTS, :], bf16)
              for j in range(ROWS_PER_EXPERT)]
    return jnp.concatenate(chunks, axis=1)


def _peer_u_kernel(ids_ref, h_ref, g_ref, u_ref, o_ref, tile_scr, act_scr, *, tt):
    even = lax.broadcasted_iota(i32, (1, 2 * PEER_SLOTS), 1) % 2 == 0

    def group(gi, carry):
        rows = []
        for tl in range(TOK_UNROLL):
            t = gi * TOK_UNROLL + tl
            r = _gather_tile(u_ref, ids_ref, t, tile_scr.at[tl % 2])
            row = h_ref[pl.ds(t, 1), :]
            h = jnp.concatenate([row[:, :HALF], row[:, HALF:]], axis=0)
            major = h.astype(bf16)
            minor = (h - major.astype(f32)).astype(bf16)
            out = lax.dot_general(jnp.concatenate([major, minor], axis=0), r, NT_DIMS,
                                  preferred_element_type=f32)
            rows.append(jnp.where(even, out[0:1] + out[2:3], out[1:2] + out[3:4]))
        act_scr[pl.ds(pl.multiple_of(gi * TOK_UNROLL, TOK_UNROLL), TOK_UNROLL), :] = (
            jnp.concatenate(rows, axis=0))
        return carry

    lax.fori_loop(0, tt // TOK_UNROLL, group, 0)
    part = act_scr[...]
    act = part + jnp.where(even, pltpu.roll(part, 2 * PEER_SLOTS - 1, 1), pltpu.roll(part, 1, 1))
    gelu = 0.5 * act * (1.0 + lax.erf(act * (2.0 ** -0.5)))
    o_ref[...] = g_ref[...] * gelu


def _peer_u_call(ids, h3, g2, u_packed, tt):
    n = ids.shape[0]
    kern = functools.partial(_peer_u_kernel, tt=tt)
    return pl.pallas_call(
        kern,
        grid=(n // tt,),
        in_specs=[pl.BlockSpec((tt, PEER_SLOTS), lambda i: (i, 0), memory_space=pltpu.SMEM),
                  pl.BlockSpec((tt, 2 * HALF), lambda i: (i, 0)),
                  pl.BlockSpec((tt, 2 * PEER_SLOTS), lambda i: (i, 0)),
                  pl.BlockSpec(u_packed.shape, lambda i: (0, 0), pipeline_mode=pl.Buffered(1))],
        out_specs=pl.BlockSpec((tt, 2 * PEER_SLOTS), lambda i: (i, 0)),
        out_shape=jax.ShapeDtypeStruct((n, 2 * PEER_SLOTS), f32),
        scratch_shapes=[pltpu.VMEM((2, ROWS_PER_EXPERT * TILE_STRIDE, LANES), i32),
                        pltpu.VMEM((tt, 2 * PEER_SLOTS), f32)],
        compiler_params=_params(("arbitrary",)),
        name="peer_u",
    )(ids, h3, g2, u_packed)


def _peer_v_kernel(ids_ref, w_ref, v_ref, o_ref, tile_scr, *, tt):
    even = lax.broadcasted_iota(i32, (1, 2 * PEER_SLOTS), 1) % 2 == 0

    def group(gi, carry):
        base = pl.multiple_of(gi * TOK_UNROLL, TOK_UNROLL)
        w8 = w_ref[pl.ds(base, TOK_UNROLL), :]
        rows = []
        for tl in range(TOK_UNROLL):
            t = gi * TOK_UNROLL + tl
            r = _gather_tile(v_ref, ids_ref, t, tile_scr.at[tl % 2])
            w = w8[tl:tl + 1]
            lhs = jnp.concatenate([jnp.where(even, w, 0.0), jnp.where(even, 0.0, w)], axis=0)
            out = jnp.dot(lhs.astype(bf16), r, preferred_element_type=f32)
            rows.append(jnp.concatenate([out[0:1], out[1:2]], axis=1))
        o_ref[pl.ds(base, TOK_UNROLL), :] = jnp.concatenate(rows, axis=0)
        return carry

    lax.fori_loop(0, tt // TOK_UNROLL, group, 0)


def _peer_v_call(ids, wgt2, v_packed, tt):
    n = ids.shape[0]
    kern = functools.partial(_peer_v_kernel, tt=tt)
    return pl.pallas_call(
        kern,
        grid=(n // tt,),
        in_specs=[pl.BlockSpec((tt, PEER_SLOTS), lambda i: (i, 0), memory_space=pltpu.SMEM),
                  pl.BlockSpec((tt, 2 * PEER_SLOTS), lambda i: (i, 0)),
                  pl.BlockSpec(v_packed.shape, lambda i: (0, 0), pipeline_mode=pl.Buffered(1))],
        out_specs=pl.BlockSpec((tt, 2 * HALF), lambda i: (i, 0)),
        out_shape=jax.ShapeDtypeStruct((n, 2 * HALF), f32),
        scratch_shapes=[pltpu.VMEM((2, ROWS_PER_EXPERT * TILE_STRIDE, LANES), i32)],
        compiler_params=_params(("arbitrary",)),
        name="peer_v",
    )(ids, wgt2, v_packed)


def _final_kernel(x1_ref, p_ref, g2_ref, gf_ref, o_ref):
    x2 = x1_ref[...] + g2_ref[0] * p_ref[...]
    o_ref[...] = x2 * lax.rsqrt(jnp.mean(x2 * x2, axis=-1, keepdims=True) + EPS) * gf_ref[...]


def _final_call(x1, peer, g2, gf, seq, tm):
    n, d = x1.shape
    per_b = seq // tm
    tok = lambda i: (i, 0)
    return pl.pallas_call(
        _final_kernel,
        grid=(n // tm,),
        in_specs=[pl.BlockSpec((tm, d), tok), pl.BlockSpec((tm, d), tok),
                  pl.BlockSpec((1, 1, d), lambda i: (i // per_b, 0, 0)),
                  pl.BlockSpec((1, d), lambda i: (0, 0))],
        out_specs=pl.BlockSpec((tm, d), tok),
        out_shape=jax.ShapeDtypeStruct((n, d), f32),
        compiler_params=_params(("parallel",)),
        name="final",
    )(x1, peer, g2, gf)


def _rope_tables(positions):
    def tab(rot):
        inv = ROPE_THETA ** (-jnp.arange(0, rot, 2, dtype=f32) / rot)
        ang = positions.astype(f32)[..., None] * inv
        return jnp.cos(ang), jnp.sin(ang)

    cos_a, sin_a = tab(ROPE_DIM_A)
    cos_i, sin_i = tab(IDX_ROPE_DIM)
    b, s = positions.shape
    one = jnp.ones((b, s, 1), f32)
    zero = jnp.zeros((b, s, 4), f32)
    trig = jnp.concatenate([cos_a, cos_i, one, zero[..., :3], sin_a, sin_i, zero], axis=-1)
    return trig.reshape(b * s, TRIG)


def kernel(x, c, positions, w_ada, b_ada, g_mix, w_in, g_kv, w_uk, w_uv, b_forget,
           g_out_a, g_out_b, w_out, g_ffn, w_peer_q, peer_keys1, peer_keys2, peer_u,
           peer_v, g_final):
    b, s, d = x.shape
    n = b * s
    assert w_ada.shape[0] == 1, "single layer supported"
    tm = min(512, s)
    x2 = x.reshape(n, d)

    mod = _mod_call(c, w_ada[0], b_ada[0])
    mod = mod.reshape(b, 6, 1, d)
    shift1, scale1, gate1, shift2, scale2, gate2 = [mod[:, j] for j in range(6)]

    perm = _in_perm()
    w_in_r = jnp.where((perm >= 0)[None, :], w_in[0][:, np.maximum(perm, 0)], 0.0).astype(bf16)
    proj, kcat, ik, latt, vbt = _inproj_call(x2, scale1, shift1, g_mix[0].reshape(1, d), w_in_r,
                                             _rope_tables(positions), g_kv[0].reshape(1, KV_RANK), s, tm)

    wuk = w_uk[0].transpose(1, 0, 2).astype(bf16)
    wuvt = w_uv[0].transpose(1, 2, 0).astype(bf16)
    o_a = _dsa_call(proj, kcat.reshape(b, s, KCAT), latt, ik.reshape(b, s, LANES), wuk, wuvt,
                    b, s, tq=128, tk=tm)

    cum = _cum_call(proj, b_forget[0], b, s)
    o_bt = _fox_call(proj, vbt, cum[:, :, None, :], cum.reshape(b, N_HEADS_B, s // tm, 1, tm),
                     tq=min(512, s), ts=tm)

    wo = w_out[0].astype(bf16)
    x1, h2, qp = _outproj_call(o_a, o_bt, x2, gate1, g_out_a[0].reshape(1, WIDTH_A),
                               g_out_b[0].reshape(1, WIDTH_B), wo[:WIDTH_A], wo[WIDTH_A:],
                               g_ffn[0].reshape(1, d), scale2, shift2,
                               w_peer_q[0].astype(bf16), s, tm)

    ids, gates2 = _route_call(qp, peer_keys1[0], peer_keys2[0], tr=min(1024, n))
    tt = 64
    wgt2 = _peer_u_call(ids, h2, gates2, _pack_table(peer_u[0]), tt)
    peer = _peer_v_call(ids, wgt2, _pack_table(peer_v[0]), tt)

    out = _final_call(x1, peer, gate2, g_final.reshape(1, d), s, tm)
    return out.reshape(b, s, d)
```

```python
import functools

import jax
import jax.numpy as jnp
import numpy as np
from jax import lax
from jax.experimental import pallas as pl
from jax.experimental.pallas import tpu as pltpu

f32 = jnp.float32
bf16 = jnp.bfloat16
i32 = jnp.int32

N_HEADS_A = 8
HEAD_DIM_A = 64
ROPE_DIM_A = 16
NOPE_DIM_A = HEAD_DIM_A - ROPE_DIM_A
KV_RANK = 128
IDX_HEADS = 8
IDX_DIM = 32
IDX_ROPE_DIM = 8
TOPK_MAX = 256
N_HEADS_B = 8
HEAD_DIM_B = 64
WIDTH_A = N_HEADS_A * HEAD_DIM_A
WIDTH_B = N_HEADS_B * HEAD_DIM_B
ROPE_THETA = 500000.0
PEER_HEADS = 8
N_KEYS = 128
PEER_KEY_DIM = 128
PEER_TOPK = 16
PEER_SLOTS = PEER_HEADS * PEER_TOPK
EPS = 1e-6
IN_SIZES = (WIDTH_A, KV_RANK, ROPE_DIM_A, IDX_HEADS * IDX_DIM, IDX_DIM, IDX_HEADS,
            WIDTH_B, WIDTH_B, WIDTH_B, N_HEADS_B)

LANES = 128
NEG_BIG = -1e30
INT_MIN = -2147483648
VMEM_LIMIT = 56 * 1024 * 1024

C_R1 = 0
C_R2 = 128
C_LAT = 256
C_QN = 384
C_MISC = 768
C_QB = 1024
C_KB = 1536
C_VB = 2048
C_TOTAL = 2560

NT_DIMS = (((1,), (1,)), ((), ()))


def _in_perm():
    offs = np.cumsum((0,) + IN_SIZES)
    o_qa, o_lat, o_kr, o_iq, o_ik, o_iw, o_qb, o_kb, o_vb, o_fb = offs[:10]
    perm = -np.ones((C_TOTAL,), np.int64)
    ha, hi = ROPE_DIM_A // 2, IDX_ROPE_DIM // 2
    for h in range(N_HEADS_A):
        for j in range(ha):
            perm[C_R1 + h * ha + j] = o_qa + h * HEAD_DIM_A + j
            perm[C_R1 + 64 + h * ha + j] = o_qa + h * HEAD_DIM_A + ha + j
        for j in range(NOPE_DIM_A):
            perm[C_QN + h * NOPE_DIM_A + j] = o_qa + h * HEAD_DIM_A + ROPE_DIM_A + j
    for h in range(IDX_HEADS):
        for j in range(hi):
            perm[C_R2 + j * IDX_HEADS + h] = o_iq + h * IDX_DIM + j
            perm[C_R2 + 64 + j * IDX_HEADS + h] = o_iq + h * IDX_DIM + hi + j
        for j in range(IDX_DIM - IDX_ROPE_DIM):
            perm[C_MISC + j * IDX_HEADS + h] = o_iq + h * IDX_DIM + IDX_ROPE_DIM + j
    for j in range(ha):
        perm[C_R2 + 32 + j] = o_kr + j
        perm[C_R2 + 64 + 32 + j] = o_kr + ha + j
    for j in range(hi):
        perm[C_R2 + 40 + j] = o_ik + j
        perm[C_R2 + 64 + 40 + j] = o_ik + hi + j
    for j in range(IDX_DIM - IDX_ROPE_DIM):
        perm[C_MISC + 192 + j] = o_ik + IDX_ROPE_DIM + j
    for j in range(IDX_HEADS):
        perm[C_MISC + 216 + j] = o_iw + j
    for j in range(N_HEADS_B):
        perm[C_MISC + 224 + j] = o_fb + j
    perm[C_LAT:C_LAT + KV_RANK] = o_lat + np.arange(KV_RANK)
    perm[C_QB:C_QB + WIDTH_B] = o_qb + np.arange(WIDTH_B)
    perm[C_KB:C_KB + WIDTH_B] = o_kb + np.arange(WIDTH_B)
    perm[C_VB:C_VB + WIDTH_B] = o_vb + np.arange(WIDTH_B)
    return perm


def _params(sem):
    return pltpu.CompilerParams(dimension_semantics=sem, vmem_limit_bytes=VMEM_LIMIT)


def _mod_kernel(c_ref, w_ref, b_ref, o_ref):
    c = c_ref[...]
    ca = c * jax.nn.sigmoid(c)
    o_ref[...] = jnp.dot(ca, w_ref[...], preferred_element_type=f32,
                         precision=lax.Precision.HIGHEST) + b_ref[...]


def _mod_call(c, w, b):
    bsz, d = c.shape
    n = w.shape[1]
    return pl.pallas_call(
        _mod_kernel,
        grid=(n // d,),
        in_specs=[pl.BlockSpec((bsz, d), lambda j: (0, 0)),
                  pl.BlockSpec((d, d), lambda j: (0, j)),
                  pl.BlockSpec((1, d), lambda j: (0, j))],
        out_specs=pl.BlockSpec((bsz, d), lambda j: (0, j)),
        out_shape=jax.ShapeDtypeStruct((bsz, n), f32),
        compiler_params=_params(("arbitrary",)),
        name="mod",
    )(c, w, b.reshape(1, n))


KCAT = 256
ONES_ROWS = 16


def _with_ones_rows(vt):
    lead = vt.shape[:-2]
    t = vt.shape[-1]
    ones = jnp.ones(lead + (1, t), vt.dtype)
    zeros = jnp.zeros(lead + (ONES_ROWS - 1, t), vt.dtype)
    return jnp.concatenate([vt, ones, zeros], axis=-2)


def _lanes_from(lane, pieces):
    out = jnp.zeros(lane.shape, f32)
    for end, src, start in reversed(pieces):
        begin = max([e for e, _, _ in pieces if e < end], default=0)
        out = jnp.where(lane < end, pltpu.roll(src, (begin - start) % LANES, 1), out)
    return out


TRIG = 32


def _trig_expand():
    e = np.zeros((TRIG, 4 * LANES), np.float32)
    cos_a, cos_i, one, sin_a, sin_i = 0, 8, 12, 16, 24
    for lane in range(LANES):
        l, sign = lane % 64, (-1.0 if lane < 64 else 1.0)
        e[cos_a + l % 8, lane] = 1.0
        e[sin_a + l % 8, LANES + lane] = sign
        if l < 32:
            e[cos_i + l // IDX_HEADS, 2 * LANES + lane] = 1.0
            e[sin_i + l // IDX_HEADS, 3 * LANES + lane] = sign
        elif l < 40:
            e[cos_a + l - 32, 2 * LANES + lane] = 1.0
            e[sin_a + l - 32, 3 * LANES + lane] = sign
        elif l < 44:
            e[cos_i + l - 40, 2 * LANES + lane] = 1.0
            e[sin_i + l - 40, 3 * LANES + lane] = sign
        else:
            e[one, 2 * LANES + lane] = 1.0
    return e


def _inproj_kernel(x_ref, sc_ref, sh_ref, g_ref, w_ref, trig_ref, exp_ref,
                   gkv_ref, o_ref, kcat_ref, ik_ref, latt_ref, vbt_ref):
    x = x_ref[...]
    ms = jnp.mean(x * x, axis=-1, keepdims=True)
    h = x * lax.rsqrt(ms + EPS) * g_ref[...]
    h = h * (1.0 + sc_ref[0]) + sh_ref[0]
    p = jnp.dot(h.astype(bf16), w_ref[...], preferred_element_type=f32)
    tab = jnp.dot(trig_ref[...], exp_ref[...], preferred_element_type=f32,
                  precision=lax.Precision.HIGHEST)
    r1 = p[:, C_R1:C_R1 + LANES]
    o_ref[:, C_R1:C_R1 + LANES] = (r1 * tab[:, 0:LANES]
                                   + pltpu.roll(r1, 64, 1) * tab[:, LANES:2 * LANES])
    r2 = p[:, C_R2:C_R2 + LANES]
    r2 = r2 * tab[:, 2 * LANES:3 * LANES] + pltpu.roll(r2, 64, 1) * tab[:, 3 * LANES:]
    o_ref[:, C_R2:C_R2 + LANES] = r2
    lat = p[:, C_LAT:C_LAT + KV_RANK]
    lms = jnp.mean(lat * lat, axis=-1, keepdims=True)
    lat = lat * lax.rsqrt(lms + EPS) * gkv_ref[...]
    o_ref[:, C_LAT:C_LAT + KV_RANK] = lat
    o_ref[:, C_QN:] = p[:, C_QN:]

    lane = lax.broadcasted_iota(i32, r2.shape, 1)
    kr = _lanes_from(lane, [(8, r2, 32), (16, r2, 96)])
    kcat_ref[...] = jnp.concatenate([lat, kr], axis=1).astype(bf16)
    misc_hi = p[:, C_MISC + LANES:C_MISC + 2 * LANES]
    ik_ref[...] = _lanes_from(lane, [(4, r2, 40), (8, r2, 104), (32, misc_hi, 64)]).astype(bf16)
    latt_ref[...] = _with_ones_rows(lat.T.astype(bf16))
    vt = p[:, C_VB:C_VB + WIDTH_B].T.astype(bf16)
    vbt_ref[...] = _with_ones_rows(vt.reshape(N_HEADS_B, HEAD_DIM_B, vt.shape[1]))


def _inproj_call(x2, sc, sh, g, w, trig, gkv, seq, tm):
    n, d = x2.shape
    per_b = seq // tm
    bsz = n // seq
    tok = lambda i: (i, 0)
    bat = lambda i: (i // per_b, 0, 0)
    cst = lambda i: (0, 0)
    return pl.pallas_call(
        _inproj_kernel,
        grid=(n // tm,),
        in_specs=[pl.BlockSpec((tm, d), tok),
                  pl.BlockSpec((1, 1, d), bat),
                  pl.BlockSpec((1, 1, d), bat),
                  pl.BlockSpec((1, d), cst),
                  pl.BlockSpec((d, C_TOTAL), cst),
                  pl.BlockSpec((tm, TRIG), tok),
                  pl.BlockSpec((TRIG, 4 * LANES), cst),
                  pl.BlockSpec((1, KV_RANK), cst)],
        out_specs=[pl.BlockSpec((tm, C_TOTAL), tok),
                   pl.BlockSpec((tm, KCAT), tok),
                   pl.BlockSpec((tm, LANES), tok),
                   pl.BlockSpec((None, None, KV_RANK + ONES_ROWS, tm),
                                lambda i: (i // per_b, i % per_b, 0, 0)),
                   pl.BlockSpec((None, N_HEADS_B, None, HEAD_DIM_B + ONES_ROWS, tm),
                                lambda i: (i // per_b, 0, i % per_b, 0, 0))],
        out_shape=[jax.ShapeDtypeStruct((n, C_TOTAL), f32),
                   jax.ShapeDtypeStruct((n, KCAT), bf16),
                   jax.ShapeDtypeStruct((n, LANES), bf16),
                   jax.ShapeDtypeStruct((bsz, per_b, KV_RANK + ONES_ROWS, tm), bf16),
                   jax.ShapeDtypeStruct((bsz, N_HEADS_B, per_b, HEAD_DIM_B + ONES_ROWS, tm), bf16)],
        compiler_params=_params(("parallel",)),
        name="inproj",
    )(x2, sc, sh, g, w, trig, jnp.asarray(_trig_expand()), gkv)


LOG2E = 1.4426950408889634


def _sortable(x):
    bits = lax.bitcast_convert_type(x, i32)
    return bits ^ ((bits >> 31) & 0x7FFFFFFF)


def _dsa_kernel(r1_ref, r2_ref, qn_ref, misc_ref, kcat_ref, latt_ref, ik_ref, wuk_ref, wuvt_ref,
                o_ref, key_scr, iq_scr, m_scr, acc_scr, *, tq, tk, topk, nbits, seq):
    nh = N_HEADS_A
    qi = pl.program_id(1)
    nc = ((qi + 1) * tq + tk - 1) // tk
    qpos = qi * tq + lax.broadcasted_iota(i32, (tk, tq), 1)
    krow = lax.broadcasted_iota(i32, (tk, tq), 0)

    r2t = r2_ref[...].T
    mt = misc_ref[...].T
    iq_scr[...] = jnp.concatenate([r2t[0:32], r2t[64:96], mt[0:192]], axis=0)
    iqt = jnp.concatenate([iq_scr[pl.ds(h, IDX_DIM, stride=IDX_HEADS), :] for h in range(IDX_HEADS)],
                          axis=1)
    iqt = jnp.concatenate([iqt, jnp.zeros((LANES - IDX_DIM, IDX_HEADS * tq), f32)], axis=0).astype(bf16)
    iwt = mt[216:224] * (IDX_DIM ** -0.5 * IDX_HEADS ** -0.5)

    def score_chunk(c, carry):
        start = pl.multiple_of(c * tk, tk)
        d = jnp.dot(ik_ref[pl.ds(start, tk), :], iqt, preferred_element_type=f32)
        sc = jnp.zeros((tk, tq), f32)
        for h in range(IDX_HEADS):
            sc = sc + jnp.maximum(d[:, h * tq:(h + 1) * tq], 0.0) * iwt[h:h + 1, :]
        sc = jnp.where(start + krow <= qpos, sc, -jnp.inf)
        key_scr[c] = _sortable(sc)
        return carry

    lax.fori_loop(0, nc, score_chunk, 0)

    def count(pred):
        def body(c, acc):
            m = pred(key_scr[c], c * tk + krow)
            return acc + jnp.sum(m.reshape(tk // 8, 8, tq), axis=0)
        acc = lax.fori_loop(0, nc, body, jnp.zeros((8, tq), i32))
        return jnp.sum(acc, axis=0, keepdims=True)

    def bit_body(i, thr):
        cand = thr + lax.shift_left(jnp.int32(1), 31 - i)
        cnt = count(lambda k, col: jnp.where(k >= cand, 1, 0))
        return jnp.where(cnt >= topk, cand, thr)

    thr = lax.fori_loop(0, 32, bit_body, jnp.full((1, tq), INT_MIN, i32))
    need = topk - count(lambda k, col: jnp.where(k > thr, 1, 0))
    n_eq = count(lambda k, col: jnp.where(k == thr, 1, 0))

    def tie_search():
        def tie_body(i, last):
            cand = last + lax.shift_left(jnp.int32(1), nbits - 1 - i)
            below = count(lambda k, col: jnp.where(k == thr, jnp.where(col < cand, 1, 0), 0))
            return jnp.where(below < need, cand, last)
        return lax.fori_loop(0, nbits, tie_body, jnp.zeros((1, tq), i32))

    ambiguous = jnp.max(jnp.where(n_eq > need, 1, 0)) > 0
    last = lax.cond(ambiguous, tie_search, lambda: jnp.full((1, tq), seq, i32))

    scale = HEAD_DIM_A ** -0.5 * LOG2E
    r1t = r1_ref[...].T
    qnt = qn_ref[...].T.astype(bf16)
    pad = jnp.zeros((KCAT - KV_RANK - ROPE_DIM_A, tq), f32)
    cols = []
    for h in range(nh):
        ql = jnp.dot(wuk_ref[h], qnt[h * NOPE_DIM_A:(h + 1) * NOPE_DIM_A], preferred_element_type=f32)
        cols.append(jnp.concatenate([ql, r1t[h * 8:(h + 1) * 8], r1t[64 + h * 8:64 + (h + 1) * 8], pad],
                                    axis=0))
    qcat = (jnp.concatenate(cols, axis=1) * scale).astype(bf16)

    m_scr[...] = jnp.full(m_scr.shape, NEG_BIG, f32)
    acc_scr[...] = jnp.zeros(acc_scr.shape, f32)

    def attend_chunk(c, carry):
        start = pl.multiple_of(c * tk, tk)
        s = jnp.dot(kcat_ref[pl.ds(start, tk), :], qcat, preferred_element_type=f32)
        k = key_scr[c]
        col = start + krow
        tie = jnp.where(k == thr, jnp.where(col <= last, 0.0, NEG_BIG), NEG_BIG)
        bias = jnp.where(col <= qpos, jnp.where(k > thr, 0.0, tie), NEG_BIG)
        s = s + jnp.concatenate([bias] * nh, axis=1)
        m_old = m_scr[...]
        m_new = jnp.maximum(m_old, jnp.max(s, axis=0, keepdims=True))
        p = jnp.exp2(s - m_new)
        alpha = jnp.exp2(m_old - m_new)
        acc_scr[...] = alpha * acc_scr[...] + jnp.dot(latt_ref[c], p.astype(bf16),
                                                      preferred_element_type=f32)
        m_scr[...] = m_new
        return carry

    lax.fori_loop(0, nc, attend_chunk, 0)

    ot = (acc_scr[0:KV_RANK, :] / acc_scr[KV_RANK:KV_RANK + 1, :]).astype(bf16)
    outs = [jnp.dot(wuvt_ref[h], ot[:, h * tq:(h + 1) * tq], preferred_element_type=f32)
            for h in range(nh)]
    o_ref[...] = jnp.concatenate(outs, axis=0).T


def _dsa_call(proj, kcat, latt, ik, wuk, wuvt, bsz, seq, tq, tk):
    nh = N_HEADS_A
    topk = min(TOPK_MAX, seq // 4)
    nbits = max(1, (seq - 1).bit_length())
    nq = seq // tq
    kern = functools.partial(_dsa_kernel, tq=tq, tk=tk, topk=topk, nbits=nbits, seq=seq)
    qcol = lambda width, blk: pl.BlockSpec((tq, width), lambda b, i: (b * nq + i, blk))
    return pl.pallas_call(
        kern,
        grid=(bsz, nq),
        in_specs=[qcol(LANES, C_R1 // LANES), qcol(LANES, C_R2 // LANES),
                  qcol(C_MISC - C_QN, C_QN // (C_MISC - C_QN)), qcol(256, C_MISC // 256),
                  pl.BlockSpec((None, seq, KCAT), lambda b, i: (b, 0, 0)),
                  pl.BlockSpec((None, seq // tk, KV_RANK + ONES_ROWS, tk), lambda b, i: (b, 0, 0, 0)),
                  pl.BlockSpec((None, seq, LANES), lambda b, i: (b, 0, 0)),
                  pl.BlockSpec(wuk.shape, lambda b, i: (0, 0, 0)),
                  pl.BlockSpec(wuvt.shape, lambda b, i: (0, 0, 0))],
        out_specs=pl.BlockSpec((tq, WIDTH_A), lambda b, i: (b * nq + i, 0)),
        out_shape=jax.ShapeDtypeStruct((bsz * seq, WIDTH_A), f32),
        scratch_shapes=[pltpu.VMEM((seq // tk, tk, tq), i32),
                        pltpu.VMEM((IDX_HEADS * IDX_DIM, tq), f32),
                        pltpu.VMEM((1, nh * tq), f32),
                        pltpu.VMEM((KV_RANK + ONES_ROWS, nh * tq), f32)],
        compiler_params=_params(("parallel", "arbitrary")),
        name="dsa",
    )(proj, proj, proj, proj, kcat, latt, ik, wuk, wuvt)


C_FB = C_MISC + 224
CUM_CHUNK = 512


def _cum_kernel(p_ref, b_ref, o_ref):
    seq = p_ref.shape[0]
    lane0 = C_FB % LANES
    x = p_ref[...].T[lane0:lane0 + N_HEADS_B] + b_ref[...]
    ls = jnp.minimum(x, 0.0) - jnp.log(1.0 + jnp.exp(-jnp.abs(x)))
    row = lax.broadcasted_iota(i32, (CUM_CHUNK, CUM_CHUNK), 0)
    col = lax.broadcasted_iota(i32, (CUM_CHUNK, CUM_CHUNK), 1)
    upper = jnp.where(row <= col, 1.0, 0.0)
    carry = jnp.zeros((N_HEADS_B, 1), f32)
    for c in range(seq // CUM_CHUNK):
        part = ls[:, c * CUM_CHUNK:(c + 1) * CUM_CHUNK]
        cs = jnp.dot(part, upper, preferred_element_type=f32, precision=lax.Precision.HIGHEST) + carry
        o_ref[:, c * CUM_CHUNK:(c + 1) * CUM_CHUNK] = cs * LOG2E
        carry = cs[:, CUM_CHUNK - 1:CUM_CHUNK]


def _cum_call(proj, b_forget, bsz, seq):
    assert seq % CUM_CHUNK == 0
    return pl.pallas_call(
        _cum_kernel,
        grid=(bsz,),
        in_specs=[pl.BlockSpec((seq, LANES), lambda b: (b, C_FB // LANES)),
                  pl.BlockSpec((N_HEADS_B, 1), lambda b: (0, 0))],
        out_specs=pl.BlockSpec((None, N_HEADS_B, seq), lambda b: (b, 0, 0)),
        out_shape=jax.ShapeDtypeStruct((bsz, N_HEADS_B, seq), f32),
        compiler_params=_params(("parallel",)),
        name="cum",
    )(proj, b_forget.reshape(N_HEADS_B, 1))


def _fox_kernel(q_ref, k_ref, vt_ref, cq_ref, ck_ref, o_ref, s_scr, *, tq, ts):
    qi = pl.program_id(2)
    nc = ((qi + 1) * tq + ts - 1) // ts
    mine = lax.broadcasted_iota(i32, (LANES, tq), 0) // HEAD_DIM_B == pl.program_id(1) % 2
    qt = jnp.where(mine, q_ref[...].T * (HEAD_DIM_B ** -0.5 * LOG2E), 0.0).astype(bf16)
    cq = cq_ref[...]
    qpos = qi * tq + lax.broadcasted_iota(i32, (ts, tq), 1)
    krow = lax.broadcasted_iota(i32, (ts, tq), 0)

    def score(c, m, masked):
        start = pl.multiple_of(c * ts, ts)
        s = jnp.dot(k_ref[pl.ds(start, ts), :].astype(bf16), qt, preferred_element_type=f32)
        ck = jnp.broadcast_to(ck_ref[c], (LANES, ts)).T
        s = s + cq - jnp.concatenate([ck] * (tq // LANES), axis=1)
        if masked:
            s = jnp.where(start + krow <= qpos, s, NEG_BIG)
        s_scr[c] = s
        return jnp.maximum(m, jnp.max(s, axis=0, keepdims=True))

    nfull = (qi * tq) // ts
    m = lax.fori_loop(0, nfull, lambda c, m: score(c, m, False), jnp.full((1, tq), NEG_BIG, f32))
    m = lax.fori_loop(nfull, nc, lambda c, m: score(c, m, True), m)

    def attend(c, acc):
        p = jnp.exp2(s_scr[c] - m)
        return acc + jnp.dot(vt_ref[c], p.astype(bf16), preferred_element_type=f32)

    acc = lax.fori_loop(0, nc, attend, jnp.zeros((HEAD_DIM_B + ONES_ROWS, tq), f32))
    o_ref[...] = acc[0:HEAD_DIM_B] / acc[HEAD_DIM_B:HEAD_DIM_B + 1]


def _fox_call(proj, vt, cq, ck, tq, ts):
    bsz, nh, _, _, _ = vt.shape
    hd = HEAD_DIM_B
    seq = proj.shape[0] // bsz
    nq = seq // tq
    kern = functools.partial(_fox_kernel, tq=tq, ts=ts)
    return pl.pallas_call(
        kern,
        grid=(bsz, nh, nq),
        in_specs=[pl.BlockSpec((tq, LANES), lambda b, h, i: (b * nq + i, C_QB // LANES + h // 2)),
                  pl.BlockSpec((seq, LANES), lambda b, h, i: (b, C_KB // LANES + h // 2)),
                  pl.BlockSpec((None, None, seq // ts, hd + ONES_ROWS, ts), lambda b, h, i: (b, h, 0, 0, 0)),
                  pl.BlockSpec((None, None, 1, tq), lambda b, h, i: (b, h, 0, i)),
                  pl.BlockSpec((None, None, seq // ts, 1, ts), lambda b, h, i: (b, h, 0, 0, 0))],
        out_specs=pl.BlockSpec((None, None, hd, tq), lambda b, h, i: (b, h, 0, i)),
        out_shape=jax.ShapeDtypeStruct((bsz, nh, hd, seq), f32),
        scratch_shapes=[pltpu.VMEM((seq // ts, ts, tq), f32)],
        compiler_params=_params(("parallel", "parallel", "arbitrary")),
        name="fox",
    )(proj, proj, vt, cq, ck)


def _outproj_kernel(oa_ref, ob_ref, x_ref, g1_ref, ga_ref, gb_ref, wa_ref, wb_ref,
                    gf_ref, sc_ref, sh_ref, wq_ref, x1_ref, h2_ref, qp_ref):
    oa = oa_ref[...]
    ob = ob_ref[...].reshape(WIDTH_B, oa.shape[0]).T
    na = oa * lax.rsqrt(jnp.mean(oa * oa, axis=-1, keepdims=True) + EPS) * ga_ref[...]
    nb = ob * lax.rsqrt(jnp.mean(ob * ob, axis=-1, keepdims=True) + EPS) * gb_ref[...]
    res = (jnp.dot(na.astype(bf16), wa_ref[...], preferred_element_type=f32)
           + jnp.dot(nb.astype(bf16), wb_ref[...], preferred_element_type=f32))
    x1 = x_ref[...] + g1_ref[0] * res
    x1_ref[...] = x1
    h2 = x1 * lax.rsqrt(jnp.mean(x1 * x1, axis=-1, keepdims=True) + EPS) * gf_ref[...]
    h2 = h2 * (1.0 + sc_ref[0]) + sh_ref[0]
    h2_ref[...] = h2
    qp_ref[...] = jnp.dot(h2.astype(bf16), wq_ref[...], preferred_element_type=f32)


def _outproj_call(oa, ob, x2, g1, ga, gb, wa, wb, gf, sc, sh, wq, seq, tm):
    n, d = x2.shape
    per_b = seq // tm
    tok = lambda i: (i, 0)
    bat = lambda i: (i // per_b, 0, 0)
    cst = lambda i: (0, 0)
    nq = wq.shape[1]
    return pl.pallas_call(
        _outproj_kernel,
        grid=(n // tm,),
        in_specs=[pl.BlockSpec((tm, WIDTH_A), tok),
                  pl.BlockSpec((None, N_HEADS_B, HEAD_DIM_B, tm), lambda i: (i // per_b, 0, 0, i % per_b)),
                  pl.BlockSpec((tm, d), tok), pl.BlockSpec((1, 1, d), bat),
                  pl.BlockSpec((1, WIDTH_A), cst), pl.BlockSpec((1, WIDTH_B), cst),
                  pl.BlockSpec((WIDTH_A, d), cst), pl.BlockSpec((WIDTH_B, d), cst),
                  pl.BlockSpec((1, d), cst), pl.BlockSpec((1, 1, d), bat),
                  pl.BlockSpec((1, 1, d), bat), pl.BlockSpec((d, nq), cst)],
        out_specs=[pl.BlockSpec((tm, d), tok), pl.BlockSpec((tm, d), tok),
                   pl.BlockSpec((tm, nq), tok)],
        out_shape=[jax.ShapeDtypeStruct((n, d), f32), jax.ShapeDtypeStruct((n, d), f32),
                   jax.ShapeDtypeStruct((n, nq), f32)],
        compiler_params=_params(("parallel",)),
        name="outproj",
    )(oa, ob, x2, g1, ga, gb, wa, wb, gf, sc, sh, wq)


def _argmax_rows(x, iota):
    vals = [x[j:j + 8] for j in range(0, x.shape[0], 8)]
    idxs = [iota[j:j + 8] for j in range(0, x.shape[0], 8)]
    while len(vals) > 1:
        nv, ni = [], []
        for a in range(0, len(vals) - 1, 2):
            keep = vals[a] >= vals[a + 1]
            nv.append(jnp.where(keep, vals[a], vals[a + 1]))
            ni.append(jnp.where(keep, idxs[a], idxs[a + 1]))
        if len(vals) % 2:
            nv.append(vals[-1])
            ni.append(idxs[-1])
        vals, idxs = nv, ni
    m = jnp.max(vals[0], axis=0, keepdims=True)
    pos = jnp.min(jnp.where(vals[0] == m, idxs[0], x.shape[0]), axis=0, keepdims=True)
    return m, pos


def _topk_rows(x, kk):
    iota = lax.broadcasted_iota(i32, x.shape, 0)
    vals, idxs = [], []
    for _ in range(kk):
        m, pos = _argmax_rows(x, iota)
        vals.append(m)
        idxs.append(pos)
        x = jnp.where(iota == pos, -jnp.inf, x)
    return jnp.concatenate(vals, axis=0), jnp.concatenate(idxs, axis=0)


CAND_PAIRS = [(a, b) for a in range(PEER_TOPK) for b in range(PEER_TOPK) if (a + 1) * (b + 1) <= PEER_TOPK]
CAND_ROWS = -(-len(CAND_PAIRS) // 8) * 8


def _cand_select():
    sel = np.zeros((2, CAND_ROWS, PEER_TOPK), np.float32)
    for r, (a, b) in enumerate(CAND_PAIRS):
        sel[0, r, a] = 1.0
        sel[1, r, b] = 1.0
    return sel


def _route_kernel(qp_ref, k1_ref, k2_ref, sel_ref, dup_ref, rows_ref, g_ref, *, nl):
    half = PEER_KEY_DIM // 2
    hp = lax.Precision.HIGHEST
    pick = lambda j, v: jnp.dot(sel_ref[j], v, preferred_element_type=f32, precision=hp)
    rows = lax.broadcasted_iota(i32, (CAND_ROWS, LANES), 0)

    def head(q):
        s1 = lax.dot_general(k1_ref[...], q[:, :half], NT_DIMS, preferred_element_type=f32, precision=hp)
        s2 = lax.dot_general(k2_ref[...], q[:, half:], NT_DIMS, preferred_element_type=f32, precision=hp)
        v1, i1 = _topk_rows(s1, PEER_TOPK)
        v2, i2 = _topk_rows(s2, PEER_TOPK)
        cand = jnp.where(rows < len(CAND_PAIRS), pick(0, v1) + pick(1, v2), -jnp.inf)
        cand_e = pick(0, i1.astype(f32)) * N_KEYS + pick(1, i2.astype(f32))
        best, experts = [], []
        for _ in range(PEER_TOPK):
            m, pos = _argmax_rows(cand, rows)
            hit = rows == pos
            experts.append(jnp.sum(jnp.where(hit, cand_e, 0.0), axis=0, keepdims=True))
            best.append(m)
            cand = jnp.where(hit, -jnp.inf, cand)
        best = jnp.concatenate(best, axis=0)
        e = jnp.exp(best - best[0:1])
        return jnp.concatenate(experts, axis=0), e / jnp.sum(e, axis=0, keepdims=True)

    def lane_tile(l, carry):
        start = pl.multiple_of(l * LANES, LANES)
        outs = [head(qp_ref[pl.ds(start, LANES), h * PEER_KEY_DIM:(h + 1) * PEER_KEY_DIM])
                for h in range(PEER_HEADS)]
        ids = jnp.concatenate([o[0] for o in outs], axis=0)
        gates = jnp.concatenate([o[1] for o in outs], axis=0)
        rows_ref[pl.ds(start, LANES), :] = (ids.T * ROWS_PER_EXPERT).astype(i32)
        g_ref[pl.ds(start, LANES), :] = jnp.dot(dup_ref[...], gates, preferred_element_type=f32,
                                                precision=hp).T
        return carry

    lax.fori_loop(0, nl, lane_tile, 0)


def _route_call(qp, k1, k2, tr):
    n = qp.shape[0]
    sel = jnp.asarray(_cand_select())
    dup = jnp.asarray(np.repeat(np.eye(PEER_SLOTS, dtype=np.float32), 2, axis=0))
    kern = functools.partial(_route_kernel, nl=tr // LANES)
    cst = lambda i: (0, 0)
    return pl.pallas_call(
        kern,
        grid=(n // tr,),
        in_specs=[pl.BlockSpec((tr, PEER_HEADS * PEER_KEY_DIM), lambda i: (i, 0)),
                  pl.BlockSpec(k1.shape, cst),
                  pl.BlockSpec(k2.shape, cst),
                  pl.BlockSpec(sel.shape, lambda i: (0, 0, 0)),
                  pl.BlockSpec(dup.shape, cst)],
        out_specs=[pl.BlockSpec((tr, PEER_SLOTS), lambda i: (i, 0)),
                   pl.BlockSpec((tr, 2 * PEER_SLOTS), lambda i: (i, 0))],
        out_shape=[jax.ShapeDtypeStruct((n, PEER_SLOTS), i32),
                   jax.ShapeDtypeStruct((n, 2 * PEER_SLOTS), f32)],
        compiler_params=_params(("parallel",)),
        name="route",
    )(qp, k1, k2, sel, dup)


ROWS_PER_EXPERT = 4
TILE_STRIDE = 136
TOK_UNROLL = 64
HALF = 512


def _pack_kernel(t_ref, o_ref):
    te = t_ref.shape[0]
    bits = lax.bitcast_convert_type(t_ref[...].astype(bf16).astype(f32), i32)
    word = (bits[:, HALF:] & jnp.int32(-65536)) | lax.shift_right_logical(bits[:, :HALF], 16)
    for r in range(ROWS_PER_EXPERT):
        o_ref[pl.ds(r, te, stride=ROWS_PER_EXPERT), :] = word[:, r * LANES:(r + 1) * LANES]


def _pack_table(tab, te=256):
    e, d = tab.shape
    return pl.pallas_call(
        _pack_kernel,
        grid=(e // te,),
        in_specs=[pl.BlockSpec((te, d), lambda i: (i, 0))],
        out_specs=pl.BlockSpec((te * ROWS_PER_EXPERT, LANES), lambda i: (i, 0)),
        out_shape=jax.ShapeDtypeStruct((e * ROWS_PER_EXPERT, LANES), i32),
        compiler_params=_params(("parallel",)),
        name="pack",
    )(tab)


def _gather_tile(tab_ref, rows_ref, t, tile_ref):
    tok_rows = rows_ref.at[t]
    for k in range(PEER_SLOTS):
        row = pl.multiple_of(tok_rows[k], ROWS_PER_EXPERT)
        tile_ref[pl.ds(k, ROWS_PER_EXPERT, stride=TILE_STRIDE), :] = tab_ref[pl.ds(row, ROWS_PER_EXPERT), :]
    chunks = [pltpu.bitcast(tile_ref[j * TILE_STRIDE:j * TILE_STRIDE + PEER_SLOTS, :], bf16)
              for j in range(ROWS_PER_EXPERT)]
    return jnp.concatenate(chunks, axis=1)


def _peer_u_kernel(ids_ref, h_ref, g_ref, u_ref, o_ref, tile_scr, act_scr, *, tt):
    even = lax.broadcasted_iota(i32, (1, 2 * PEER_SLOTS), 1) % 2 == 0

    def group(gi, carry):
        rows = []
        for tl in range(TOK_UNROLL):
            t = gi * TOK_UNROLL + tl
            r = _gather_tile(u_ref, ids_ref, t, tile_scr.at[tl % 2])
            row = h_ref[pl.ds(t, 1), :]
            h = jnp.concatenate([row[:, :HALF], row[:, HALF:]], axis=0)
            major = h.astype(bf16)
            minor = (h - major.astype(f32)).astype(bf16)
            out = lax.dot_general(jnp.concatenate([major, minor], axis=0), r, NT_DIMS,
                                  preferred_element_type=f32)
            rows.append(jnp.where(even, out[0:1] + out[2:3], out[1:2] + out[3:4]))
        act_scr[pl.ds(pl.multiple_of(gi * TOK_UNROLL, TOK_UNROLL), TOK_UNROLL), :] = (
            jnp.concatenate(rows, axis=0))
        return carry

    lax.fori_loop(0, tt // TOK_UNROLL, group, 0)
    part = act_scr[...]
    act = part + jnp.where(even, pltpu.roll(part, 2 * PEER_SLOTS - 1, 1), pltpu.roll(part, 1, 1))
    gelu = 0.5 * act * (1.0 + lax.erf(act * (2.0 ** -0.5)))
    o_ref[...] = g_ref[...] * gelu


def _peer_u_call(ids, h3, g2, u_packed, tt):
    n = ids.shape[0]
    kern = functools.partial(_peer_u_kernel, tt=tt)
    return pl.pallas_call(
        kern,
        grid=(n // tt,),
        in_specs=[pl.BlockSpec((tt, PEER_SLOTS), lambda i: (i, 0), memory_space=pltpu.SMEM),
                  pl.BlockSpec((tt, 2 * HALF), lambda i: (i, 0)),
                  pl.BlockSpec((tt, 2 * PEER_SLOTS), lambda i: (i, 0)),
                  pl.BlockSpec(u_packed.shape, lambda i: (0, 0), pipeline_mode=pl.Buffered(1))],
        out_specs=pl.BlockSpec((tt, 2 * PEER_SLOTS), lambda i: (i, 0)),
        out_shape=jax.ShapeDtypeStruct((n, 2 * PEER_SLOTS), f32),
        scratch_shapes=[pltpu.VMEM((2, ROWS_PER_EXPERT * TILE_STRIDE, LANES), i32),
                        pltpu.VMEM((tt, 2 * PEER_SLOTS), f32)],
        compiler_params=_params(("arbitrary",)),
        name="peer_u",
    )(ids, h3, g2, u_packed)


def _peer_v_kernel(ids_ref, w_ref, v_ref, o_ref, tile_scr, *, tt):
    even = lax.broadcasted_iota(i32, (1, 2 * PEER_SLOTS), 1) % 2 == 0

    def group(gi, carry):
        base = pl.multiple_of(gi * TOK_UNROLL, TOK_UNROLL)
        w8 = w_ref[pl.ds(base, TOK_UNROLL), :]
        rows = []
        for tl in range(TOK_UNROLL):
            t = gi * TOK_UNROLL + tl
            r = _gather_tile(v_ref, ids_ref, t, tile_scr.at[tl % 2])
            w = w8[tl:tl + 1]
            lhs = jnp.concatenate([jnp.where(even, w, 0.0), jnp.where(even, 0.0, w)], axis=0)
            out = jnp.dot(lhs.astype(bf16), r, preferred_element_type=f32)
            rows.append(jnp.concatenate([out[0:1], out[1:2]], axis=1))
        o_ref[pl.ds(base, TOK_UNROLL), :] = jnp.concatenate(rows, axis=0)
        return carry

    lax.fori_loop(0, tt // TOK_UNROLL, group, 0)


def _peer_v_call(ids, wgt2, v_packed, tt):
    n = ids.shape[0]
    kern = functools.partial(_peer_v_kernel, tt=tt)
    return pl.pallas_call(
        kern,
        grid=(n // tt,),
        in_specs=[pl.BlockSpec((tt, PEER_SLOTS), lambda i: (i, 0), memory_space=pltpu.SMEM),
                  pl.BlockSpec((tt, 2 * PEER_SLOTS), lambda i: (i, 0)),
                  pl.BlockSpec(v_packed.shape, lambda i: (0, 0), pipeline_mode=pl.Buffered(1))],
        out_specs=pl.BlockSpec((tt, 2 * HALF), lambda i: (i, 0)),
        out_shape=jax.ShapeDtypeStruct((n, 2 * HALF), f32),
        scratch_shapes=[pltpu.VMEM((2, ROWS_PER_EXPERT * TILE_STRIDE, LANES), i32)],
        compiler_params=_params(("arbitrary",)),
        name="peer_v",
    )(ids, wgt2, v_packed)


def _final_kernel(x1_ref, p_ref, g2_ref, gf_ref, o_ref):
    x2 = x1_ref[...] + g2_ref[0] * p_ref[...]
    o_ref[...] = x2 * lax.rsqrt(jnp.mean(x2 * x2, axis=-1, keepdims=True) + EPS) * gf_ref[...]


def _final_call(x1, peer, g2, gf, seq, tm):
    n, d = x1.shape
    per_b = seq // tm
    tok = lambda i: (i, 0)
    return pl.pallas_call(
        _final_kernel,
        grid=(n // tm,),
        in_specs=[pl.BlockSpec((tm, d), tok), pl.BlockSpec((tm, d), tok),
                  pl.BlockSpec((1, 1, d), lambda i: (i // per_b, 0, 0)),
                  pl.BlockSpec((1, d), lambda i: (0, 0))],
        out_specs=pl.BlockSpec((tm, d), tok),
        out_shape=jax.ShapeDtypeStruct((n, d), f32),
        compiler_params=_params(("parallel",)),
        name="final",
    )(x1, peer, g2, gf)


def _rope_tables(positions):
    def tab(rot):
        inv = ROPE_THETA ** (-jnp.arange(0, rot, 2, dtype=f32) / rot)
        ang = positions.astype(f32)[..., None] * inv
        return jnp.cos(ang), jnp.sin(ang)

    cos_a, sin_a = tab(ROPE_DIM_A)
    cos_i, sin_i = tab(IDX_ROPE_DIM)
    b, s = positions.shape
    one = jnp.ones((b, s, 1), f32)
    zero = jnp.zeros((b, s, 4), f32)
    trig = jnp.concatenate([cos_a, cos_i, one, zero[..., :3], sin_a, sin_i, zero], axis=-1)
    return trig.reshape(b * s, TRIG)


def kernel(x, c, positions, w_ada, b_ada, g_mix, w_in, g_kv, w_uk, w_uv, b_forget,
           g_out_a, g_out_b, w_out, g_ffn, w_peer_q, peer_keys1, peer_keys2, peer_u,
           peer_v, g_final):
    b, s, d = x.shape
    n = b * s
    assert w_ada.shape[0] == 1, "single layer supported"
    tm = min(512, s)
    x2 = x.reshape(n, d)

    mod = _mod_call(c, w_ada[0], b_ada[0])
    mod = mod.reshape(b, 6, 1, d)
    shift1, scale1, gate1, shift2, scale2, gate2 = [mod[:, j] for j in range(6)]

    perm = _in_perm()
    w_in_r = jnp.where((perm >= 0)[None, :], w_in[0][:, np.maximum(perm, 0)], 0.0).astype(bf16)
    proj, kcat, ik, latt, vbt = _inproj_call(x2, scale1, shift1, g_mix[0].reshape(1, d), w_in_r,
                                             _rope_tables(positions), g_kv[0].reshape(1, KV_RANK), s, tm)

    wuk = w_uk[0].transpose(1, 0, 2).astype(bf16)
    wuvt = w_uv[0].transpose(1, 2, 0).astype(bf16)
    o_a = _dsa_call(proj, kcat.reshape(b, s, KCAT), latt, ik.reshape(b, s, LANES), wuk, wuvt,
                    b, s, tq=128, tk=tm)

    cum = _cum_call(proj, b_forget[0], b, s)
    o_bt = _fox_call(proj, vbt, cum[:, :, None, :], cum.reshape(b, N_HEADS_B, s // tm, 1, tm),
                     tq=min(512, s), ts=tm)

    wo = w_out[0].astype(bf16)
    x1, h2, qp = _outproj_call(o_a, o_bt, x2, gate1, g_out_a[0].reshape(1, WIDTH_A),
                               g_out_b[0].reshape(1, WIDTH_B), wo[:WIDTH_A], wo[WIDTH_A:],
                               g_ffn[0].reshape(1, d), scale2, shift2,
                               w_peer_q[0].astype(bf16), s, tm)

    ids, gates2 = _route_call(qp, peer_keys1[0], peer_keys2[0], tr=min(1024, n))
    tt = 64
    wgt2 = _peer_u_call(ids, h2, gates2, _pack_table(peer_u[0]), tt)
    peer = _peer_v_call(ids, wgt2, _pack_table(peer_v[0]), tt)

    out = _final_call(x1, peer, gate2, g_final.reshape(1, d), s, tm)
    return out.reshape(b, s, d)
```

```python
import functools

import jax
import jax.numpy as jnp
import numpy as np
from jax import lax
from jax.experimental import pallas as pl
from jax.experimental.pallas import tpu as pltpu

f32 = jnp.float32
bf16 = jnp.bfloat16
i32 = jnp.int32

N_HEADS_A = 8
HEAD_DIM_A = 64
ROPE_DIM_A = 16
NOPE_DIM_A = HEAD_DIM_A - ROPE_DIM_A
KV_RANK = 128
IDX_HEADS = 8
IDX_DIM = 32
IDX_ROPE_DIM = 8
TOPK_MAX = 256
N_HEADS_B = 8
HEAD_DIM_B = 64
WIDTH_A = N_HEADS_A * HEAD_DIM_A
WIDTH_B = N_HEADS_B * HEAD_DIM_B
ROPE_THETA = 500000.0
PEER_HEADS = 8
N_KEYS = 128
PEER_KEY_DIM = 128
PEER_TOPK = 16
PEER_SLOTS = PEER_HEADS * PEER_TOPK
EPS = 1e-6
IN_SIZES = (WIDTH_A, KV_RANK, ROPE_DIM_A, IDX_HEADS * IDX_DIM, IDX_DIM, IDX_HEADS,
            WIDTH_B, WIDTH_B, WIDTH_B, N_HEADS_B)

LANES = 128
NEG_BIG = -1e30
INT_MIN = -2147483648
VMEM_LIMIT = 56 * 1024 * 1024

C_R1 = 0
C_R2 = 128
C_LAT = 256
C_QN = 384
C_MISC = 768
C_QB = 1024
C_KB = 1536
C_VB = 2048
C_TOTAL = 2560

NT_DIMS = (((1,), (1,)), ((), ()))


def _in_perm():
    offs = np.cumsum((0,) + IN_SIZES)
    o_qa, o_lat, o_kr, o_iq, o_ik, o_iw, o_qb, o_kb, o_vb, o_fb = offs[:10]
    perm = -np.ones((C_TOTAL,), np.int64)
    ha, hi = ROPE_DIM_A // 2, IDX_ROPE_DIM // 2
    for h in range(N_HEADS_A):
        for j in range(ha):
            perm[C_R1 + h * ha + j] = o_qa + h * HEAD_DIM_A + j
            perm[C_R1 + 64 + h * ha + j] = o_qa + h * HEAD_DIM_A + ha + j
        for j in range(NOPE_DIM_A):
            perm[C_QN + h * NOPE_DIM_A + j] = o_qa + h * HEAD_DIM_A + ROPE_DIM_A + j
    for h in range(IDX_HEADS):
        for j in range(hi):
            perm[C_R2 + j * IDX_HEADS + h] = o_iq + h * IDX_DIM + j
            perm[C_R2 + 64 + j * IDX_HEADS + h] = o_iq + h * IDX_DIM + hi + j
        for j in range(IDX_DIM - IDX_ROPE_DIM):
            perm[C_MISC + j * IDX_HEADS + h] = o_iq + h * IDX_DIM + IDX_ROPE_DIM + j
    for j in range(ha):
        perm[C_R2 + 32 + j] = o_kr + j
        perm[C_R2 + 64 + 32 + j] = o_kr + ha + j
    for j in range(hi):
        perm[C_R2 + 40 + j] = o_ik + j
        perm[C_R2 + 64 + 40 + j] = o_ik + hi + j
    for j in range(IDX_DIM - IDX_ROPE_DIM):
        perm[C_MISC + 192 + j] = o_ik + IDX_ROPE_DIM + j
    for j in range(IDX_HEADS):
        perm[C_MISC + 216 + j] = o_iw + j
    for j in range(N_HEADS_B):
        perm[C_MISC + 224 + j] = o_fb + j
    perm[C_LAT:C_LAT + KV_RANK] = o_lat + np.arange(KV_RANK)
    perm[C_QB:C_QB + WIDTH_B] = o_qb + np.arange(WIDTH_B)
    perm[C_KB:C_KB + WIDTH_B] = o_kb + np.arange(WIDTH_B)
    perm[C_VB:C_VB + WIDTH_B] = o_vb + np.arange(WIDTH_B)
    return perm


def _params(sem):
    return pltpu.CompilerParams(dimension_semantics=sem, vmem_limit_bytes=VMEM_LIMIT)


def _loop_pairs(lo, hi, body, init):
    pairs = (hi - lo) // 2

    def two(i, carry):
        return body(lo + 2 * i + 1, body(lo + 2 * i, carry))

    carry = lax.fori_loop(0, pairs, two, init)
    return lax.fori_loop(lo + 2 * pairs, hi, body, carry)


def _mod_kernel(c_ref, w_ref, b_ref, o_ref):
    c = c_ref[...]
    ca = c * jax.nn.sigmoid(c)
    o_ref[...] = jnp.dot(ca, w_ref[...], preferred_element_type=f32,
                         precision=lax.Precision.HIGHEST) + b_ref[...]


def _mod_call(c, w, b):
    bsz, d = c.shape
    n = w.shape[1]
    return pl.pallas_call(
        _mod_kernel,
        grid=(n // d,),
        in_specs=[pl.BlockSpec((bsz, d), lambda j: (0, 0)),
                  pl.BlockSpec((d, d), lambda j: (0, j)),
                  pl.BlockSpec((1, d), lambda j: (0, j))],
        out_specs=pl.BlockSpec((bsz, d), lambda j: (0, j)),
        out_shape=jax.ShapeDtypeStruct((bsz, n), f32),
        compiler_params=_params(("arbitrary",)),
        name="mod",
    )(c, w, b.reshape(1, n))


KCAT = 256
ONES_ROWS = 16


def _with_ones_rows(vt):
    lead = vt.shape[:-2]
    t = vt.shape[-1]
    ones = jnp.ones(lead + (1, t), vt.dtype)
    zeros = jnp.zeros(lead + (ONES_ROWS - 1, t), vt.dtype)
    return jnp.concatenate([vt, ones, zeros], axis=-2)


def _lanes_from(lane, pieces):
    out = jnp.zeros(lane.shape, f32)
    for end, src, start in reversed(pieces):
        begin = max([e for e, _, _ in pieces if e < end], default=0)
        out = jnp.where(lane < end, pltpu.roll(src, (begin - start) % LANES, 1), out)
    return out


TRIG = 32


def _trig_expand():
    e = np.zeros((TRIG, 4 * LANES), np.float32)
    cos_a, cos_i, one, sin_a, sin_i = 0, 8, 12, 16, 24
    for lane in range(LANES):
        l, sign = lane % 64, (-1.0 if lane < 64 else 1.0)
        e[cos_a + l % 8, lane] = 1.0
        e[sin_a + l % 8, LANES + lane] = sign
        if l < 32:
            e[cos_i + l // IDX_HEADS, 2 * LANES + lane] = 1.0
            e[sin_i + l // IDX_HEADS, 3 * LANES + lane] = sign
        elif l < 40:
            e[cos_a + l - 32, 2 * LANES + lane] = 1.0
            e[sin_a + l - 32, 3 * LANES + lane] = sign
        elif l < 44:
            e[cos_i + l - 40, 2 * LANES + lane] = 1.0
            e[sin_i + l - 40, 3 * LANES + lane] = sign
        else:
            e[one, 2 * LANES + lane] = 1.0
    return e


def _inproj_kernel(x_ref, sc_ref, sh_ref, g_ref, w_ref, trig_ref, exp_ref,
                   gkv_ref, o_ref, kcat_ref, ik_ref, latt_ref, vbt_ref):
    x = x_ref[...]
    ms = jnp.mean(x * x, axis=-1, keepdims=True)
    h = x * lax.rsqrt(ms + EPS) * g_ref[...]
    h = h * (1.0 + sc_ref[0]) + sh_ref[0]
    p = jnp.dot(h.astype(bf16), w_ref[...], preferred_element_type=f32)
    tab = jnp.dot(trig_ref[...], exp_ref[...], preferred_element_type=f32,
                  precision=lax.Precision.HIGHEST)
    r1 = p[:, C_R1:C_R1 + LANES]
    o_ref[:, C_R1:C_R1 + LANES] = (r1 * tab[:, 0:LANES]
                                   + pltpu.roll(r1, 64, 1) * tab[:, LANES:2 * LANES])
    r2 = p[:, C_R2:C_R2 + LANES]
    r2 = r2 * tab[:, 2 * LANES:3 * LANES] + pltpu.roll(r2, 64, 1) * tab[:, 3 * LANES:]
    o_ref[:, C_R2:C_R2 + LANES] = r2
    lat = p[:, C_LAT:C_LAT + KV_RANK]
    lms = jnp.mean(lat * lat, axis=-1, keepdims=True)
    lat = lat * lax.rsqrt(lms + EPS) * gkv_ref[...]
    o_ref[:, C_LAT:C_LAT + KV_RANK] = lat
    o_ref[:, C_QN:] = p[:, C_QN:]

    lane = lax.broadcasted_iota(i32, r2.shape, 1)
    kr = _lanes_from(lane, [(8, r2, 32), (16, r2, 96)])
    kcat_ref[...] = jnp.concatenate([lat, kr], axis=1).astype(bf16)
    misc_hi = p[:, C_MISC + LANES:C_MISC + 2 * LANES]
    ik_ref[...] = _lanes_from(lane, [(4, r2, 40), (8, r2, 104), (32, misc_hi, 64)]).astype(bf16)
    latt_ref[...] = _with_ones_rows(lat.T.astype(bf16))
    vt = p[:, C_VB:C_VB + WIDTH_B].T.astype(bf16)
    vbt_ref[...] = _with_ones_rows(vt.reshape(N_HEADS_B, HEAD_DIM_B, vt.shape[1]))


def _inproj_call(x2, sc, sh, g, w, trig, gkv, seq, tm):
    n, d = x2.shape
    per_b = seq // tm
    bsz = n // seq
    tok = lambda i: (i, 0)
    bat = lambda i: (i // per_b, 0, 0)
    cst = lambda i: (0, 0)
    return pl.pallas_call(
        _inproj_kernel,
        grid=(n // tm,),
        in_specs=[pl.BlockSpec((tm, d), tok),
                  pl.BlockSpec((1, 1, d), bat),
                  pl.BlockSpec((1, 1, d), bat),
                  pl.BlockSpec((1, d), cst),
                  pl.BlockSpec((d, C_TOTAL), cst),
                  pl.BlockSpec((tm, TRIG), tok),
                  pl.BlockSpec((TRIG, 4 * LANES), cst),
                  pl.BlockSpec((1, KV_RANK), cst)],
        out_specs=[pl.BlockSpec((tm, C_TOTAL), tok),
                   pl.BlockSpec((tm, KCAT), tok),
                   pl.BlockSpec((tm, LANES), tok),
                   pl.BlockSpec((None, None, KV_RANK + ONES_ROWS, tm),
                                lambda i: (i // per_b, i % per_b, 0, 0)),
                   pl.BlockSpec((None, N_HEADS_B, None, HEAD_DIM_B + ONES_ROWS, tm),
                                lambda i: (i // per_b, 0, i % per_b, 0, 0))],
        out_shape=[jax.ShapeDtypeStruct((n, C_TOTAL), f32),
                   jax.ShapeDtypeStruct((n, KCAT), bf16),
                   jax.ShapeDtypeStruct((n, LANES), bf16),
                   jax.ShapeDtypeStruct((bsz, per_b, KV_RANK + ONES_ROWS, tm), bf16),
                   jax.ShapeDtypeStruct((bsz, N_HEADS_B, per_b, HEAD_DIM_B + ONES_ROWS, tm), bf16)],
        compiler_params=_params(("parallel",)),
        name="inproj",
    )(x2, sc, sh, g, w, trig, jnp.asarray(_trig_expand()), gkv)


LOG2E = 1.4426950408889634


def _sortable(x):
    bits = lax.bitcast_convert_type(x, i32)
    return bits ^ ((bits >> 31) & 0x7FFFFFFF)


def _dsa_kernel(r1_ref, r2_ref, qn_ref, misc_ref, kcat_ref, latt_ref, ik_ref, wuk_ref, wuvt_ref,
                o_ref, key_scr, iq_scr, m_scr, acc_scr, *, tq, tk, topk, nbits, seq):
    nh = N_HEADS_A
    qi = pl.program_id(1)
    nc = ((qi + 1) * tq + tk - 1) // tk
    qpos = qi * tq + lax.broadcasted_iota(i32, (tk, tq), 1)
    krow = lax.broadcasted_iota(i32, (tk, tq), 0)

    r2t = r2_ref[...].T
    mt = misc_ref[...].T
    iq_scr[...] = jnp.concatenate([r2t[0:32], r2t[64:96], mt[0:192]], axis=0)
    iqt = jnp.concatenate([iq_scr[pl.ds(h, IDX_DIM, stride=IDX_HEADS), :] for h in range(IDX_HEADS)],
                          axis=1)
    iqt = jnp.concatenate([iqt, jnp.zeros((LANES - IDX_DIM, IDX_HEADS * tq), f32)], axis=0).astype(bf16)
    iwt = mt[216:224] * (IDX_DIM ** -0.5 * IDX_HEADS ** -0.5)

    def score_chunk(c, carry):
        start = pl.multiple_of(c * tk, tk)
        d = jnp.dot(ik_ref[pl.ds(start, tk), :], iqt, preferred_element_type=f32)
        sc = jnp.zeros((tk, tq), f32)
        for h in range(IDX_HEADS):
            sc = sc + jnp.maximum(d[:, h * tq:(h + 1) * tq], 0.0) * iwt[h:h + 1, :]
        sc = jnp.where(start + krow <= qpos, sc, -jnp.inf)
        key_scr[c] = _sortable(sc)
        return carry

    _loop_pairs(0, nc, score_chunk, 0)

    def count(pred):
        def body(c, acc):
            m = pred(key_scr[c], c * tk + krow)
            return acc + jnp.sum(m.reshape(tk // 8, 8, tq), axis=0)
        acc = lax.fori_loop(0, nc, body, jnp.zeros((8, tq), i32))
        return jnp.sum(acc, axis=0, keepdims=True)

    def bit_body(i, thr):
        cand = thr + lax.shift_left(jnp.int32(1), 31 - i)
        cnt = count(lambda k, col: jnp.where(k >= cand, 1, 0))
        return jnp.where(cnt >= topk, cand, thr)

    thr = lax.fori_loop(0, 32, bit_body, jnp.full((1, tq), INT_MIN, i32))
    need = topk - count(lambda k, col: jnp.where(k > thr, 1, 0))
    n_eq = count(lambda k, col: jnp.where(k == thr, 1, 0))

    def tie_search():
        def tie_body(i, last):
            cand = last + lax.shift_left(jnp.int32(1), nbits - 1 - i)
            below = count(lambda k, col: jnp.where(k == thr, jnp.where(col < cand, 1, 0), 0))
            return jnp.where(below < need, cand, last)
        return lax.fori_loop(0, nbits, tie_body, jnp.zeros((1, tq), i32))

    ambiguous = jnp.max(jnp.where(n_eq > need, 1, 0)) > 0
    last = lax.cond(ambiguous, tie_search, lambda: jnp.full((1, tq), seq, i32))

    scale = HEAD_DIM_A ** -0.5 * LOG2E
    r1t = r1_ref[...].T
    qnt = qn_ref[...].T.astype(bf16)
    pad = jnp.zeros((KCAT - KV_RANK - ROPE_DIM_A, tq), f32)
    cols = []
    for h in range(nh):
        ql = jnp.dot(wuk_ref[h], qnt[h * NOPE_DIM_A:(h + 1) * NOPE_DIM_A], preferred_element_type=f32)
        cols.append(jnp.concatenate([ql, r1t[h * 8:(h + 1) * 8], r1t[64 + h * 8:64 + (h + 1) * 8], pad],
                                    axis=0))
    qcat = (jnp.concatenate(cols, axis=1) * scale).astype(bf16)

    m_scr[...] = jnp.full(m_scr.shape, NEG_BIG, f32)
    acc_scr[...] = jnp.zeros(acc_scr.shape, f32)

    def attend_chunk(c, carry):
        start = pl.multiple_of(c * tk, tk)
        s = jnp.dot(kcat_ref[pl.ds(start, tk), :], qcat, preferred_element_type=f32)
        k = key_scr[c]
        col = start + krow
        tie = jnp.where(k == thr, jnp.where(col <= last, 0.0, NEG_BIG), NEG_BIG)
        bias = jnp.where(col <= qpos, jnp.where(k > thr, 0.0, tie), NEG_BIG)
        s = s + jnp.concatenate([bias] * nh, axis=1)
        m_old = m_scr[...]
        m_new = jnp.maximum(m_old, jnp.max(s, axis=0, keepdims=True))
        p = jnp.exp2(s - m_new)
        alpha = jnp.exp2(m_old - m_new)
        acc_scr[...] = alpha * acc_scr[...] + jnp.dot(latt_ref[c], p.astype(bf16),
                                                      preferred_element_type=f32)
        m_scr[...] = m_new
        return carry

    _loop_pairs(0, nc, attend_chunk, 0)

    ot = (acc_scr[0:KV_RANK, :] / acc_scr[KV_RANK:KV_RANK + 1, :]).astype(bf16)
    outs = [jnp.dot(wuvt_ref[h], ot[:, h * tq:(h + 1) * tq], preferred_element_type=f32)
            for h in range(nh)]
    o_ref[...] = jnp.concatenate(outs, axis=0).T


def _dsa_call(proj, kcat, latt, ik, wuk, wuvt, bsz, seq, tq, tk):
    nh = N_HEADS_A
    topk = min(TOPK_MAX, seq // 4)
    nbits = max(1, (seq - 1).bit_length())
    nq = seq // tq
    kern = functools.partial(_dsa_kernel, tq=tq, tk=tk, topk=topk, nbits=nbits, seq=seq)
    qcol = lambda width, blk: pl.BlockSpec((tq, width), lambda b, i: (b * nq + i, blk))
    return pl.pallas_call(
        kern,
        grid=(bsz, nq),
        in_specs=[qcol(LANES, C_R1 // LANES), qcol(LANES, C_R2 // LANES),
                  qcol(C_MISC - C_QN, C_QN // (C_MISC - C_QN)), qcol(256, C_MISC // 256),
                  pl.BlockSpec((None, seq, KCAT), lambda b, i: (b, 0, 0)),
                  pl.BlockSpec((None, seq // tk, KV_RANK + ONES_ROWS, tk), lambda b, i: (b, 0, 0, 0)),
                  pl.BlockSpec((None, seq, LANES), lambda b, i: (b, 0, 0)),
                  pl.BlockSpec(wuk.shape, lambda b, i: (0, 0, 0)),
                  pl.BlockSpec(wuvt.shape, lambda b, i: (0, 0, 0))],
        out_specs=pl.BlockSpec((tq, WIDTH_A), lambda b, i: (b * nq + i, 0)),
        out_shape=jax.ShapeDtypeStruct((bsz * seq, WIDTH_A), f32),
        scratch_shapes=[pltpu.VMEM((seq // tk, tk, tq), i32),
                        pltpu.VMEM((IDX_HEADS * IDX_DIM, tq), f32),
                        pltpu.VMEM((1, nh * tq), f32),
                        pltpu.VMEM((KV_RANK + ONES_ROWS, nh * tq), f32)],
        compiler_params=_params(("parallel", "arbitrary")),
        name="dsa",
    )(proj, proj, proj, proj, kcat, latt, ik, wuk, wuvt)


C_FB = C_MISC + 224
CUM_CHUNK = 512


def _cum_kernel(p_ref, b_ref, o_ref):
    seq = p_ref.shape[0]
    lane0 = C_FB % LANES
    x = p_ref[...].T[lane0:lane0 + N_HEADS_B] + b_ref[...]
    ls = jnp.minimum(x, 0.0) - jnp.log(1.0 + jnp.exp(-jnp.abs(x)))
    row = lax.broadcasted_iota(i32, (CUM_CHUNK, CUM_CHUNK), 0)
    col = lax.broadcasted_iota(i32, (CUM_CHUNK, CUM_CHUNK), 1)
    upper = jnp.where(row <= col, 1.0, 0.0)
    carry = jnp.zeros((N_HEADS_B, 1), f32)
    for c in range(seq // CUM_CHUNK):
        part = ls[:, c * CUM_CHUNK:(c + 1) * CUM_CHUNK]
        cs = jnp.dot(part, upper, preferred_element_type=f32, precision=lax.Precision.HIGHEST) + carry
        o_ref[:, c * CUM_CHUNK:(c + 1) * CUM_CHUNK] = cs * LOG2E
        carry = cs[:, CUM_CHUNK - 1:CUM_CHUNK]


def _cum_call(proj, b_forget, bsz, seq):
    assert seq % CUM_CHUNK == 0
    return pl.pallas_call(
        _cum_kernel,
        grid=(bsz,),
        in_specs=[pl.BlockSpec((seq, LANES), lambda b: (b, C_FB // LANES)),
                  pl.BlockSpec((N_HEADS_B, 1), lambda b: (0, 0))],
        out_specs=pl.BlockSpec((None, N_HEADS_B, seq), lambda b: (b, 0, 0)),
        out_shape=jax.ShapeDtypeStruct((bsz, N_HEADS_B, seq), f32),
        compiler_params=_params(("parallel",)),
        name="cum",
    )(proj, b_forget.reshape(N_HEADS_B, 1))


def _fox_kernel(q_ref, k_ref, vt_ref, cq_ref, ck_ref, o_ref, s_scr, *, tq, ts):
    qi = pl.program_id(2)
    nc = ((qi + 1) * tq + ts - 1) // ts
    mine = lax.broadcasted_iota(i32, (LANES, tq), 0) // HEAD_DIM_B == pl.program_id(1) % 2
    qt = jnp.where(mine, q_ref[...].T * (HEAD_DIM_B ** -0.5 * LOG2E), 0.0).astype(bf16)
    cq = cq_ref[...]
    qpos = qi * tq + lax.broadcasted_iota(i32, (ts, tq), 1)
    krow = lax.broadcasted_iota(i32, (ts, tq), 0)

    def score(c, m, masked):
        start = pl.multiple_of(c * ts, ts)
        s = jnp.dot(k_ref[pl.ds(start, ts), :].astype(bf16), qt, preferred_element_type=f32)
        ck = jnp.broadcast_to(ck_ref[c], (LANES, ts)).T
        s = s + cq - jnp.concatenate([ck] * (tq // LANES), axis=1)
        if masked:
            s = jnp.where(start + krow <= qpos, s, NEG_BIG)
        s_scr[c] = s
        return jnp.maximum(m, jnp.max(s, axis=0, keepdims=True))

    nfull = (qi * tq) // ts
    m = _loop_pairs(0, nfull, lambda c, m: score(c, m, False), jnp.full((1, tq), NEG_BIG, f32))
    m = lax.fori_loop(nfull, nc, lambda c, m: score(c, m, True), m)

    def attend(c, acc):
        p = jnp.exp2(s_scr[c] - m)
        return acc + jnp.dot(vt_ref[c], p.astype(bf16), preferred_element_type=f32)

    acc = _loop_pairs(0, nc, attend, jnp.zeros((HEAD_DIM_B + ONES_ROWS, tq), f32))
    o_ref[...] = acc[0:HEAD_DIM_B] / acc[HEAD_DIM_B:HEAD_DIM_B + 1]


def _fox_call(proj, vt, cq, ck, tq, ts):
    bsz, nh, _, _, _ = vt.shape
    hd = HEAD_DIM_B
    seq = proj.shape[0] // bsz
    nq = seq // tq
    kern = functools.partial(_fox_kernel, tq=tq, ts=ts)
    return pl.pallas_call(
        kern,
        grid=(bsz, nh, nq),
        in_specs=[pl.BlockSpec((tq, LANES), lambda b, h, i: (b * nq + i, C_QB // LANES + h // 2)),
                  pl.BlockSpec((seq, LANES), lambda b, h, i: (b, C_KB // LANES + h // 2)),
                  pl.BlockSpec((None, None, seq // ts, hd + ONES_ROWS, ts), lambda b, h, i: (b, h, 0, 0, 0)),
                  pl.BlockSpec((None, None, 1, tq), lambda b, h, i: (b, h, 0, i)),
                  pl.BlockSpec((None, None, seq // ts, 1, ts), lambda b, h, i: (b, h, 0, 0, 0))],
        out_specs=pl.BlockSpec((None, None, hd, tq), lambda b, h, i: (b, h, 0, i)),
        out_shape=jax.ShapeDtypeStruct((bsz, nh, hd, seq), f32),
        scratch_shapes=[pltpu.VMEM((seq // ts, ts, tq), f32)],
        compiler_params=_params(("parallel", "parallel", "arbitrary")),
        name="fox",
    )(proj, proj, vt, cq, ck)


def _outproj_kernel(oa_ref, ob_ref, x_ref, g1_ref, ga_ref, gb_ref, wa_ref, wb_ref,
                    gf_ref, sc_ref, sh_ref, wq_ref, x1_ref, h2_ref, qp_ref):
    oa = oa_ref[...]
    ob = ob_ref[...].reshape(WIDTH_B, oa.shape[0]).T
    na = oa * lax.rsqrt(jnp.mean(oa * oa, axis=-1, keepdims=True) + EPS) * ga_ref[...]
    nb = ob * lax.rsqrt(jnp.mean(ob * ob, axis=-1, keepdims=True) + EPS) * gb_ref[...]
    res = (jnp.dot(na.astype(bf16), wa_ref[...], preferred_element_type=f32)
           + jnp.dot(nb.astype(bf16), wb_ref[...], preferred_element_type=f32))
    x1 = x_ref[...] + g1_ref[0] * res
    x1_ref[...] = x1
    h2 = x1 * lax.rsqrt(jnp.mean(x1 * x1, axis=-1, keepdims=True) + EPS) * gf_ref[...]
    h2 = h2 * (1.0 + sc_ref[0]) + sh_ref[0]
    h2_ref[...] = h2
    qp_ref[...] = jnp.dot(h2.astype(bf16), wq_ref[...], preferred_element_type=f32)


def _outproj_call(oa, ob, x2, g1, ga, gb, wa, wb, gf, sc, sh, wq, seq, tm):
    n, d = x2.shape
    per_b = seq // tm
    tok = lambda i: (i, 0)
    bat = lambda i: (i // per_b, 0, 0)
    cst = lambda i: (0, 0)
    nq = wq.shape[1]
    return pl.pallas_call(
        _outproj_kernel,
        grid=(n // tm,),
        in_specs=[pl.BlockSpec((tm, WIDTH_A), tok),
                  pl.BlockSpec((None, N_HEADS_B, HEAD_DIM_B, tm), lambda i: (i // per_b, 0, 0, i % per_b)),
                  pl.BlockSpec((tm, d), tok), pl.BlockSpec((1, 1, d), bat),
                  pl.BlockSpec((1, WIDTH_A), cst), pl.BlockSpec((1, WIDTH_B), cst),
                  pl.BlockSpec((WIDTH_A, d), cst), pl.BlockSpec((WIDTH_B, d), cst),
                  pl.BlockSpec((1, d), cst), pl.BlockSpec((1, 1, d), bat),
                  pl.BlockSpec((1, 1, d), bat), pl.BlockSpec((d, nq), cst)],
        out_specs=[pl.BlockSpec((tm, d), tok), pl.BlockSpec((tm, d), tok),
                   pl.BlockSpec((tm, nq), tok)],
        out_shape=[jax.ShapeDtypeStruct((n, d), f32), jax.ShapeDtypeStruct((n, d), f32),
                   jax.ShapeDtypeStruct((n, nq), f32)],
        compiler_params=_params(("parallel",)),
        name="outproj",
    )(oa, ob, x2, g1, ga, gb, wa, wb, gf, sc, sh, wq)


def _argmax_rows(x, iota):
    vals = [x[j:j + 8] for j in range(0, x.shape[0], 8)]
    idxs = [iota[j:j + 8] for j in range(0, x.shape[0], 8)]
    while len(vals) > 1:
        nv, ni = [], []
        for a in range(0, len(vals) - 1, 2):
            keep = vals[a] >= vals[a + 1]
            nv.append(jnp.where(keep, vals[a], vals[a + 1]))
            ni.append(jnp.where(keep, idxs[a], idxs[a + 1]))
        if len(vals) % 2:
            nv.append(vals[-1])
            ni.append(idxs[-1])
        vals, idxs = nv, ni
    m = jnp.max(vals[0], axis=0, keepdims=True)
    pos = jnp.min(jnp.where(vals[0] == m, idxs[0], x.shape[0]), axis=0, keepdims=True)
    return m, pos


def _topk_rows(x, kk):
    iota = lax.broadcasted_iota(i32, x.shape, 0)
    vals, idxs = [], []
    for _ in range(kk):
        m, pos = _argmax_rows(x, iota)
        vals.append(m)
        idxs.append(pos)
        x = jnp.where(iota == pos, -jnp.inf, x)
    return jnp.concatenate(vals, axis=0), jnp.concatenate(idxs, axis=0)


CAND_PAIRS = [(a, b) for a in range(PEER_TOPK) for b in range(PEER_TOPK) if (a + 1) * (b + 1) <= PEER_TOPK]
CAND_ROWS = -(-len(CAND_PAIRS) // 8) * 8


def _cand_select():
    sel = np.zeros((2, CAND_ROWS, PEER_TOPK), np.float32)
    for r, (a, b) in enumerate(CAND_PAIRS):
        sel[0, r, a] = 1.0
        sel[1, r, b] = 1.0
    return sel


def _route_kernel(qp_ref, k1_ref, k2_ref, sel_ref, dup_ref, rows_ref, g_ref, *, nl):
    half = PEER_KEY_DIM // 2
    hp = lax.Precision.HIGHEST
    pick = lambda j, v: jnp.dot(sel_ref[j], v, preferred_element_type=f32, precision=hp)
    rows = lax.broadcasted_iota(i32, (CAND_ROWS, LANES), 0)

    def head(q):
        s1 = lax.dot_general(k1_ref[...], q[:, :half], NT_DIMS, preferred_element_type=f32, precision=hp)
        s2 = lax.dot_general(k2_ref[...], q[:, half:], NT_DIMS, preferred_element_type=f32, precision=hp)
        v1, i1 = _topk_rows(s1, PEER_TOPK)
        v2, i2 = _topk_rows(s2, PEER_TOPK)
        cand = jnp.where(rows < len(CAND_PAIRS), pick(0, v1) + pick(1, v2), -jnp.inf)
        cand_e = pick(0, i1.astype(f32)) * N_KEYS + pick(1, i2.astype(f32))
        best, experts = [], []
        for _ in range(PEER_TOPK):
            m, pos = _argmax_rows(cand, rows)
            hit = rows == pos
            experts.append(jnp.sum(jnp.where(hit, cand_e, 0.0), axis=0, keepdims=True))
            best.append(m)
            cand = jnp.where(hit, -jnp.inf, cand)
        best = jnp.concatenate(best, axis=0)
        e = jnp.exp(best - best[0:1])
        return jnp.concatenate(experts, axis=0), e / jnp.sum(e, axis=0, keepdims=True)

    def lane_tile(l, carry):
        start = pl.multiple_of(l * LANES, LANES)
        outs = [head(qp_ref[pl.ds(start, LANES), h * PEER_KEY_DIM:(h + 1) * PEER_KEY_DIM])
                for h in range(PEER_HEADS)]
        ids = jnp.concatenate([o[0] for o in outs], axis=0)
        gates = jnp.concatenate([o[1] for o in outs], axis=0)
        rows_ref[pl.ds(start, LANES), :] = (ids.T * ROWS_PER_EXPERT).astype(i32)
        g_ref[pl.ds(start, LANES), :] = jnp.dot(dup_ref[...], gates, preferred_element_type=f32,
                                                precision=hp).T
        return carry

    lax.fori_loop(0, nl, lane_tile, 0)


def _route_call(qp, k1, k2, tr):
    n = qp.shape[0]
    sel = jnp.asarray(_cand_select())
    dup = jnp.asarray(np.repeat(np.eye(PEER_SLOTS, dtype=np.float32), 2, axis=0))
    kern = functools.partial(_route_kernel, nl=tr // LANES)
    cst = lambda i: (0, 0)
    return pl.pallas_call(
        kern,
        grid=(n // tr,),
        in_specs=[pl.BlockSpec((tr, PEER_HEADS * PEER_KEY_DIM), lambda i: (i, 0)),
                  pl.BlockSpec(k1.shape, cst),
                  pl.BlockSpec(k2.shape, cst),
                  pl.BlockSpec(sel.shape, lambda i: (0, 0, 0)),
                  pl.BlockSpec(dup.shape, cst)],
        out_specs=[pl.BlockSpec((tr, PEER_SLOTS), lambda i: (i, 0)),
                   pl.BlockSpec((tr, 2 * PEER_SLOTS), lambda i: (i, 0))],
        out_shape=[jax.ShapeDtypeStruct((n, PEER_SLOTS), i32),
                   jax.ShapeDtypeStruct((n, 2 * PEER_SLOTS), f32)],
        compiler_params=_params(("parallel",)),
        name="route",
    )(qp, k1, k2, sel, dup)


ROWS_PER_EXPERT = 4
TILE_STRIDE = 136
TOK_UNROLL = 64
HALF = 512


def _pack_kernel(t_ref, o_ref):
    te = t_ref.shape[0]
    bits = lax.bitcast_convert_type(t_ref[...].astype(bf16).astype(f32), i32)
    word = (bits[:, HALF:] & jnp.int32(-65536)) | lax.shift_right_logical(bits[:, :HALF], 16)
    for r in range(ROWS_PER_EXPERT):
        o_ref[pl.ds(r, te, stride=ROWS_PER_EXPERT), :] = word[:, r * LANES:(r + 1) * LANES]


def _pack_table(tab, te=256):
    e, d = tab.shape
    return pl.pallas_call(
        _pack_kernel,
        grid=(e // te,),
        in_specs=[pl.BlockSpec((te, d), lambda i: (i, 0))],
        out_specs=pl.BlockSpec((te * ROWS_PER_EXPERT, LANES), lambda i: (i, 0)),
        out_shape=jax.ShapeDtypeStruct((e * ROWS_PER_EXPERT, LANES), i32),
        compiler_params=_params(("parallel",)),
        name="pack",
    )(tab)


def _gather_tile(tab_ref, rows_ref, t, tile_ref):
    tok_rows = rows_ref.at[t]
    for k in range(PEER_SLOTS):
        row = pl.multiple_of(tok_rows[k], ROWS_PER_EXPERT)
        tile_ref[pl.ds(k, ROWS_PER_EXPERT, stride=TILE_STRIDE), :] = tab_ref[pl.ds(row, ROWS_PER_EXPERT), :]
    chunks = [pltpu.bitcast(tile_ref[j * TILE_STRIDE:j * TILE_STRIDE + PEER_SLOTS, :], bf16)
              for j in range(ROWS_PER_EXPERT)]
    return jnp.concatenate(chunks, axis=1)


def _for_tokens(tab_ref, rows_ref, tile_scr, base, per_token):
    for tl in range(TOK_UNROLL):
        per_token(tl, base + tl, _gather_tile(tab_ref, rows_ref, base + tl, tile_scr.at[tl % 2]))


def _peer_u_kernel(ids_ref, h_ref, g_ref, u_ref, o_ref, tile_scr, act_scr, *, tt):
    even = lax.broadcasted_iota(i32, (1, 2 * PEER_SLOTS), 1) % 2 == 0

    def group(gi, carry):
        base = pl.multiple_of(gi * TOK_UNROLL, TOK_UNROLL)
        rows = []

        def per_token(tl, t, r):
            row = h_ref[pl.ds(t, 1), :]
            h = jnp.concatenate([row[:, :HALF], row[:, HALF:]], axis=0)
            major = h.astype(bf16)
            minor = (h - major.astype(f32)).astype(bf16)
            out = lax.dot_general(jnp.concatenate([major, minor], axis=0), r, NT_DIMS,
                                  preferred_element_type=f32)
            rows.append(jnp.where(even, out[0:1] + out[2:3], out[1:2] + out[3:4]))

        _for_tokens(u_ref, ids_ref, tile_scr, base, per_token)
        act_scr[pl.ds(base, TOK_UNROLL), :] = jnp.concatenate(rows, axis=0)
        return carry

    lax.fori_loop(0, tt // TOK_UNROLL, group, 0)
    part = act_scr[...]
    act = part + jnp.where(even, pltpu.roll(part, 2 * PEER_SLOTS - 1, 1), pltpu.roll(part, 1, 1))
    gelu = 0.5 * act * (1.0 + lax.erf(act * (2.0 ** -0.5)))
    o_ref[...] = g_ref[...] * gelu


def _peer_u_call(ids, h3, g2, u_packed, tt):
    n = ids.shape[0]
    kern = functools.partial(_peer_u_kernel, tt=tt)
    return pl.pallas_call(
        kern,
        grid=(n // tt,),
        in_specs=[pl.BlockSpec((tt, PEER_SLOTS), lambda i: (i, 0), memory_space=pltpu.SMEM),
                  pl.BlockSpec((tt, 2 * HALF), lambda i: (i, 0)),
                  pl.BlockSpec((tt, 2 * PEER_SLOTS), lambda i: (i, 0)),
                  pl.BlockSpec(u_packed.shape, lambda i: (0, 0), pipeline_mode=pl.Buffered(1))],
        out_specs=pl.BlockSpec((tt, 2 * PEER_SLOTS), lambda i: (i, 0)),
        out_shape=jax.ShapeDtypeStruct((n, 2 * PEER_SLOTS), f32),
        scratch_shapes=[pltpu.VMEM((2, ROWS_PER_EXPERT * TILE_STRIDE, LANES), i32),
                        pltpu.VMEM((tt, 2 * PEER_SLOTS), f32)],
        compiler_params=_params(("arbitrary",)),
        name="peer_u",
    )(ids, h3, g2, u_packed)


def _peer_v_kernel(ids_ref, w_ref, v_ref, o_ref, tile_scr, *, tt):
    even = lax.broadcasted_iota(i32, (1, 2 * PEER_SLOTS), 1) % 2 == 0

    def group(gi, carry):
        base = pl.multiple_of(gi * TOK_UNROLL, TOK_UNROLL)
        wts = w_ref[pl.ds(base, TOK_UNROLL), :]
        rows = []

        def per_token(tl, t, r):
            w = wts[tl:tl + 1]
            lhs = jnp.concatenate([jnp.where(even, w, 0.0), jnp.where(even, 0.0, w)], axis=0)
            out = jnp.dot(lhs.astype(bf16), r, preferred_element_type=f32)
            rows.append(jnp.concatenate([out[0:1], out[1:2]], axis=1))
            if len(rows) == 8:
                o_ref[pl.ds(pl.multiple_of(base + tl - 7, 8), 8), :] = jnp.concatenate(rows, axis=0)
                rows.clear()

        _for_tokens(v_ref, ids_ref, tile_scr, base, per_token)
        return carry

    lax.fori_loop(0, tt // TOK_UNROLL, group, 0)


def _peer_v_call(ids, wgt2, v_packed, tt):
    n = ids.shape[0]
    kern = functools.partial(_peer_v_kernel, tt=tt)
    return pl.pallas_call(
        kern,
        grid=(n // tt,),
        in_specs=[pl.BlockSpec((tt, PEER_SLOTS), lambda i: (i, 0), memory_space=pltpu.SMEM),
                  pl.BlockSpec((tt, 2 * PEER_SLOTS), lambda i: (i, 0)),
                  pl.BlockSpec(v_packed.shape, lambda i: (0, 0), pipeline_mode=pl.Buffered(1))],
        out_specs=pl.BlockSpec((tt, 2 * HALF), lambda i: (i, 0)),
        out_shape=jax.ShapeDtypeStruct((n, 2 * HALF), f32),
        scratch_shapes=[pltpu.VMEM((2, ROWS_PER_EXPERT * TILE_STRIDE, LANES), i32)],
        compiler_params=_params(("arbitrary",)),
        name="peer_v",
    )(ids, wgt2, v_packed)


def _final_kernel(x1_ref, p_ref, g2_ref, gf_ref, o_ref):
    x2 = x1_ref[...] + g2_ref[0] * p_ref[...]
    o_ref[...] = x2 * lax.rsqrt(jnp.mean(x2 * x2, axis=-1, keepdims=True) + EPS) * gf_ref[...]


def _final_call(x1, peer, g2, gf, seq, tm):
    n, d = x1.shape
    per_b = seq // tm
    tok = lambda i: (i, 0)
    return pl.pallas_call(
        _final_kernel,
        grid=(n // tm,),
        in_specs=[pl.BlockSpec((tm, d), tok), pl.BlockSpec((tm, d), tok),
                  pl.BlockSpec((1, 1, d), lambda i: (i // per_b, 0, 0)),
                  pl.BlockSpec((1, d), lambda i: (0, 0))],
        out_specs=pl.BlockSpec((tm, d), tok),
        out_shape=jax.ShapeDtypeStruct((n, d), f32),
        compiler_params=_params(("parallel",)),
        name="final",
    )(x1, peer, g2, gf)


def _rope_tables(positions):
    def tab(rot):
        inv = ROPE_THETA ** (-jnp.arange(0, rot, 2, dtype=f32) / rot)
        ang = positions.astype(f32)[..., None] * inv
        return jnp.cos(ang), jnp.sin(ang)

    cos_a, sin_a = tab(ROPE_DIM_A)
    cos_i, sin_i = tab(IDX_ROPE_DIM)
    b, s = positions.shape
    one = jnp.ones((b, s, 1), f32)
    zero = jnp.zeros((b, s, 4), f32)
    trig = jnp.concatenate([cos_a, cos_i, one, zero[..., :3], sin_a, sin_i, zero], axis=-1)
    return trig.reshape(b * s, TRIG)


def kernel(x, c, positions, w_ada, b_ada, g_mix, w_in, g_kv, w_uk, w_uv, b_forget,
           g_out_a, g_out_b, w_out, g_ffn, w_peer_q, peer_keys1, peer_keys2, peer_u,
           peer_v, g_final):
    b, s, d = x.shape
    n = b * s
    assert w_ada.shape[0] == 1, "single layer supported"
    tm = min(512, s)
    x2 = x.reshape(n, d)

    mod = _mod_call(c, w_ada[0], b_ada[0])
    mod = mod.reshape(b, 6, 1, d)
    shift1, scale1, gate1, shift2, scale2, gate2 = [mod[:, j] for j in range(6)]

    perm = _in_perm()
    w_in_r = jnp.where((perm >= 0)[None, :], w_in[0][:, np.maximum(perm, 0)], 0.0).astype(bf16)
    proj, kcat, ik, latt, vbt = _inproj_call(x2, scale1, shift1, g_mix[0].reshape(1, d), w_in_r,
                                             _rope_tables(positions), g_kv[0].reshape(1, KV_RANK), s, tm)

    wuk = w_uk[0].transpose(1, 0, 2).astype(bf16)
    wuvt = w_uv[0].transpose(1, 2, 0).astype(bf16)
    o_a = _dsa_call(proj, kcat.reshape(b, s, KCAT), latt, ik.reshape(b, s, LANES), wuk, wuvt,
                    b, s, tq=128, tk=tm)

    cum = _cum_call(proj, b_forget[0], b, s)
    o_bt = _fox_call(proj, vbt, cum[:, :, None, :], cum.reshape(b, N_HEADS_B, s // tm, 1, tm),
                     tq=min(512, s), ts=tm)

    wo = w_out[0].astype(bf16)
    x1, h2, qp = _outproj_call(o_a, o_bt, x2, gate1, g_out_a[0].reshape(1, WIDTH_A),
                               g_out_b[0].reshape(1, WIDTH_B), wo[:WIDTH_A], wo[WIDTH_A:],
                               g_ffn[0].reshape(1, d), scale2, shift2,
                               w_peer_q[0].astype(bf16), s, tm)

    ids, gates2 = _route_call(qp, peer_keys1[0], peer_keys2[0], tr=min(1024, n))
    tt = 64
    wgt2 = _peer_u_call(ids, h2, gates2, _pack_table(peer_u[0]), tt)
    peer = _peer_v_call(ids, wgt2, _pack_table(peer_v[0]), tt)

    out = _final_call(x1, peer, gate2, g_final.reshape(1, d), s, tm)
    return out.reshape(b, s, d)
```

```python
import functools

import jax
import jax.numpy as jnp
import numpy as np
from jax import lax
from jax.experimental import pallas as pl
from jax.experimental.pallas import tpu as pltpu

f32 = jnp.float32
bf16 = jnp.bfloat16
i32 = jnp.int32

N_HEADS_A = 8
HEAD_DIM_A = 64
ROPE_DIM_A = 16
NOPE_DIM_A = HEAD_DIM_A - ROPE_DIM_A
KV_RANK = 128
IDX_HEADS = 8
IDX_DIM = 32
IDX_ROPE_DIM = 8
TOPK_MAX = 256
N_HEADS_B = 8
HEAD_DIM_B = 64
WIDTH_A = N_HEADS_A * HEAD_DIM_A
WIDTH_B = N_HEADS_B * HEAD_DIM_B
ROPE_THETA = 500000.0
PEER_HEADS = 8
N_KEYS = 128
PEER_KEY_DIM = 128
PEER_TOPK = 16
PEER_SLOTS = PEER_HEADS * PEER_TOPK
EPS = 1e-6
IN_SIZES = (WIDTH_A, KV_RANK, ROPE_DIM_A, IDX_HEADS * IDX_DIM, IDX_DIM, IDX_HEADS,
            WIDTH_B, WIDTH_B, WIDTH_B, N_HEADS_B)

LANES = 128
NEG_BIG = -1e30
INT_MIN = -2147483648
VMEM_LIMIT = 56 * 1024 * 1024

C_R1 = 0
C_R2 = 128
C_LAT = 256
C_QN = 384
C_MISC = 768
C_QB = 1024
C_KB = 1536
C_VB = 2048
C_TOTAL = 2560

NT_DIMS = (((1,), (1,)), ((), ()))


def _in_perm():
    offs = np.cumsum((0,) + IN_SIZES)
    o_qa, o_lat, o_kr, o_iq, o_ik, o_iw, o_qb, o_kb, o_vb, o_fb = offs[:10]
    perm = -np.ones((C_TOTAL,), np.int64)
    ha, hi = ROPE_DIM_A // 2, IDX_ROPE_DIM // 2
    for h in range(N_HEADS_A):
        for j in range(ha):
            perm[C_R1 + h * ha + j] = o_qa + h * HEAD_DIM_A + j
            perm[C_R1 + 64 + h * ha + j] = o_qa + h * HEAD_DIM_A + ha + j
        for j in range(NOPE_DIM_A):
            perm[C_QN + h * NOPE_DIM_A + j] = o_qa + h * HEAD_DIM_A + ROPE_DIM_A + j
    for h in range(IDX_HEADS):
        for j in range(hi):
            perm[C_R2 + j * IDX_HEADS + h] = o_iq + h * IDX_DIM + j
            perm[C_R2 + 64 + j * IDX_HEADS + h] = o_iq + h * IDX_DIM + hi + j
        for j in range(IDX_DIM - IDX_ROPE_DIM):
            perm[C_MISC + j * IDX_HEADS + h] = o_iq + h * IDX_DIM + IDX_ROPE_DIM + j
    for j in range(ha):
        perm[C_R2 + 32 + j] = o_kr + j
        perm[C_R2 + 64 + 32 + j] = o_kr + ha + j
    for j in range(hi):
        perm[C_R2 + 40 + j] = o_ik + j
        perm[C_R2 + 64 + 40 + j] = o_ik + hi + j
    for j in range(IDX_DIM - IDX_ROPE_DIM):
        perm[C_MISC + 192 + j] = o_ik + IDX_ROPE_DIM + j
    for j in range(IDX_HEADS):
        perm[C_MISC + 216 + j] = o_iw + j
    for j in range(N_HEADS_B):
        perm[C_MISC + 224 + j] = o_fb + j
    perm[C_LAT:C_LAT + KV_RANK] = o_lat + np.arange(KV_RANK)
    perm[C_QB:C_QB + WIDTH_B] = o_qb + np.arange(WIDTH_B)
    perm[C_KB:C_KB + WIDTH_B] = o_kb + np.arange(WIDTH_B)
    perm[C_VB:C_VB + WIDTH_B] = o_vb + np.arange(WIDTH_B)
    return perm


def _params(sem):
    return pltpu.CompilerParams(dimension_semantics=sem, vmem_limit_bytes=VMEM_LIMIT)


def _loop_pairs(lo, hi, body, init):
    pairs = (hi - lo) // 2

    def two(i, carry):
        return body(lo + 2 * i + 1, body(lo + 2 * i, carry))

    carry = lax.fori_loop(0, pairs, two, init)
    return lax.fori_loop(lo + 2 * pairs, hi, body, carry)


def _mod_kernel(c_ref, w_ref, b_ref, o_ref):
    c = c_ref[...]
    ca = c * jax.nn.sigmoid(c)
    o_ref[...] = jnp.dot(ca, w_ref[...], preferred_element_type=f32,
                         precision=lax.Precision.HIGHEST) + b_ref[...]


def _mod_call(c, w, b):
    bsz, d = c.shape
    n = w.shape[1]
    return pl.pallas_call(
        _mod_kernel,
        grid=(n // d,),
        in_specs=[pl.BlockSpec((bsz, d), lambda j: (0, 0)),
                  pl.BlockSpec((d, d), lambda j: (0, j)),
                  pl.BlockSpec((1, d), lambda j: (0, j))],
        out_specs=pl.BlockSpec((bsz, d), lambda j: (0, j)),
        out_shape=jax.ShapeDtypeStruct((bsz, n), f32),
        compiler_params=_params(("arbitrary",)),
        name="mod",
    )(c, w, b.reshape(1, n))


KCAT = 256
ONES_ROWS = 16


def _with_ones_rows(vt):
    lead = vt.shape[:-2]
    t = vt.shape[-1]
    ones = jnp.ones(lead + (1, t), vt.dtype)
    zeros = jnp.zeros(lead + (ONES_ROWS - 1, t), vt.dtype)
    return jnp.concatenate([vt, ones, zeros], axis=-2)


def _lanes_from(lane, pieces):
    out = jnp.zeros(lane.shape, f32)
    for end, src, start in reversed(pieces):
        begin = max([e for e, _, _ in pieces if e < end], default=0)
        out = jnp.where(lane < end, pltpu.roll(src, (begin - start) % LANES, 1), out)
    return out


TRIG = 32


def _trig_expand():
    e = np.zeros((TRIG, 4 * LANES), np.float32)
    cos_a, cos_i, one, sin_a, sin_i = 0, 8, 12, 16, 24
    for lane in range(LANES):
        l, sign = lane % 64, (-1.0 if lane < 64 else 1.0)
        e[cos_a + l % 8, lane] = 1.0
        e[sin_a + l % 8, LANES + lane] = sign
        if l < 32:
            e[cos_i + l // IDX_HEADS, 2 * LANES + lane] = 1.0
            e[sin_i + l // IDX_HEADS, 3 * LANES + lane] = sign
        elif l < 40:
            e[cos_a + l - 32, 2 * LANES + lane] = 1.0
            e[sin_a + l - 32, 3 * LANES + lane] = sign
        elif l < 44:
            e[cos_i + l - 40, 2 * LANES + lane] = 1.0
            e[sin_i + l - 40, 3 * LANES + lane] = sign
        else:
            e[one, 2 * LANES + lane] = 1.0
    return e


def _inproj_kernel(x_ref, sc_ref, sh_ref, g_ref, w_ref, trig_ref, exp_ref,
                   gkv_ref, o_ref, kcat_ref, ik_ref, latt_ref, vbt_ref):
    x = x_ref[...]
    ms = jnp.mean(x * x, axis=-1, keepdims=True)
    h = x * lax.rsqrt(ms + EPS) * g_ref[...]
    h = h * (1.0 + sc_ref[0]) + sh_ref[0]
    p = jnp.dot(h.astype(bf16), w_ref[...], preferred_element_type=f32)
    tab = jnp.dot(trig_ref[...], exp_ref[...], preferred_element_type=f32,
                  precision=lax.Precision.HIGHEST)
    r1 = p[:, C_R1:C_R1 + LANES]
    o_ref[:, C_R1:C_R1 + LANES] = (r1 * tab[:, 0:LANES]
                                   + pltpu.roll(r1, 64, 1) * tab[:, LANES:2 * LANES])
    r2 = p[:, C_R2:C_R2 + LANES]
    r2 = r2 * tab[:, 2 * LANES:3 * LANES] + pltpu.roll(r2, 64, 1) * tab[:, 3 * LANES:]
    o_ref[:, C_R2:C_R2 + LANES] = r2
    lat = p[:, C_LAT:C_LAT + KV_RANK]
    lms = jnp.mean(lat * lat, axis=-1, keepdims=True)
    lat = lat * lax.rsqrt(lms + EPS) * gkv_ref[...]
    o_ref[:, C_LAT:C_LAT + KV_RANK] = lat
    o_ref[:, C_QN:] = p[:, C_QN:]

    lane = lax.broadcasted_iota(i32, r2.shape, 1)
    kr = _lanes_from(lane, [(8, r2, 32), (16, r2, 96)])
    kcat_ref[...] = jnp.concatenate([lat, kr], axis=1).astype(bf16)
    misc_hi = p[:, C_MISC + LANES:C_MISC + 2 * LANES]
    ik_ref[...] = _lanes_from(lane, [(4, r2, 40), (8, r2, 104), (32, misc_hi, 64)]).astype(bf16)
    latt_ref[...] = _with_ones_rows(lat.T.astype(bf16))
    vt = p[:, C_VB:C_VB + WIDTH_B].T.astype(bf16)
    vbt_ref[...] = _with_ones_rows(vt.reshape(N_HEADS_B, HEAD_DIM_B, vt.shape[1]))


def _inproj_call(x2, sc, sh, g, w, trig, gkv, seq, tm):
    n, d = x2.shape
    per_b = seq // tm
    bsz = n // seq
    tok = lambda i: (i, 0)
    bat = lambda i: (i // per_b, 0, 0)
    cst = lambda i: (0, 0)
    return pl.pallas_call(
        _inproj_kernel,
        grid=(n // tm,),
        in_specs=[pl.BlockSpec((tm, d), tok),
                  pl.BlockSpec((1, 1, d), bat),
                  pl.BlockSpec((1, 1, d), bat),
                  pl.BlockSpec((1, d), cst),
                  pl.BlockSpec((d, C_TOTAL), cst),
                  pl.BlockSpec((tm, TRIG), tok),
                  pl.BlockSpec((TRIG, 4 * LANES), cst),
                  pl.BlockSpec((1, KV_RANK), cst)],
        out_specs=[pl.BlockSpec((tm, C_TOTAL), tok),
                   pl.BlockSpec((tm, KCAT), tok),
                   pl.BlockSpec((tm, LANES), tok),
                   pl.BlockSpec((None, None, KV_RANK + ONES_ROWS, tm),
                                lambda i: (i // per_b, i % per_b, 0, 0)),
                   pl.BlockSpec((None, N_HEADS_B, None, HEAD_DIM_B + ONES_ROWS, tm),
                                lambda i: (i // per_b, 0, i % per_b, 0, 0))],
        out_shape=[jax.ShapeDtypeStruct((n, C_TOTAL), f32),
                   jax.ShapeDtypeStruct((n, KCAT), bf16),
                   jax.ShapeDtypeStruct((n, LANES), bf16),
                   jax.ShapeDtypeStruct((bsz, per_b, KV_RANK + ONES_ROWS, tm), bf16),
                   jax.ShapeDtypeStruct((bsz, N_HEADS_B, per_b, HEAD_DIM_B + ONES_ROWS, tm), bf16)],
        compiler_params=_params(("parallel",)),
        name="inproj",
    )(x2, sc, sh, g, w, trig, jnp.asarray(_trig_expand()), gkv)


LOG2E = 1.4426950408889634


def _sortable(x):
    bits = lax.bitcast_convert_type(x, i32)
    return bits ^ ((bits >> 31) & 0x7FFFFFFF)


def _dsa_kernel(r1_ref, r2_ref, qn_ref, misc_ref, kcat_ref, latt_ref, ik_ref, wuk_ref, wuvt_ref,
                o_ref, key_scr, iq_scr, m_scr, acc_scr, *, tq, tk, topk, nbits, seq):
    nh = N_HEADS_A
    qi = pl.program_id(1)
    nc = ((qi + 1) * tq + tk - 1) // tk
    qpos = qi * tq + lax.broadcasted_iota(i32, (tk, tq), 1)
    krow = lax.broadcasted_iota(i32, (tk, tq), 0)

    r2t = r2_ref[...].T
    mt = misc_ref[...].T
    iq_scr[...] = jnp.concatenate([r2t[0:32], r2t[64:96], mt[0:192]], axis=0)
    iqt = jnp.concatenate([iq_scr[pl.ds(h, IDX_DIM, stride=IDX_HEADS), :] for h in range(IDX_HEADS)],
                          axis=1)
    iqt = jnp.concatenate([iqt, jnp.zeros((LANES - IDX_DIM, IDX_HEADS * tq), f32)], axis=0).astype(bf16)
    iwt = mt[216:224] * (IDX_DIM ** -0.5 * IDX_HEADS ** -0.5)

    def score_chunk(c, carry):
        start = pl.multiple_of(c * tk, tk)
        d = jnp.dot(ik_ref[pl.ds(start, tk), :], iqt, preferred_element_type=f32)
        sc = jnp.zeros((tk, tq), f32)
        for h in range(IDX_HEADS):
            sc = sc + jnp.maximum(d[:, h * tq:(h + 1) * tq], 0.0) * iwt[h:h + 1, :]
        sc = jnp.where(start + krow <= qpos, sc, -jnp.inf)
        key_scr[c] = _sortable(sc)
        return carry

    _loop_pairs(0, nc, score_chunk, 0)

    def count(pred):
        def body(c, acc):
            m = pred(key_scr[c], c * tk + krow)
            return acc + jnp.sum(m.reshape(tk // 8, 8, tq), axis=0)
        acc = _loop_pairs(0, nc, body, jnp.zeros((8, tq), i32))
        return jnp.sum(acc, axis=0, keepdims=True)

    def bit_body(i, thr):
        cand = thr + lax.shift_left(jnp.int32(1), 31 - i)
        cnt = count(lambda k, col: jnp.where(k >= cand, 1, 0))
        return jnp.where(cnt >= topk, cand, thr)

    thr = lax.fori_loop(0, 32, bit_body, jnp.full((1, tq), INT_MIN, i32))
    need = topk - count(lambda k, col: jnp.where(k > thr, 1, 0))
    n_eq = count(lambda k, col: jnp.where(k == thr, 1, 0))

    def tie_search():
        def tie_body(i, last):
            cand = last + lax.shift_left(jnp.int32(1), nbits - 1 - i)
            below = count(lambda k, col: jnp.where(k == thr, jnp.where(col < cand, 1, 0), 0))
            return jnp.where(below < need, cand, last)
        return lax.fori_loop(0, nbits, tie_body, jnp.zeros((1, tq), i32))

    ambiguous = jnp.max(jnp.where(n_eq > need, 1, 0)) > 0
    last = lax.cond(ambiguous, tie_search, lambda: jnp.full((1, tq), seq, i32))

    scale = HEAD_DIM_A ** -0.5 * LOG2E
    r1t = r1_ref[...].T
    qnt = qn_ref[...].T.astype(bf16)
    pad = jnp.zeros((KCAT - KV_RANK - ROPE_DIM_A, tq), f32)
    cols = []
    for h in range(nh):
        ql = jnp.dot(wuk_ref[h], qnt[h * NOPE_DIM_A:(h + 1) * NOPE_DIM_A], preferred_element_type=f32)
        cols.append(jnp.concatenate([ql, r1t[h * 8:(h + 1) * 8], r1t[64 + h * 8:64 + (h + 1) * 8], pad],
                                    axis=0))
    qcat = (jnp.concatenate(cols, axis=1) * scale).astype(bf16)

    m_scr[...] = jnp.full(m_scr.shape, NEG_BIG, f32)
    acc_scr[...] = jnp.zeros(acc_scr.shape, f32)

    def attend_chunk(c, carry):
        start = pl.multiple_of(c * tk, tk)
        s = jnp.dot(kcat_ref[pl.ds(start, tk), :], qcat, preferred_element_type=f32)
        k = key_scr[c]
        col = start + krow
        tie = jnp.where(k == thr, jnp.where(col <= last, 0.0, NEG_BIG), NEG_BIG)
        bias = jnp.where(col <= qpos, jnp.where(k > thr, 0.0, tie), NEG_BIG)
        s = s + jnp.concatenate([bias] * nh, axis=1)
        m_old = m_scr[...]
        m_new = jnp.maximum(m_old, jnp.max(s, axis=0, keepdims=True))
        p = jnp.exp2(s - m_new)
        alpha = jnp.exp2(m_old - m_new)
        acc_scr[...] = alpha * acc_scr[...] + jnp.dot(latt_ref[c], p.astype(bf16),
                                                      preferred_element_type=f32)
        m_scr[...] = m_new
        return carry

    _loop_pairs(0, nc, attend_chunk, 0)

    ot = (acc_scr[0:KV_RANK, :] / acc_scr[KV_RANK:KV_RANK + 1, :]).astype(bf16)
    outs = [jnp.dot(wuvt_ref[h], ot[:, h * tq:(h + 1) * tq], preferred_element_type=f32)
            for h in range(nh)]
    o_ref[...] = jnp.concatenate(outs, axis=0).T


def _dsa_call(proj, kcat, latt, ik, wuk, wuvt, bsz, seq, tq, tk):
    nh = N_HEADS_A
    topk = min(TOPK_MAX, seq // 4)
    nbits = max(1, (seq - 1).bit_length())
    nq = seq // tq
    kern = functools.partial(_dsa_kernel, tq=tq, tk=tk, topk=topk, nbits=nbits, seq=seq)
    qcol = lambda width, blk: pl.BlockSpec((tq, width), lambda b, i: (b * nq + i, blk))
    return pl.pallas_call(
        kern,
        grid=(bsz, nq),
        in_specs=[qcol(LANES, C_R1 // LANES), qcol(LANES, C_R2 // LANES),
                  qcol(C_MISC - C_QN, C_QN // (C_MISC - C_QN)), qcol(256, C_MISC // 256),
                  pl.BlockSpec((None, seq, KCAT), lambda b, i: (b, 0, 0)),
                  pl.BlockSpec((None, seq // tk, KV_RANK + ONES_ROWS, tk), lambda b, i: (b, 0, 0, 0)),
                  pl.BlockSpec((None, seq, LANES), lambda b, i: (b, 0, 0)),
                  pl.BlockSpec(wuk.shape, lambda b, i: (0, 0, 0)),
                  pl.BlockSpec(wuvt.shape, lambda b, i: (0, 0, 0))],
        out_specs=pl.BlockSpec((tq, WIDTH_A), lambda b, i: (b * nq + i, 0)),
        out_shape=jax.ShapeDtypeStruct((bsz * seq, WIDTH_A), f32),
        scratch_shapes=[pltpu.VMEM((seq // tk, tk, tq), i32),
                        pltpu.VMEM((IDX_HEADS * IDX_DIM, tq), f32),
                        pltpu.VMEM((1, nh * tq), f32),
                        pltpu.VMEM((KV_RANK + ONES_ROWS, nh * tq), f32)],
        compiler_params=_params(("parallel", "arbitrary")),
        name="dsa",
    )(proj, proj, proj, proj, kcat, latt, ik, wuk, wuvt)


C_FB = C_MISC + 224
CUM_CHUNK = 512


def _cum_kernel(p_ref, b_ref, o_ref):
    seq = p_ref.shape[0]
    lane0 = C_FB % LANES
    x = p_ref[...].T[lane0:lane0 + N_HEADS_B] + b_ref[...]
    ls = jnp.minimum(x, 0.0) - jnp.log(1.0 + jnp.exp(-jnp.abs(x)))
    row = lax.broadcasted_iota(i32, (CUM_CHUNK, CUM_CHUNK), 0)
    col = lax.broadcasted_iota(i32, (CUM_CHUNK, CUM_CHUNK), 1)
    upper = jnp.where(row <= col, 1.0, 0.0)
    carry = jnp.zeros((N_HEADS_B, 1), f32)
    for c in range(seq // CUM_CHUNK):
        part = ls[:, c * CUM_CHUNK:(c + 1) * CUM_CHUNK]
        cs = jnp.dot(part, upper, preferred_element_type=f32, precision=lax.Precision.HIGHEST) + carry
        o_ref[:, c * CUM_CHUNK:(c + 1) * CUM_CHUNK] = cs * LOG2E
        carry = cs[:, CUM_CHUNK - 1:CUM_CHUNK]


def _cum_call(proj, b_forget, bsz, seq):
    assert seq % CUM_CHUNK == 0
    return pl.pallas_call(
        _cum_kernel,
        grid=(bsz,),
        in_specs=[pl.BlockSpec((seq, LANES), lambda b: (b, C_FB // LANES)),
                  pl.BlockSpec((N_HEADS_B, 1), lambda b: (0, 0))],
        out_specs=pl.BlockSpec((None, N_HEADS_B, seq), lambda b: (b, 0, 0)),
        out_shape=jax.ShapeDtypeStruct((bsz, N_HEADS_B, seq), f32),
        compiler_params=_params(("parallel",)),
        name="cum",
    )(proj, b_forget.reshape(N_HEADS_B, 1))


def _fox_kernel(q_ref, k_ref, vt_ref, cq_ref, ck_ref, o_ref, s_scr, *, tq, ts):
    qi = pl.program_id(2)
    nc = ((qi + 1) * tq + ts - 1) // ts
    mine = lax.broadcasted_iota(i32, (LANES, tq), 0) // HEAD_DIM_B == pl.program_id(1) % 2
    qt = jnp.where(mine, q_ref[...].T * (HEAD_DIM_B ** -0.5 * LOG2E), 0.0).astype(bf16)
    cq = cq_ref[...]
    qpos = qi * tq + lax.broadcasted_iota(i32, (ts, tq), 1)
    krow = lax.broadcasted_iota(i32, (ts, tq), 0)

    def score(c, m, masked):
        start = pl.multiple_of(c * ts, ts)
        s = jnp.dot(k_ref[pl.ds(start, ts), :].astype(bf16), qt, preferred_element_type=f32)
        ck = jnp.broadcast_to(ck_ref[c], (LANES, ts)).T
        s = s + cq - jnp.concatenate([ck] * (tq // LANES), axis=1)
        if masked:
            s = jnp.where(start + krow <= qpos, s, NEG_BIG)
        s_scr[c] = s
        return jnp.maximum(m, jnp.max(s, axis=0, keepdims=True))

    nfull = (qi * tq) // ts
    m = _loop_pairs(0, nfull, lambda c, m: score(c, m, False), jnp.full((1, tq), NEG_BIG, f32))
    m = lax.fori_loop(nfull, nc, lambda c, m: score(c, m, True), m)

    def attend(c, acc):
        p = jnp.exp2(s_scr[c] - m)
        return acc + jnp.dot(vt_ref[c], p.astype(bf16), preferred_element_type=f32)

    acc = _loop_pairs(0, nc, attend, jnp.zeros((HEAD_DIM_B + ONES_ROWS, tq), f32))
    o_ref[...] = acc[0:HEAD_DIM_B] / acc[HEAD_DIM_B:HEAD_DIM_B + 1]


def _fox_call(proj, vt, cq, ck, tq, ts):
    bsz, nh, _, _, _ = vt.shape
    hd = HEAD_DIM_B
    seq = proj.shape[0] // bsz
    nq = seq // tq
    kern = functools.partial(_fox_kernel, tq=tq, ts=ts)
    return pl.pallas_call(
        kern,
        grid=(bsz, nh, nq),
        in_specs=[pl.BlockSpec((tq, LANES), lambda b, h, i: (b * nq + i, C_QB // LANES + h // 2)),
                  pl.BlockSpec((seq, LANES), lambda b, h, i: (b, C_KB // LANES + h // 2)),
                  pl.BlockSpec((None, None, seq // ts, hd + ONES_ROWS, ts), lambda b, h, i: (b, h, 0, 0, 0)),
                  pl.BlockSpec((None, None, 1, tq), lambda b, h, i: (b, h, 0, i)),
                  pl.BlockSpec((None, None, seq // ts, 1, ts), lambda b, h, i: (b, h, 0, 0, 0))],
        out_specs=pl.BlockSpec((None, None, hd, tq), lambda b, h, i: (b, h, 0, i)),
        out_shape=jax.ShapeDtypeStruct((bsz, nh, hd, seq), f32),
        scratch_shapes=[pltpu.VMEM((seq // ts, ts, tq), f32)],
        compiler_params=_params(("parallel", "parallel", "arbitrary")),
        name="fox",
    )(proj, proj, vt, cq, ck)


def _outproj_kernel(oa_ref, ob_ref, x_ref, g1_ref, ga_ref, gb_ref, wa_ref, wb_ref,
                    gf_ref, sc_ref, sh_ref, wq_ref, x1_ref, h2_ref, qp_ref):
    oa = oa_ref[...]
    ob = ob_ref[...].reshape(WIDTH_B, oa.shape[0]).T
    na = oa * lax.rsqrt(jnp.mean(oa * oa, axis=-1, keepdims=True) + EPS) * ga_ref[...]
    nb = ob * lax.rsqrt(jnp.mean(ob * ob, axis=-1, keepdims=True) + EPS) * gb_ref[...]
    res = (jnp.dot(na.astype(bf16), wa_ref[...], preferred_element_type=f32)
           + jnp.dot(nb.astype(bf16), wb_ref[...], preferred_element_type=f32))
    x1 = x_ref[...] + g1_ref[0] * res
    x1_ref[...] = x1
    h2 = x1 * lax.rsqrt(jnp.mean(x1 * x1, axis=-1, keepdims=True) + EPS) * gf_ref[...]
    h2 = h2 * (1.0 + sc_ref[0]) + sh_ref[0]
    h2_ref[...] = h2
    qp_ref[...] = jnp.dot(h2.astype(bf16), wq_ref[...], preferred_element_type=f32)


def _outproj_call(oa, ob, x2, g1, ga, gb, wa, wb, gf, sc, sh, wq, seq, tm):
    n, d = x2.shape
    per_b = seq // tm
    tok = lambda i: (i, 0)
    bat = lambda i: (i // per_b, 0, 0)
    cst = lambda i: (0, 0)
    nq = wq.shape[1]
    return pl.pallas_call(
        _outproj_kernel,
        grid=(n // tm,),
        in_specs=[pl.BlockSpec((tm, WIDTH_A), tok),
                  pl.BlockSpec((None, N_HEADS_B, HEAD_DIM_B, tm), lambda i: (i // per_b, 0, 0, i % per_b)),
                  pl.BlockSpec((tm, d), tok), pl.BlockSpec((1, 1, d), bat),
                  pl.BlockSpec((1, WIDTH_A), cst), pl.BlockSpec((1, WIDTH_B), cst),
                  pl.BlockSpec((WIDTH_A, d), cst), pl.BlockSpec((WIDTH_B, d), cst),
                  pl.BlockSpec((1, d), cst), pl.BlockSpec((1, 1, d), bat),
                  pl.BlockSpec((1, 1, d), bat), pl.BlockSpec((d, nq), cst)],
        out_specs=[pl.BlockSpec((tm, d), tok), pl.BlockSpec((tm, d), tok),
                   pl.BlockSpec((tm, nq), tok)],
        out_shape=[jax.ShapeDtypeStruct((n, d), f32), jax.ShapeDtypeStruct((n, d), f32),
                   jax.ShapeDtypeStruct((n, nq), f32)],
        compiler_params=_params(("parallel",)),
        name="outproj",
    )(oa, ob, x2, g1, ga, gb, wa, wb, gf, sc, sh, wq)


def _argmax_rows(x, iota):
    vals = [x[j:j + 8] for j in range(0, x.shape[0], 8)]
    idxs = [iota[j:j + 8] for j in range(0, x.shape[0], 8)]
    while len(vals) > 1:
        nv, ni = [], []
        for a in range(0, len(vals) - 1, 2):
            keep = vals[a] >= vals[a + 1]
            nv.append(jnp.where(keep, vals[a], vals[a + 1]))
            ni.append(jnp.where(keep, idxs[a], idxs[a + 1]))
        if len(vals) % 2:
            nv.append(vals[-1])
            ni.append(idxs[-1])
        vals, idxs = nv, ni
    m = jnp.max(vals[0], axis=0, keepdims=True)
    pos = jnp.min(jnp.where(vals[0] == m, idxs[0], x.shape[0]), axis=0, keepdims=True)
    return m, pos


def _topk_rows(x, kk):
    iota = lax.broadcasted_iota(i32, x.shape, 0)
    vals, idxs = [], []
    for _ in range(kk):
        m, pos = _argmax_rows(x, iota)
        vals.append(m)
        idxs.append(pos)
        x = jnp.where(iota == pos, -jnp.inf, x)
    return jnp.concatenate(vals, axis=0), jnp.concatenate(idxs, axis=0)


CAND_PAIRS = [(a, b) for a in range(PEER_TOPK) for b in range(PEER_TOPK) if (a + 1) * (b + 1) <= PEER_TOPK]
CAND_ROWS = -(-len(CAND_PAIRS) // 8) * 8


def _cand_select():
    sel = np.zeros((2, CAND_ROWS, PEER_TOPK), np.float32)
    for r, (a, b) in enumerate(CAND_PAIRS):
        sel[0, r, a] = 1.0
        sel[1, r, b] = 1.0
    return sel


def _route_kernel(qp_ref, k1_ref, k2_ref, sel_ref, dup_ref, rows_ref, g_ref, *, nl):
    half = PEER_KEY_DIM // 2
    hp = lax.Precision.HIGHEST
    pick = lambda j, v: jnp.dot(sel_ref[j], v, preferred_element_type=f32, precision=hp)
    rows = lax.broadcasted_iota(i32, (CAND_ROWS, LANES), 0)

    def head(q):
        s1 = lax.dot_general(k1_ref[...], q[:, :half], NT_DIMS, preferred_element_type=f32, precision=hp)
        s2 = lax.dot_general(k2_ref[...], q[:, half:], NT_DIMS, preferred_element_type=f32, precision=hp)
        v1, i1 = _topk_rows(s1, PEER_TOPK)
        v2, i2 = _topk_rows(s2, PEER_TOPK)
        cand = jnp.where(rows < len(CAND_PAIRS), pick(0, v1) + pick(1, v2), -jnp.inf)
        cand_e = pick(0, i1.astype(f32)) * N_KEYS + pick(1, i2.astype(f32))
        best, experts = [], []
        for _ in range(PEER_TOPK):
            m, pos = _argmax_rows(cand, rows)
            hit = rows == pos
            experts.append(jnp.sum(jnp.where(hit, cand_e, 0.0), axis=0, keepdims=True))
            best.append(m)
            cand = jnp.where(hit, -jnp.inf, cand)
        best = jnp.concatenate(best, axis=0)
        e = jnp.exp(best - best[0:1])
        return jnp.concatenate(experts, axis=0), e / jnp.sum(e, axis=0, keepdims=True)

    def lane_tile(l, carry):
        start = pl.multiple_of(l * LANES, LANES)
        outs = [head(qp_ref[pl.ds(start, LANES), h * PEER_KEY_DIM:(h + 1) * PEER_KEY_DIM])
                for h in range(PEER_HEADS)]
        ids = jnp.concatenate([o[0] for o in outs], axis=0)
        gates = jnp.concatenate([o[1] for o in outs], axis=0)
        rows_ref[pl.ds(start, LANES), :] = (ids.T * ROWS_PER_EXPERT).astype(i32)
        g_ref[pl.ds(start, LANES), :] = jnp.dot(dup_ref[...], gates, preferred_element_type=f32,
                                                precision=hp).T
        return carry

    lax.fori_loop(0, nl, lane_tile, 0)


def _route_call(qp, k1, k2, tr):
    n = qp.shape[0]
    sel = jnp.asarray(_cand_select())
    dup = jnp.asarray(np.repeat(np.eye(PEER_SLOTS, dtype=np.float32), 2, axis=0))
    kern = functools.partial(_route_kernel, nl=tr // LANES)
    cst = lambda i: (0, 0)
    return pl.pallas_call(
        kern,
        grid=(n // tr,),
        in_specs=[pl.BlockSpec((tr, PEER_HEADS * PEER_KEY_DIM), lambda i: (i, 0)),
                  pl.BlockSpec(k1.shape, cst),
                  pl.BlockSpec(k2.shape, cst),
                  pl.BlockSpec(sel.shape, lambda i: (0, 0, 0)),
                  pl.BlockSpec(dup.shape, cst)],
        out_specs=[pl.BlockSpec((tr, PEER_SLOTS), lambda i: (i, 0)),
                   pl.BlockSpec((tr, 2 * PEER_SLOTS), lambda i: (i, 0))],
        out_shape=[jax.ShapeDtypeStruct((n, PEER_SLOTS), i32),
                   jax.ShapeDtypeStruct((n, 2 * PEER_SLOTS), f32)],
        compiler_params=_params(("parallel",)),
        name="route",
    )(qp, k1, k2, sel, dup)


ROWS_PER_EXPERT = 4
TILE_STRIDE = 136
TOK_UNROLL = 64
HALF = 512


def _pack_kernel(t_ref, o_ref):
    te = t_ref.shape[0]
    bits = lax.bitcast_convert_type(t_ref[...].astype(bf16).astype(f32), i32)
    word = (bits[:, HALF:] & jnp.int32(-65536)) | lax.shift_right_logical(bits[:, :HALF], 16)
    for r in range(ROWS_PER_EXPERT):
        o_ref[pl.ds(r, te, stride=ROWS_PER_EXPERT), :] = word[:, r * LANES:(r + 1) * LANES]


def _pack_table(tab, te=256):
    e, d = tab.shape
    return pl.pallas_call(
        _pack_kernel,
        grid=(e // te,),
        in_specs=[pl.BlockSpec((te, d), lambda i: (i, 0))],
        out_specs=pl.BlockSpec((te * ROWS_PER_EXPERT, LANES), lambda i: (i, 0)),
        out_shape=jax.ShapeDtypeStruct((e * ROWS_PER_EXPERT, LANES), i32),
        compiler_params=_params(("parallel",)),
        name="pack",
    )(tab)


def _gather_tile(tab_ref, rows_ref, t, tile_ref):
    tok_rows = rows_ref.at[t]
    for k in range(PEER_SLOTS):
        row = pl.multiple_of(tok_rows[k], ROWS_PER_EXPERT)
        tile_ref[pl.ds(k, ROWS_PER_EXPERT, stride=TILE_STRIDE), :] = tab_ref[pl.ds(row, ROWS_PER_EXPERT), :]
    chunks = [pltpu.bitcast(tile_ref[j * TILE_STRIDE:j * TILE_STRIDE + PEER_SLOTS, :], bf16)
              for j in range(ROWS_PER_EXPERT)]
    return jnp.concatenate(chunks, axis=1)


def _for_tokens(tab_ref, rows_ref, tile_scr, base, per_token):
    for tl in range(TOK_UNROLL):
        per_token(tl, base + tl, _gather_tile(tab_ref, rows_ref, base + tl, tile_scr.at[tl % 2]))


def _peer_u_kernel(ids_ref, h_ref, g_ref, u_ref, o_ref, tile_scr, act_scr, *, tt):
    even = lax.broadcasted_iota(i32, (1, 2 * PEER_SLOTS), 1) % 2 == 0

    def group(gi, carry):
        base = pl.multiple_of(gi * TOK_UNROLL, TOK_UNROLL)
        rows = []

        def per_token(tl, t, r):
            row = h_ref[pl.ds(t, 1), :]
            h = jnp.concatenate([row[:, :HALF], row[:, HALF:]], axis=0)
            major = h.astype(bf16)
            minor = (h - major.astype(f32)).astype(bf16)
            out = lax.dot_general(jnp.concatenate([major, minor], axis=0), r, NT_DIMS,
                                  preferred_element_type=f32)
            rows.append(jnp.where(even, out[0:1] + out[2:3], out[1:2] + out[3:4]))

        _for_tokens(u_ref, ids_ref, tile_scr, base, per_token)
        act_scr[pl.ds(base, TOK_UNROLL), :] = jnp.concatenate(rows, axis=0)
        return carry

    lax.fori_loop(0, tt // TOK_UNROLL, group, 0)
    part = act_scr[...]
    act = part + jnp.where(even, pltpu.roll(part, 2 * PEER_SLOTS - 1, 1), pltpu.roll(part, 1, 1))
    gelu = 0.5 * act * (1.0 + lax.erf(act * (2.0 ** -0.5)))
    o_ref[...] = g_ref[...] * gelu


def _peer_u_call(ids, h3, g2, u_packed, tt):
    n = ids.shape[0]
    kern = functools.partial(_peer_u_kernel, tt=tt)
    return pl.pallas_call(
        kern,
        grid=(n // tt,),
        in_specs=[pl.BlockSpec((tt, PEER_SLOTS), lambda i: (i, 0), memory_space=pltpu.SMEM),
                  pl.BlockSpec((tt, 2 * HALF), lambda i: (i, 0)),
                  pl.BlockSpec((tt, 2 * PEER_SLOTS), lambda i: (i, 0)),
                  pl.BlockSpec(u_packed.shape, lambda i: (0, 0), pipeline_mode=pl.Buffered(1))],
        out_specs=pl.BlockSpec((tt, 2 * PEER_SLOTS), lambda i: (i, 0)),
        out_shape=jax.ShapeDtypeStruct((n, 2 * PEER_SLOTS), f32),
        scratch_shapes=[pltpu.VMEM((2, ROWS_PER_EXPERT * TILE_STRIDE, LANES), i32),
                        pltpu.VMEM((tt, 2 * PEER_SLOTS), f32)],
        compiler_params=_params(("arbitrary",)),
        name="peer_u",
    )(ids, h3, g2, u_packed)


def _peer_v_kernel(ids_ref, w_ref, v_ref, o_ref, tile_scr, *, tt):
    even = lax.broadcasted_iota(i32, (1, 2 * PEER_SLOTS), 1) % 2 == 0

    def group(gi, carry):
        base = pl.multiple_of(gi * TOK_UNROLL, TOK_UNROLL)
        wts = w_ref[pl.ds(base, TOK_UNROLL), :]
        rows = []

        def per_token(tl, t, r):
            w = wts[tl:tl + 1]
            lhs = jnp.concatenate([jnp.where(even, w, 0.0), jnp.where(even, 0.0, w)], axis=0)
            out = jnp.dot(lhs.astype(bf16), r, preferred_element_type=f32)
            rows.append(jnp.concatenate([out[0:1], out[1:2]], axis=1))
            if len(rows) == 8:
                o_ref[pl.ds(pl.multiple_of(base + tl - 7, 8), 8), :] = jnp.concatenate(rows, axis=0)
                rows.clear()

        _for_tokens(v_ref, ids_ref, tile_scr, base, per_token)
        return carry

    lax.fori_loop(0, tt // TOK_UNROLL, group, 0)


def _peer_v_call(ids, wgt2, v_packed, tt):
    n = ids.shape[0]
    kern = functools.partial(_peer_v_kernel, tt=tt)
    return pl.pallas_call(
        kern,
        grid=(n // tt,),
        in_specs=[pl.BlockSpec((tt, PEER_SLOTS), lambda i: (i, 0), memory_space=pltpu.SMEM),
                  pl.BlockSpec((tt, 2 * PEER_SLOTS), lambda i: (i, 0)),
                  pl.BlockSpec(v_packed.shape, lambda i: (0, 0), pipeline_mode=pl.Buffered(1))],
        out_specs=pl.BlockSpec((tt, 2 * HALF), lambda i: (i, 0)),
        out_shape=jax.ShapeDtypeStruct((n, 2 * HALF), f32),
        scratch_shapes=[pltpu.VMEM((2, ROWS_PER_EXPERT * TILE_STRIDE, LANES), i32)],
        compiler_params=_params(("arbitrary",)),
        name="peer_v",
    )(ids, wgt2, v_packed)


def _final_kernel(x1_ref, p_ref, g2_ref, gf_ref, o_ref):
    x2 = x1_ref[...] + g2_ref[0] * p_ref[...]
    o_ref[...] = x2 * lax.rsqrt(jnp.mean(x2 * x2, axis=-1, keepdims=True) + EPS) * gf_ref[...]


def _final_call(x1, peer, g2, gf, seq, tm):
    n, d = x1.shape
    per_b = seq // tm
    tok = lambda i: (i, 0)
    return pl.pallas_call(
        _final_kernel,
        grid=(n // tm,),
        in_specs=[pl.BlockSpec((tm, d), tok), pl.BlockSpec((tm, d), tok),
                  pl.BlockSpec((1, 1, d), lambda i: (i // per_b, 0, 0)),
                  pl.BlockSpec((1, d), lambda i: (0, 0))],
        out_specs=pl.BlockSpec((tm, d), tok),
        out_shape=jax.ShapeDtypeStruct((n, d), f32),
        compiler_params=_params(("parallel",)),
        name="final",
    )(x1, peer, g2, gf)


def _rope_tables(positions):
    def tab(rot):
        inv = ROPE_THETA ** (-jnp.arange(0, rot, 2, dtype=f32) / rot)
        ang = positions.astype(f32)[..., None] * inv
        return jnp.cos(ang), jnp.sin(ang)

    cos_a, sin_a = tab(ROPE_DIM_A)
    cos_i, sin_i = tab(IDX_ROPE_DIM)
    b, s = positions.shape
    one = jnp.ones((b, s, 1), f32)
    zero = jnp.zeros((b, s, 4), f32)
    trig = jnp.concatenate([cos_a, cos_i, one, zero[..., :3], sin_a, sin_i, zero], axis=-1)
    return trig.reshape(b * s, TRIG)


def kernel(x, c, positions, w_ada, b_ada, g_mix, w_in, g_kv, w_uk, w_uv, b_forget,
           g_out_a, g_out_b, w_out, g_ffn, w_peer_q, peer_keys1, peer_keys2, peer_u,
           peer_v, g_final):
    b, s, d = x.shape
    n = b * s
    assert w_ada.shape[0] == 1, "single layer supported"
    tm = min(512, s)
    x2 = x.reshape(n, d)

    mod = _mod_call(c, w_ada[0], b_ada[0])
    mod = mod.reshape(b, 6, 1, d)
    shift1, scale1, gate1, shift2, scale2, gate2 = [mod[:, j] for j in range(6)]

    perm = _in_perm()
    w_in_r = jnp.where((perm >= 0)[None, :], w_in[0][:, np.maximum(perm, 0)], 0.0).astype(bf16)
    proj, kcat, ik, latt, vbt = _inproj_call(x2, scale1, shift1, g_mix[0].reshape(1, d), w_in_r,
                                             _rope_tables(positions), g_kv[0].reshape(1, KV_RANK), s, tm)

    wuk = w_uk[0].transpose(1, 0, 2).astype(bf16)
    wuvt = w_uv[0].transpose(1, 2, 0).astype(bf16)
    o_a = _dsa_call(proj, kcat.reshape(b, s, KCAT), latt, ik.reshape(b, s, LANES), wuk, wuvt,
                    b, s, tq=128, tk=tm)

    cum = _cum_call(proj, b_forget[0], b, s)
    o_bt = _fox_call(proj, vbt, cum[:, :, None, :], cum.reshape(b, N_HEADS_B, s // tm, 1, tm),
                     tq=min(512, s), ts=tm)

    wo = w_out[0].astype(bf16)
    x1, h2, qp = _outproj_call(o_a, o_bt, x2, gate1, g_out_a[0].reshape(1, WIDTH_A),
                               g_out_b[0].reshape(1, WIDTH_B), wo[:WIDTH_A], wo[WIDTH_A:],
                               g_ffn[0].reshape(1, d), scale2, shift2,
                               w_peer_q[0].astype(bf16), s, tm)

    ids, gates2 = _route_call(qp, peer_keys1[0], peer_keys2[0], tr=min(1024, n))
    tt = 128
    wgt2 = _peer_u_call(ids, h2, gates2, _pack_table(peer_u[0]), tt)
    peer = _peer_v_call(ids, wgt2, _pack_table(peer_v[0]), tt)

    out = _final_call(x1, peer, gate2, g_final.reshape(1, d), s, tm)
    return out.reshape(b, s, d)
```

```python
import functools

import jax
import jax.numpy as jnp
import numpy as np
from jax import lax
from jax.experimental import pallas as pl
from jax.experimental.pallas import tpu as pltpu

f32 = jnp.float32
bf16 = jnp.bfloat16
i32 = jnp.int32

N_HEADS_A = 8
HEAD_DIM_A = 64
ROPE_DIM_A = 16
NOPE_DIM_A = HEAD_DIM_A - ROPE_DIM_A
KV_RANK = 128
IDX_HEADS = 8
IDX_DIM = 32
IDX_ROPE_DIM = 8
TOPK_MAX = 256
N_HEADS_B = 8
HEAD_DIM_B = 64
WIDTH_A = N_HEADS_A * HEAD_DIM_A
WIDTH_B = N_HEADS_B * HEAD_DIM_B
ROPE_THETA = 500000.0
PEER_HEADS = 8
N_KEYS = 128
PEER_KEY_DIM = 128
PEER_TOPK = 16
PEER_SLOTS = PEER_HEADS * PEER_TOPK
EPS = 1e-6
IN_SIZES = (WIDTH_A, KV_RANK, ROPE_DIM_A, IDX_HEADS * IDX_DIM, IDX_DIM, IDX_HEADS,
            WIDTH_B, WIDTH_B, WIDTH_B, N_HEADS_B)

LANES = 128
TOKEN_TILE = 512
DSA_QUERY_TILE = 128
FOX_QUERY_TILE = 512
ROUTE_TILE = 1024
PEER_TILE = 64
NEG_BIG = -1e30
INT_MIN = -2147483648
VMEM_LIMIT = 56 * 1024 * 1024

C_R1 = 0
C_R2 = 128
C_LAT = 256
C_QN = 384
C_MISC = 768
C_QB = 1024
C_KB = 1536
C_VB = 2048
C_TOTAL = 2560

NT_DIMS = (((1,), (1,)), ((), ()))


def _in_perm():
    offs = np.cumsum((0,) + IN_SIZES)
    o_qa, o_lat, o_kr, o_iq, o_ik, o_iw, o_qb, o_kb, o_vb, o_fb = offs[:10]
    perm = -np.ones((C_TOTAL,), np.int64)
    ha, hi = ROPE_DIM_A // 2, IDX_ROPE_DIM // 2
    for h in range(N_HEADS_A):
        for j in range(ha):
            perm[C_R1 + h * ha + j] = o_qa + h * HEAD_DIM_A + j
            perm[C_R1 + 64 + h * ha + j] = o_qa + h * HEAD_DIM_A + ha + j
        for j in range(NOPE_DIM_A):
            perm[C_QN + h * NOPE_DIM_A + j] = o_qa + h * HEAD_DIM_A + ROPE_DIM_A + j
    for h in range(IDX_HEADS):
        for j in range(hi):
            perm[C_R2 + j * IDX_HEADS + h] = o_iq + h * IDX_DIM + j
            perm[C_R2 + 64 + j * IDX_HEADS + h] = o_iq + h * IDX_DIM + hi + j
        for j in range(IDX_DIM - IDX_ROPE_DIM):
            perm[C_MISC + j * IDX_HEADS + h] = o_iq + h * IDX_DIM + IDX_ROPE_DIM + j
    for j in range(ha):
        perm[C_R2 + 32 + j] = o_kr + j
        perm[C_R2 + 64 + 32 + j] = o_kr + ha + j
    for j in range(hi):
        perm[C_R2 + 40 + j] = o_ik + j
        perm[C_R2 + 64 + 40 + j] = o_ik + hi + j
    for j in range(IDX_DIM - IDX_ROPE_DIM):
        perm[C_MISC + 192 + j] = o_ik + IDX_ROPE_DIM + j
    for j in range(IDX_HEADS):
        perm[C_MISC + 216 + j] = o_iw + j
    for j in range(N_HEADS_B):
        perm[C_MISC + 224 + j] = o_fb + j
    perm[C_LAT:C_LAT + KV_RANK] = o_lat + np.arange(KV_RANK)
    perm[C_QB:C_QB + WIDTH_B] = o_qb + np.arange(WIDTH_B)
    perm[C_KB:C_KB + WIDTH_B] = o_kb + np.arange(WIDTH_B)
    perm[C_VB:C_VB + WIDTH_B] = o_vb + np.arange(WIDTH_B)
    return perm


def _params(sem):
    return pltpu.CompilerParams(dimension_semantics=sem, vmem_limit_bytes=VMEM_LIMIT)


def _loop_pairs(lo, hi, body, init):
    pairs = (hi - lo) // 2

    def two(i, carry):
        return body(lo + 2 * i + 1, body(lo + 2 * i, carry))

    carry = lax.fori_loop(0, pairs, two, init)
    return lax.fori_loop(lo + 2 * pairs, hi, body, carry)


def _mod_kernel(c_ref, w_ref, b_ref, o_ref):
    c = c_ref[...]
    ca = c * jax.nn.sigmoid(c)
    o_ref[...] = jnp.dot(ca, w_ref[...], preferred_element_type=f32,
                         precision=lax.Precision.HIGHEST) + b_ref[...]


def _mod_call(c, w, b):
    bsz, d = c.shape
    n = w.shape[1]
    return pl.pallas_call(
        _mod_kernel,
        grid=(n // d,),
        in_specs=[pl.BlockSpec((bsz, d), lambda j: (0, 0)),
                  pl.BlockSpec((d, d), lambda j: (0, j)),
                  pl.BlockSpec((1, d), lambda j: (0, j))],
        out_specs=pl.BlockSpec((bsz, d), lambda j: (0, j)),
        out_shape=jax.ShapeDtypeStruct((bsz, n), f32),
        compiler_params=_params(("arbitrary",)),
        name="mod",
    )(c, w, b.reshape(1, n))


KCAT = 256
ONES_ROWS = 16


def _with_ones_rows(vt):
    lead = vt.shape[:-2]
    t = vt.shape[-1]
    ones = jnp.ones(lead + (1, t), vt.dtype)
    zeros = jnp.zeros(lead + (ONES_ROWS - 1, t), vt.dtype)
    return jnp.concatenate([vt, ones, zeros], axis=-2)


def _lanes_from(lane, pieces):
    out = jnp.zeros(lane.shape, f32)
    for end, src, start in reversed(pieces):
        begin = max([e for e, _, _ in pieces if e < end], default=0)
        out = jnp.where(lane < end, pltpu.roll(src, (begin - start) % LANES, 1), out)
    return out


TRIG = 32


def _trig_expand():
    e = np.zeros((TRIG, 4 * LANES), np.float32)
    cos_a, cos_i, one, sin_a, sin_i = 0, 8, 12, 16, 24
    for lane in range(LANES):
        l, sign = lane % 64, (-1.0 if lane < 64 else 1.0)
        e[cos_a + l % 8, lane] = 1.0
        e[sin_a + l % 8, LANES + lane] = sign
        if l < 32:
            e[cos_i + l // IDX_HEADS, 2 * LANES + lane] = 1.0
            e[sin_i + l // IDX_HEADS, 3 * LANES + lane] = sign
        elif l < 40:
            e[cos_a + l - 32, 2 * LANES + lane] = 1.0
            e[sin_a + l - 32, 3 * LANES + lane] = sign
        elif l < 44:
            e[cos_i + l - 40, 2 * LANES + lane] = 1.0
            e[sin_i + l - 40, 3 * LANES + lane] = sign
        else:
            e[one, 2 * LANES + lane] = 1.0
    return e


def _inproj_kernel(x_ref, sc_ref, sh_ref, g_ref, w_ref, trig_ref, exp_ref,
                   gkv_ref, o_ref, kcat_ref, ik_ref, latt_ref, vbt_ref):
    x = x_ref[...]
    ms = jnp.mean(x * x, axis=-1, keepdims=True)
    h = x * lax.rsqrt(ms + EPS) * g_ref[...]
    h = h * (1.0 + sc_ref[0]) + sh_ref[0]
    p = jnp.dot(h.astype(bf16), w_ref[...], preferred_element_type=f32)
    tab = jnp.dot(trig_ref[...], exp_ref[...], preferred_element_type=f32,
                  precision=lax.Precision.HIGHEST)
    r1 = p[:, C_R1:C_R1 + LANES]
    o_ref[:, C_R1:C_R1 + LANES] = (r1 * tab[:, 0:LANES]
                                   + pltpu.roll(r1, 64, 1) * tab[:, LANES:2 * LANES])
    r2 = p[:, C_R2:C_R2 + LANES]
    r2 = r2 * tab[:, 2 * LANES:3 * LANES] + pltpu.roll(r2, 64, 1) * tab[:, 3 * LANES:]
    o_ref[:, C_R2:C_R2 + LANES] = r2
    lat = p[:, C_LAT:C_LAT + KV_RANK]
    lms = jnp.mean(lat * lat, axis=-1, keepdims=True)
    lat = lat * lax.rsqrt(lms + EPS) * gkv_ref[...]
    o_ref[:, C_LAT:C_LAT + KV_RANK] = lat
    o_ref[:, C_QN:] = p[:, C_QN:]

    lane = lax.broadcasted_iota(i32, r2.shape, 1)
    kr = _lanes_from(lane, [(8, r2, 32), (16, r2, 96)])
    kcat_ref[...] = jnp.concatenate([lat, kr], axis=1).astype(bf16)
    misc_hi = p[:, C_MISC + LANES:C_MISC + 2 * LANES]
    ik_ref[...] = _lanes_from(lane, [(4, r2, 40), (8, r2, 104), (32, misc_hi, 64)]).astype(bf16)
    latt_ref[...] = _with_ones_rows(lat.T.astype(bf16))
    vt = p[:, C_VB:C_VB + WIDTH_B].T.astype(bf16)
    vbt_ref[...] = _with_ones_rows(vt.reshape(N_HEADS_B, HEAD_DIM_B, vt.shape[1]))


def _inproj_call(x2, sc, sh, g, w, trig, gkv, seq, tm):
    n, d = x2.shape
    per_b = seq // tm
    bsz = n // seq
    tok = lambda i: (i, 0)
    bat = lambda i: (i // per_b, 0, 0)
    cst = lambda i: (0, 0)
    return pl.pallas_call(
        _inproj_kernel,
        grid=(n // tm,),
        in_specs=[pl.BlockSpec((tm, d), tok),
                  pl.BlockSpec((1, 1, d), bat),
                  pl.BlockSpec((1, 1, d), bat),
                  pl.BlockSpec((1, d), cst),
                  pl.BlockSpec((d, C_TOTAL), cst),
                  pl.BlockSpec((tm, TRIG), tok),
                  pl.BlockSpec((TRIG, 4 * LANES), cst),
                  pl.BlockSpec((1, KV_RANK), cst)],
        out_specs=[pl.BlockSpec((tm, C_TOTAL), tok),
                   pl.BlockSpec((tm, KCAT), tok),
                   pl.BlockSpec((tm, LANES), tok),
                   pl.BlockSpec((None, None, KV_RANK + ONES_ROWS, tm),
                                lambda i: (i // per_b, i % per_b, 0, 0)),
                   pl.BlockSpec((None, N_HEADS_B, None, HEAD_DIM_B + ONES_ROWS, tm),
                                lambda i: (i // per_b, 0, i % per_b, 0, 0))],
        out_shape=[jax.ShapeDtypeStruct((n, C_TOTAL), f32),
                   jax.ShapeDtypeStruct((n, KCAT), bf16),
                   jax.ShapeDtypeStruct((n, LANES), bf16),
                   jax.ShapeDtypeStruct((bsz, per_b, KV_RANK + ONES_ROWS, tm), bf16),
                   jax.ShapeDtypeStruct((bsz, N_HEADS_B, per_b, HEAD_DIM_B + ONES_ROWS, tm), bf16)],
        compiler_params=_params(("parallel",)),
        name="inproj",
    )(x2, sc, sh, g, w, trig, jnp.asarray(_trig_expand()), gkv)


LOG2E = 1.4426950408889634


def _sortable(x):
    bits = lax.bitcast_convert_type(x, i32)
    return bits ^ ((bits >> 31) & 0x7FFFFFFF)


def _dsa_kernel(r1_ref, r2_ref, qn_ref, misc_ref, kcat_ref, latt_ref, ik_ref, wuk_ref, wuvt_ref,
                o_ref, key_scr, iq_scr, m_scr, acc_scr, *, tq, tk, topk, nbits, seq):
    nh = N_HEADS_A
    qi = pl.program_id(1)
    nc = ((qi + 1) * tq + tk - 1) // tk
    qpos = qi * tq + lax.broadcasted_iota(i32, (tk, tq), 1)
    krow = lax.broadcasted_iota(i32, (tk, tq), 0)

    r2t = r2_ref[...].T
    mt = misc_ref[...].T
    iq_scr[...] = jnp.concatenate([r2t[0:32], r2t[64:96], mt[0:192]], axis=0)
    iqt = jnp.concatenate([iq_scr[pl.ds(h, IDX_DIM, stride=IDX_HEADS), :] for h in range(IDX_HEADS)],
                          axis=1)
    iqt = jnp.concatenate([iqt, jnp.zeros((LANES - IDX_DIM, IDX_HEADS * tq), f32)], axis=0).astype(bf16)
    iwt = mt[216:224] * (IDX_DIM ** -0.5 * IDX_HEADS ** -0.5)

    def score_chunk(c, carry):
        start = pl.multiple_of(c * tk, tk)
        d = jnp.dot(ik_ref[pl.ds(start, tk), :], iqt, preferred_element_type=f32)
        sc = jnp.zeros((tk, tq), f32)
        for h in range(IDX_HEADS):
            sc = sc + jnp.maximum(d[:, h * tq:(h + 1) * tq], 0.0) * iwt[h:h + 1, :]
        sc = jnp.where(start + krow <= qpos, sc, -jnp.inf)
        key_scr[c] = _sortable(sc)
        return carry

    _loop_pairs(0, nc, score_chunk, 0)

    def count(pred):
        def body(c, acc):
            m = pred(key_scr[c], c * tk + krow)
            return acc + jnp.sum(m.reshape(tk // 8, 8, tq), axis=0)
        acc = _loop_pairs(0, nc, body, jnp.zeros((8, tq), i32))
        return jnp.sum(acc, axis=0, keepdims=True)

    def bit_body(i, thr):
        cand = thr + lax.shift_left(jnp.int32(1), 31 - i)
        cnt = count(lambda k, col: jnp.where(k >= cand, 1, 0))
        return jnp.where(cnt >= topk, cand, thr)

    thr = lax.fori_loop(0, 32, bit_body, jnp.full((1, tq), INT_MIN, i32))
    need = topk - count(lambda k, col: jnp.where(k > thr, 1, 0))
    n_eq = count(lambda k, col: jnp.where(k == thr, 1, 0))

    def tie_search():
        def tie_body(i, last):
            cand = last + lax.shift_left(jnp.int32(1), nbits - 1 - i)
            below = count(lambda k, col: jnp.where(k == thr, jnp.where(col < cand, 1, 0), 0))
            return jnp.where(below < need, cand, last)
        return lax.fori_loop(0, nbits, tie_body, jnp.zeros((1, tq), i32))

    ambiguous = jnp.max(jnp.where(n_eq > need, 1, 0)) > 0
    last = lax.cond(ambiguous, tie_search, lambda: jnp.full((1, tq), seq, i32))

    scale = HEAD_DIM_A ** -0.5 * LOG2E
    r1t = r1_ref[...].T
    qnt = qn_ref[...].T.astype(bf16)
    pad = jnp.zeros((KCAT - KV_RANK - ROPE_DIM_A, tq), f32)
    cols = []
    for h in range(nh):
        ql = jnp.dot(wuk_ref[h], qnt[h * NOPE_DIM_A:(h + 1) * NOPE_DIM_A], preferred_element_type=f32)
        cols.append(jnp.concatenate([ql, r1t[h * 8:(h + 1) * 8], r1t[64 + h * 8:64 + (h + 1) * 8], pad],
                                    axis=0))
    qcat = (jnp.concatenate(cols, axis=1) * scale).astype(bf16)

    m_scr[...] = jnp.full(m_scr.shape, NEG_BIG, f32)
    acc_scr[...] = jnp.zeros(acc_scr.shape, f32)

    def attend_chunk(c, carry):
        start = pl.multiple_of(c * tk, tk)
        s = jnp.dot(kcat_ref[pl.ds(start, tk), :], qcat, preferred_element_type=f32)
        k = key_scr[c]
        col = start + krow
        tie = jnp.where(k == thr, jnp.where(col <= last, 0.0, NEG_BIG), NEG_BIG)
        bias = jnp.where(col <= qpos, jnp.where(k > thr, 0.0, tie), NEG_BIG)
        s = s + jnp.concatenate([bias] * nh, axis=1)
        m_old = m_scr[...]
        m_new = jnp.maximum(m_old, jnp.max(s, axis=0, keepdims=True))
        p = jnp.exp2(s - m_new)
        alpha = jnp.exp2(m_old - m_new)
        acc_scr[...] = alpha * acc_scr[...] + jnp.dot(latt_ref[c], p.astype(bf16),
                                                      preferred_element_type=f32)
        m_scr[...] = m_new
        return carry

    _loop_pairs(0, nc, attend_chunk, 0)

    ot = (acc_scr[0:KV_RANK, :] / acc_scr[KV_RANK:KV_RANK + 1, :]).astype(bf16)
    outs = [jnp.dot(wuvt_ref[h], ot[:, h * tq:(h + 1) * tq], preferred_element_type=f32)
            for h in range(nh)]
    o_ref[...] = jnp.concatenate(outs, axis=0).T


def _dsa_call(proj, kcat, latt, ik, wuk, wuvt, bsz, seq, tq, tk):
    nh = N_HEADS_A
    topk = min(TOPK_MAX, seq // 4)
    nbits = max(1, (seq - 1).bit_length())
    nq = seq // tq
    kern = functools.partial(_dsa_kernel, tq=tq, tk=tk, topk=topk, nbits=nbits, seq=seq)
    qcol = lambda width, blk: pl.BlockSpec((tq, width), lambda b, i: (b * nq + i, blk))
    return pl.pallas_call(
        kern,
        grid=(bsz, nq),
        in_specs=[qcol(LANES, C_R1 // LANES), qcol(LANES, C_R2 // LANES),
                  qcol(C_MISC - C_QN, C_QN // (C_MISC - C_QN)), qcol(256, C_MISC // 256),
                  pl.BlockSpec((None, seq, KCAT), lambda b, i: (b, 0, 0)),
                  pl.BlockSpec((None, seq // tk, KV_RANK + ONES_ROWS, tk), lambda b, i: (b, 0, 0, 0)),
                  pl.BlockSpec((None, seq, LANES), lambda b, i: (b, 0, 0)),
                  pl.BlockSpec(wuk.shape, lambda b, i: (0, 0, 0)),
                  pl.BlockSpec(wuvt.shape, lambda b, i: (0, 0, 0))],
        out_specs=pl.BlockSpec((tq, WIDTH_A), lambda b, i: (b * nq + i, 0)),
        out_shape=jax.ShapeDtypeStruct((bsz * seq, WIDTH_A), f32),
        scratch_shapes=[pltpu.VMEM((seq // tk, tk, tq), i32),
                        pltpu.VMEM((IDX_HEADS * IDX_DIM, tq), f32),
                        pltpu.VMEM((1, nh * tq), f32),
                        pltpu.VMEM((KV_RANK + ONES_ROWS, nh * tq), f32)],
        compiler_params=_params(("parallel", "arbitrary")),
        name="dsa",
    )(proj, proj, proj, proj, kcat, latt, ik, wuk, wuvt)


C_FB = C_MISC + 224
CUM_CHUNK = 512


def _cum_kernel(p_ref, b_ref, o_ref):
    seq = p_ref.shape[0]
    lane0 = C_FB % LANES
    x = p_ref[...].T[lane0:lane0 + N_HEADS_B] + b_ref[...]
    ls = jnp.minimum(x, 0.0) - jnp.log(1.0 + jnp.exp(-jnp.abs(x)))
    row = lax.broadcasted_iota(i32, (CUM_CHUNK, CUM_CHUNK), 0)
    col = lax.broadcasted_iota(i32, (CUM_CHUNK, CUM_CHUNK), 1)
    upper = jnp.where(row <= col, 1.0, 0.0)
    carry = jnp.zeros((N_HEADS_B, 1), f32)
    for c in range(seq // CUM_CHUNK):
        part = ls[:, c * CUM_CHUNK:(c + 1) * CUM_CHUNK]
        cs = jnp.dot(part, upper, preferred_element_type=f32, precision=lax.Precision.HIGHEST) + carry
        o_ref[:, c * CUM_CHUNK:(c + 1) * CUM_CHUNK] = cs * LOG2E
        carry = cs[:, CUM_CHUNK - 1:CUM_CHUNK]


def _cum_call(proj, b_forget, bsz, seq):
    assert seq % CUM_CHUNK == 0
    return pl.pallas_call(
        _cum_kernel,
        grid=(bsz,),
        in_specs=[pl.BlockSpec((seq, LANES), lambda b: (b, C_FB // LANES)),
                  pl.BlockSpec((N_HEADS_B, 1), lambda b: (0, 0))],
        out_specs=pl.BlockSpec((None, N_HEADS_B, seq), lambda b: (b, 0, 0)),
        out_shape=jax.ShapeDtypeStruct((bsz, N_HEADS_B, seq), f32),
        compiler_params=_params(("parallel",)),
        name="cum",
    )(proj, b_forget.reshape(N_HEADS_B, 1))


def _fox_kernel(q_ref, k_ref, vt_ref, cq_ref, ck_ref, o_ref, s_scr, *, tq, ts):
    qi = pl.program_id(2)
    nc = ((qi + 1) * tq + ts - 1) // ts
    mine = lax.broadcasted_iota(i32, (LANES, tq), 0) // HEAD_DIM_B == pl.program_id(1) % 2
    qt = jnp.where(mine, q_ref[...].T * (HEAD_DIM_B ** -0.5 * LOG2E), 0.0).astype(bf16)
    cq = cq_ref[...]
    qpos = qi * tq + lax.broadcasted_iota(i32, (ts, tq), 1)
    krow = lax.broadcasted_iota(i32, (ts, tq), 0)

    def score(c, m, masked):
        start = pl.multiple_of(c * ts, ts)
        s = jnp.dot(k_ref[pl.ds(start, ts), :].astype(bf16), qt, preferred_element_type=f32)
        ck = jnp.broadcast_to(ck_ref[c], (LANES, ts)).T
        s = s + cq - jnp.concatenate([ck] * (tq // LANES), axis=1)
        if masked:
            s = jnp.where(start + krow <= qpos, s, NEG_BIG)
        s_scr[c] = s
        return jnp.maximum(m, jnp.max(s, axis=0, keepdims=True))

    nfull = (qi * tq) // ts
    m = _loop_pairs(0, nfull, lambda c, m: score(c, m, False), jnp.full((1, tq), NEG_BIG, f32))
    m = lax.fori_loop(nfull, nc, lambda c, m: score(c, m, True), m)

    def attend(c, acc):
        p = jnp.exp2(s_scr[c] - m)
        return acc + jnp.dot(vt_ref[c], p.astype(bf16), preferred_element_type=f32)

    acc = _loop_pairs(0, nc, attend, jnp.zeros((HEAD_DIM_B + ONES_ROWS, tq), f32))
    o_ref[...] = acc[0:HEAD_DIM_B] / acc[HEAD_DIM_B:HEAD_DIM_B + 1]


def _fox_call(proj, vt, cq, ck, tq, ts):
    bsz, nh, _, _, _ = vt.shape
    hd = HEAD_DIM_B
    seq = proj.shape[0] // bsz
    nq = seq // tq
    kern = functools.partial(_fox_kernel, tq=tq, ts=ts)
    return pl.pallas_call(
        kern,
        grid=(bsz, nh, nq),
        in_specs=[pl.BlockSpec((tq, LANES), lambda b, h, i: (b * nq + i, C_QB // LANES + h // 2)),
                  pl.BlockSpec((seq, LANES), lambda b, h, i: (b, C_KB // LANES + h // 2)),
                  pl.BlockSpec((None, None, seq // ts, hd + ONES_ROWS, ts), lambda b, h, i: (b, h, 0, 0, 0)),
                  pl.BlockSpec((None, None, 1, tq), lambda b, h, i: (b, h, 0, i)),
                  pl.BlockSpec((None, None, seq // ts, 1, ts), lambda b, h, i: (b, h, 0, 0, 0))],
        out_specs=pl.BlockSpec((None, None, hd, tq), lambda b, h, i: (b, h, 0, i)),
        out_shape=jax.ShapeDtypeStruct((bsz, nh, hd, seq), f32),
        scratch_shapes=[pltpu.VMEM((seq // ts, ts, tq), f32)],
        compiler_params=_params(("parallel", "parallel", "arbitrary")),
        name="fox",
    )(proj, proj, vt, cq, ck)


def _outproj_kernel(oa_ref, ob_ref, x_ref, g1_ref, ga_ref, gb_ref, wa_ref, wb_ref,
                    gf_ref, sc_ref, sh_ref, wq_ref, x1_ref, h2_ref, qp_ref):
    oa = oa_ref[...]
    ob = ob_ref[...].reshape(WIDTH_B, oa.shape[0]).T
    na = oa * lax.rsqrt(jnp.mean(oa * oa, axis=-1, keepdims=True) + EPS) * ga_ref[...]
    nb = ob * lax.rsqrt(jnp.mean(ob * ob, axis=-1, keepdims=True) + EPS) * gb_ref[...]
    res = (jnp.dot(na.astype(bf16), wa_ref[...], preferred_element_type=f32)
           + jnp.dot(nb.astype(bf16), wb_ref[...], preferred_element_type=f32))
    x1 = x_ref[...] + g1_ref[0] * res
    x1_ref[...] = x1
    h2 = x1 * lax.rsqrt(jnp.mean(x1 * x1, axis=-1, keepdims=True) + EPS) * gf_ref[...]
    h2 = h2 * (1.0 + sc_ref[0]) + sh_ref[0]
    h2_ref[...] = h2
    qp_ref[...] = jnp.dot(h2.astype(bf16), wq_ref[...], preferred_element_type=f32)


def _outproj_call(oa, ob, x2, g1, ga, gb, wa, wb, gf, sc, sh, wq, seq, tm):
    n, d = x2.shape
    per_b = seq // tm
    tok = lambda i: (i, 0)
    bat = lambda i: (i // per_b, 0, 0)
    cst = lambda i: (0, 0)
    nq = wq.shape[1]
    return pl.pallas_call(
        _outproj_kernel,
        grid=(n // tm,),
        in_specs=[pl.BlockSpec((tm, WIDTH_A), tok),
                  pl.BlockSpec((None, N_HEADS_B, HEAD_DIM_B, tm), lambda i: (i // per_b, 0, 0, i % per_b)),
                  pl.BlockSpec((tm, d), tok), pl.BlockSpec((1, 1, d), bat),
                  pl.BlockSpec((1, WIDTH_A), cst), pl.BlockSpec((1, WIDTH_B), cst),
                  pl.BlockSpec((WIDTH_A, d), cst), pl.BlockSpec((WIDTH_B, d), cst),
                  pl.BlockSpec((1, d), cst), pl.BlockSpec((1, 1, d), bat),
                  pl.BlockSpec((1, 1, d), bat), pl.BlockSpec((d, nq), cst)],
        out_specs=[pl.BlockSpec((tm, d), tok), pl.BlockSpec((tm, d), tok),
                   pl.BlockSpec((tm, nq), tok)],
        out_shape=[jax.ShapeDtypeStruct((n, d), f32), jax.ShapeDtypeStruct((n, d), f32),
                   jax.ShapeDtypeStruct((n, nq), f32)],
        compiler_params=_params(("parallel",)),
        name="outproj",
    )(oa, ob, x2, g1, ga, gb, wa, wb, gf, sc, sh, wq)


def _argmax_rows(x, iota):
    vals = [x[j:j + 8] for j in range(0, x.shape[0], 8)]
    idxs = [iota[j:j + 8] for j in range(0, x.shape[0], 8)]
    while len(vals) > 1:
        nv, ni = [], []
        for a in range(0, len(vals) - 1, 2):
            keep = vals[a] >= vals[a + 1]
            nv.append(jnp.where(keep, vals[a], vals[a + 1]))
            ni.append(jnp.where(keep, idxs[a], idxs[a + 1]))
        if len(vals) % 2:
            nv.append(vals[-1])
            ni.append(idxs[-1])
        vals, idxs = nv, ni
    m = jnp.max(vals[0], axis=0, keepdims=True)
    pos = jnp.min(jnp.where(vals[0] == m, idxs[0], x.shape[0]), axis=0, keepdims=True)
    return m, pos


def _topk_rows(x, kk):
    iota = lax.broadcasted_iota(i32, x.shape, 0)
    vals, idxs = [], []
    for _ in range(kk):
        m, pos = _argmax_rows(x, iota)
        vals.append(m)
        idxs.append(pos)
        x = jnp.where(iota == pos, -jnp.inf, x)
    return jnp.concatenate(vals, axis=0), jnp.concatenate(idxs, axis=0)


CAND_PAIRS = [(a, b) for a in range(PEER_TOPK) for b in range(PEER_TOPK) if (a + 1) * (b + 1) <= PEER_TOPK]
CAND_ROWS = -(-len(CAND_PAIRS) // 8) * 8


def _cand_select():
    sel = np.zeros((2, CAND_ROWS, PEER_TOPK), np.float32)
    for r, (a, b) in enumerate(CAND_PAIRS):
        sel[0, r, a] = 1.0
        sel[1, r, b] = 1.0
    return sel


def _route_kernel(qp_ref, k1_ref, k2_ref, sel_ref, dup_ref, rows_ref, g_ref, *, nl):
    half = PEER_KEY_DIM // 2
    hp = lax.Precision.HIGHEST
    pick = lambda j, v: jnp.dot(sel_ref[j], v, preferred_element_type=f32, precision=hp)
    rows = lax.broadcasted_iota(i32, (CAND_ROWS, LANES), 0)

    def head(q):
        s1 = lax.dot_general(k1_ref[...], q[:, :half], NT_DIMS, preferred_element_type=f32, precision=hp)
        s2 = lax.dot_general(k2_ref[...], q[:, half:], NT_DIMS, preferred_element_type=f32, precision=hp)
        v1, i1 = _topk_rows(s1, PEER_TOPK)
        v2, i2 = _topk_rows(s2, PEER_TOPK)
        cand = jnp.where(rows < len(CAND_PAIRS), pick(0, v1) + pick(1, v2), -jnp.inf)
        cand_e = pick(0, i1.astype(f32)) * N_KEYS + pick(1, i2.astype(f32))
        best, experts = [], []
        for _ in range(PEER_TOPK):
            m, pos = _argmax_rows(cand, rows)
            hit = rows == pos
            experts.append(jnp.sum(jnp.where(hit, cand_e, 0.0), axis=0, keepdims=True))
            best.append(m)
            cand = jnp.where(hit, -jnp.inf, cand)
        best = jnp.concatenate(best, axis=0)
        e = jnp.exp(best - best[0:1])
        return jnp.concatenate(experts, axis=0), e / jnp.sum(e, axis=0, keepdims=True)

    def lane_tile(l, carry):
        start = pl.multiple_of(l * LANES, LANES)
        outs = [head(qp_ref[pl.ds(start, LANES), h * PEER_KEY_DIM:(h + 1) * PEER_KEY_DIM])
                for h in range(PEER_HEADS)]
        ids = jnp.concatenate([o[0] for o in outs], axis=0)
        gates = jnp.concatenate([o[1] for o in outs], axis=0)
        rows_ref[pl.ds(start, LANES), :] = (ids.T * ROWS_PER_EXPERT).astype(i32)
        g_ref[pl.ds(start, LANES), :] = jnp.dot(dup_ref[...], gates, preferred_element_type=f32,
                                                precision=hp).T
        return carry

    lax.fori_loop(0, nl, lane_tile, 0)


def _route_call(qp, k1, k2, tr):
    n = qp.shape[0]
    sel = jnp.asarray(_cand_select())
    dup = jnp.asarray(np.repeat(np.eye(PEER_SLOTS, dtype=np.float32), 2, axis=0))
    kern = functools.partial(_route_kernel, nl=tr // LANES)
    cst = lambda i: (0, 0)
    return pl.pallas_call(
        kern,
        grid=(n // tr,),
        in_specs=[pl.BlockSpec((tr, PEER_HEADS * PEER_KEY_DIM), lambda i: (i, 0)),
                  pl.BlockSpec(k1.shape, cst),
                  pl.BlockSpec(k2.shape, cst),
                  pl.BlockSpec(sel.shape, lambda i: (0, 0, 0)),
                  pl.BlockSpec(dup.shape, cst)],
        out_specs=[pl.BlockSpec((tr, PEER_SLOTS), lambda i: (i, 0)),
                   pl.BlockSpec((tr, 2 * PEER_SLOTS), lambda i: (i, 0))],
        out_shape=[jax.ShapeDtypeStruct((n, PEER_SLOTS), i32),
                   jax.ShapeDtypeStruct((n, 2 * PEER_SLOTS), f32)],
        compiler_params=_params(("parallel",)),
        name="route",
    )(qp, k1, k2, sel, dup)


ROWS_PER_EXPERT = 4
TILE_STRIDE = 136
TOK_UNROLL = 64
HALF = 512


def _pack_kernel(t_ref, o_ref):
    te = t_ref.shape[0]
    bits = lax.bitcast_convert_type(t_ref[...].astype(bf16).astype(f32), i32)
    word = (bits[:, HALF:] & jnp.int32(-65536)) | lax.shift_right_logical(bits[:, :HALF], 16)
    for r in range(ROWS_PER_EXPERT):
        o_ref[pl.ds(r, te, stride=ROWS_PER_EXPERT), :] = word[:, r * LANES:(r + 1) * LANES]


def _pack_table(tab, te=256):
    e, d = tab.shape
    return pl.pallas_call(
        _pack_kernel,
        grid=(e // te,),
        in_specs=[pl.BlockSpec((te, d), lambda i: (i, 0))],
        out_specs=pl.BlockSpec((te * ROWS_PER_EXPERT, LANES), lambda i: (i, 0)),
        out_shape=jax.ShapeDtypeStruct((e * ROWS_PER_EXPERT, LANES), i32),
        compiler_params=_params(("parallel",)),
        name="pack",
    )(tab)


def _gather_tile(tab_ref, rows_ref, t, tile_ref):
    tok_rows = rows_ref.at[t]
    for k in range(PEER_SLOTS):
        row = pl.multiple_of(tok_rows[k], ROWS_PER_EXPERT)
        tile_ref[pl.ds(k, ROWS_PER_EXPERT, stride=TILE_STRIDE), :] = tab_ref[pl.ds(row, ROWS_PER_EXPERT), :]
    chunks = [pltpu.bitcast(tile_ref[j * TILE_STRIDE:j * TILE_STRIDE + PEER_SLOTS, :], bf16)
              for j in range(ROWS_PER_EXPERT)]
    return jnp.concatenate(chunks, axis=1)


def _for_tokens(tab_ref, rows_ref, tile_scr, base, per_token):
    for tl in range(TOK_UNROLL):
        per_token(tl, base + tl, _gather_tile(tab_ref, rows_ref, base + tl, tile_scr.at[tl % 2]))


def _peer_u_kernel(ids_ref, h_ref, g_ref, u_ref, o_ref, tile_scr, act_scr, *, tt):
    even = lax.broadcasted_iota(i32, (1, 2 * PEER_SLOTS), 1) % 2 == 0

    def group(gi, carry):
        base = pl.multiple_of(gi * TOK_UNROLL, TOK_UNROLL)
        rows = []

        def per_token(tl, t, r):
            row = h_ref[pl.ds(t, 1), :]
            h = jnp.concatenate([row[:, :HALF], row[:, HALF:]], axis=0)
            major = h.astype(bf16)
            minor = (h - major.astype(f32)).astype(bf16)
            out = lax.dot_general(jnp.concatenate([major, minor], axis=0), r, NT_DIMS,
                                  preferred_element_type=f32)
            rows.append(jnp.where(even, out[0:1] + out[2:3], out[1:2] + out[3:4]))

        _for_tokens(u_ref, ids_ref, tile_scr, base, per_token)
        act_scr[pl.ds(base, TOK_UNROLL), :] = jnp.concatenate(rows, axis=0)
        return carry

    lax.fori_loop(0, tt // TOK_UNROLL, group, 0)
    part = act_scr[...]
    act = part + jnp.where(even, pltpu.roll(part, 2 * PEER_SLOTS - 1, 1), pltpu.roll(part, 1, 1))
    gelu = 0.5 * act * (1.0 + lax.erf(act * (2.0 ** -0.5)))
    o_ref[...] = g_ref[...] * gelu


def _peer_u_call(ids, h3, g2, u_packed, tt):
    n = ids.shape[0]
    kern = functools.partial(_peer_u_kernel, tt=tt)
    return pl.pallas_call(
        kern,
        grid=(n // tt,),
        in_specs=[pl.BlockSpec((tt, PEER_SLOTS), lambda i: (i, 0), memory_space=pltpu.SMEM),
                  pl.BlockSpec((tt, 2 * HALF), lambda i: (i, 0)),
                  pl.BlockSpec((tt, 2 * PEER_SLOTS), lambda i: (i, 0)),
                  pl.BlockSpec(u_packed.shape, lambda i: (0, 0), pipeline_mode=pl.Buffered(1))],
        out_specs=pl.BlockSpec((tt, 2 * PEER_SLOTS), lambda i: (i, 0)),
        out_shape=jax.ShapeDtypeStruct((n, 2 * PEER_SLOTS), f32),
        scratch_shapes=[pltpu.VMEM((2, ROWS_PER_EXPERT * TILE_STRIDE, LANES), i32),
                        pltpu.VMEM((tt, 2 * PEER_SLOTS), f32)],
        compiler_params=_params(("arbitrary",)),
        name="peer_u",
    )(ids, h3, g2, u_packed)


def _peer_v_kernel(ids_ref, w_ref, v_ref, o_ref, tile_scr, *, tt):
    even = lax.broadcasted_iota(i32, (1, 2 * PEER_SLOTS), 1) % 2 == 0

    def group(gi, carry):
        base = pl.multiple_of(gi * TOK_UNROLL, TOK_UNROLL)
        wts = w_ref[pl.ds(base, TOK_UNROLL), :]
        rows = []

        def per_token(tl, t, r):
            w = wts[tl:tl + 1]
            lhs = jnp.concatenate([jnp.where(even, w, 0.0), jnp.where(even, 0.0, w)], axis=0)
            out = jnp.dot(lhs.astype(bf16), r, preferred_element_type=f32)
            rows.append(jnp.concatenate([out[0:1], out[1:2]], axis=1))
            if len(rows) == 8:
                o_ref[pl.ds(pl.multiple_of(base + tl - 7, 8), 8), :] = jnp.concatenate(rows, axis=0)
                rows.clear()

        _for_tokens(v_ref, ids_ref, tile_scr, base, per_token)
        return carry

    lax.fori_loop(0, tt // TOK_UNROLL, group, 0)


def _peer_v_call(ids, wgt2, v_packed, tt):
    n = ids.shape[0]
    kern = functools.partial(_peer_v_kernel, tt=tt)
    return pl.pallas_call(
        kern,
        grid=(n // tt,),
        in_specs=[pl.BlockSpec((tt, PEER_SLOTS), lambda i: (i, 0), memory_space=pltpu.SMEM),
                  pl.BlockSpec((tt, 2 * PEER_SLOTS), lambda i: (i, 0)),
                  pl.BlockSpec(v_packed.shape, lambda i: (0, 0), pipeline_mode=pl.Buffered(1))],
        out_specs=pl.BlockSpec((tt, 2 * HALF), lambda i: (i, 0)),
        out_shape=jax.ShapeDtypeStruct((n, 2 * HALF), f32),
        scratch_shapes=[pltpu.VMEM((2, ROWS_PER_EXPERT * TILE_STRIDE, LANES), i32)],
        compiler_params=_params(("arbitrary",)),
        name="peer_v",
    )(ids, wgt2, v_packed)


def _final_kernel(x1_ref, p_ref, g2_ref, gf_ref, o_ref):
    x2 = x1_ref[...] + g2_ref[0] * p_ref[...]
    o_ref[...] = x2 * lax.rsqrt(jnp.mean(x2 * x2, axis=-1, keepdims=True) + EPS) * gf_ref[...]


def _final_call(x1, peer, g2, gf, seq, tm):
    n, d = x1.shape
    per_b = seq // tm
    tok = lambda i: (i, 0)
    return pl.pallas_call(
        _final_kernel,
        grid=(n // tm,),
        in_specs=[pl.BlockSpec((tm, d), tok), pl.BlockSpec((tm, d), tok),
                  pl.BlockSpec((1, 1, d), lambda i: (i // per_b, 0, 0)),
                  pl.BlockSpec((1, d), lambda i: (0, 0))],
        out_specs=pl.BlockSpec((tm, d), tok),
        out_shape=jax.ShapeDtypeStruct((n, d), f32),
        compiler_params=_params(("parallel",)),
        name="final",
    )(x1, peer, g2, gf)


def _rope_tables(positions):
    def tab(rot):
        inv = ROPE_THETA ** (-jnp.arange(0, rot, 2, dtype=f32) / rot)
        ang = positions.astype(f32)[..., None] * inv
        return jnp.cos(ang), jnp.sin(ang)

    cos_a, sin_a = tab(ROPE_DIM_A)
    cos_i, sin_i = tab(IDX_ROPE_DIM)
    b, s = positions.shape
    one = jnp.ones((b, s, 1), f32)
    zero = jnp.zeros((b, s, 4), f32)
    trig = jnp.concatenate([cos_a, cos_i, one, zero[..., :3], sin_a, sin_i, zero], axis=-1)
    return trig.reshape(b * s, TRIG)


def kernel(x, c, positions, w_ada, b_ada, g_mix, w_in, g_kv, w_uk, w_uv, b_forget,
           g_out_a, g_out_b, w_out, g_ffn, w_peer_q, peer_keys1, peer_keys2, peer_u,
           peer_v, g_final):
    b, s, d = x.shape
    n = b * s
    assert w_ada.shape[0] == 1, "single layer supported"
    tm = min(TOKEN_TILE, s)
    x2 = x.reshape(n, d)

    mod = _mod_call(c, w_ada[0], b_ada[0])
    mod = mod.reshape(b, 6, 1, d)
    shift1, scale1, gate1, shift2, scale2, gate2 = [mod[:, j] for j in range(6)]

    perm = _in_perm()
    w_in_r = jnp.where((perm >= 0)[None, :], w_in[0][:, np.maximum(perm, 0)], 0.0).astype(bf16)
    proj, kcat, ik, latt, vbt = _inproj_call(x2, scale1, shift1, g_mix[0].reshape(1, d), w_in_r,
                                             _rope_tables(positions), g_kv[0].reshape(1, KV_RANK), s, tm)

    wuk = w_uk[0].transpose(1, 0, 2).astype(bf16)
    wuvt = w_uv[0].transpose(1, 2, 0).astype(bf16)
    o_a = _dsa_call(proj, kcat.reshape(b, s, KCAT), latt, ik.reshape(b, s, LANES), wuk, wuvt,
                    b, s, tq=DSA_QUERY_TILE, tk=tm)

    cum = _cum_call(proj, b_forget[0], b, s)
    o_bt = _fox_call(proj, vbt, cum[:, :, None, :], cum.reshape(b, N_HEADS_B, s // tm, 1, tm),
                     tq=min(FOX_QUERY_TILE, s), ts=tm)

    wo = w_out[0].astype(bf16)
    x1, h2, qp = _outproj_call(o_a, o_bt, x2, gate1, g_out_a[0].reshape(1, WIDTH_A),
                               g_out_b[0].reshape(1, WIDTH_B), wo[:WIDTH_A], wo[WIDTH_A:],
                               g_ffn[0].reshape(1, d), scale2, shift2,
                               w_peer_q[0].astype(bf16), s, tm)

    ids, gates2 = _route_call(qp, peer_keys1[0], peer_keys2[0], tr=min(ROUTE_TILE, n))
    wgt2 = _peer_u_call(ids, h2, gates2, _pack_table(peer_u[0]), PEER_TILE)
    peer = _peer_v_call(ids, wgt2, _pack_table(peer_v[0]), PEER_TILE)

    out = _final_call(x1, peer, gate2, g_final.reshape(1, d), s, tm)
    return out.reshape(b, s, d)
```

```python
import functools

import jax
import jax.numpy as jnp
import numpy as np
from jax import lax
from jax.experimental import pallas as pl
from jax.experimental.pallas import tpu as pltpu

f32 = jnp.float32
bf16 = jnp.bfloat16
i32 = jnp.int32

N_HEADS_A = 8
HEAD_DIM_A = 64
ROPE_DIM_A = 16
NOPE_DIM_A = HEAD_DIM_A - ROPE_DIM_A
KV_RANK = 128
IDX_HEADS = 8
IDX_DIM = 32
IDX_ROPE_DIM = 8
TOPK_MAX = 256
N_HEADS_B = 8
HEAD_DIM_B = 64
WIDTH_A = N_HEADS_A * HEAD_DIM_A
WIDTH_B = N_HEADS_B * HEAD_DIM_B
ROPE_THETA = 500000.0
PEER_HEADS = 8
N_KEYS = 128
PEER_KEY_DIM = 128
PEER_TOPK = 16
PEER_SLOTS = PEER_HEADS * PEER_TOPK
EPS = 1e-6
IN_SIZES = (WIDTH_A, KV_RANK, ROPE_DIM_A, IDX_HEADS * IDX_DIM, IDX_DIM, IDX_HEADS,
            WIDTH_B, WIDTH_B, WIDTH_B, N_HEADS_B)

LANES = 128
TOKEN_TILE = 512
DSA_QUERY_TILE = 128
FOX_QUERY_TILE = 1024
ROUTE_TILE = 1024
PEER_TILE = 64
NEG_BIG = -1e30
INT_MIN = -2147483648
VMEM_LIMIT = 56 * 1024 * 1024

C_R1 = 0
C_R2 = 128
C_LAT = 256
C_QN = 384
C_MISC = 768
C_QB = 1024
C_KB = 1536
C_VB = 2048
C_TOTAL = 2560

NT_DIMS = (((1,), (1,)), ((), ()))


def _in_perm():
    offs = np.cumsum((0,) + IN_SIZES)
    o_qa, o_lat, o_kr, o_iq, o_ik, o_iw, o_qb, o_kb, o_vb, o_fb = offs[:10]
    perm = -np.ones((C_TOTAL,), np.int64)
    ha, hi = ROPE_DIM_A // 2, IDX_ROPE_DIM // 2
    for h in range(N_HEADS_A):
        for j in range(ha):
            perm[C_R1 + h * ha + j] = o_qa + h * HEAD_DIM_A + j
            perm[C_R1 + 64 + h * ha + j] = o_qa + h * HEAD_DIM_A + ha + j
        for j in range(NOPE_DIM_A):
            perm[C_QN + h * NOPE_DIM_A + j] = o_qa + h * HEAD_DIM_A + ROPE_DIM_A + j
    for h in range(IDX_HEADS):
        for j in range(hi):
            perm[C_R2 + j * IDX_HEADS + h] = o_iq + h * IDX_DIM + j
            perm[C_R2 + 64 + j * IDX_HEADS + h] = o_iq + h * IDX_DIM + hi + j
        for j in range(IDX_DIM - IDX_ROPE_DIM):
            perm[C_MISC + j * IDX_HEADS + h] = o_iq + h * IDX_DIM + IDX_ROPE_DIM + j
    for j in range(ha):
        perm[C_R2 + 32 + j] = o_kr + j
        perm[C_R2 + 64 + 32 + j] = o_kr + ha + j
    for j in range(hi):
        perm[C_R2 + 40 + j] = o_ik + j
        perm[C_R2 + 64 + 40 + j] = o_ik + hi + j
    for j in range(IDX_DIM - IDX_ROPE_DIM):
        perm[C_MISC + 192 + j] = o_ik + IDX_ROPE_DIM + j
    for j in range(IDX_HEADS):
        perm[C_MISC + 216 + j] = o_iw + j
    for j in range(N_HEADS_B):
        perm[C_MISC + 224 + j] = o_fb + j
    perm[C_LAT:C_LAT + KV_RANK] = o_lat + np.arange(KV_RANK)
    perm[C_QB:C_QB + WIDTH_B] = o_qb + np.arange(WIDTH_B)
    perm[C_KB:C_KB + WIDTH_B] = o_kb + np.arange(WIDTH_B)
    perm[C_VB:C_VB + WIDTH_B] = o_vb + np.arange(WIDTH_B)
    return perm


def _params(sem):
    return pltpu.CompilerParams(dimension_semantics=sem, vmem_limit_bytes=VMEM_LIMIT)


def _loop_pairs(lo, hi, body, init):
    pairs = (hi - lo) // 2

    def two(i, carry):
        return body(lo + 2 * i + 1, body(lo + 2 * i, carry))

    carry = lax.fori_loop(0, pairs, two, init)
    return lax.fori_loop(lo + 2 * pairs, hi, body, carry)


def _mod_kernel(c_ref, w_ref, b_ref, o_ref):
    c = c_ref[...]
    ca = c * jax.nn.sigmoid(c)
    o_ref[...] = jnp.dot(ca, w_ref[...], preferred_element_type=f32,
                         precision=lax.Precision.HIGHEST) + b_ref[...]


def _mod_call(c, w, b):
    bsz, d = c.shape
    n = w.shape[1]
    return pl.pallas_call(
        _mod_kernel,
        grid=(n // d,),
        in_specs=[pl.BlockSpec((bsz, d), lambda j: (0, 0)),
                  pl.BlockSpec((d, d), lambda j: (0, j)),
                  pl.BlockSpec((1, d), lambda j: (0, j))],
        out_specs=pl.BlockSpec((bsz, d), lambda j: (0, j)),
        out_shape=jax.ShapeDtypeStruct((bsz, n), f32),
        compiler_params=_params(("arbitrary",)),
        name="mod",
    )(c, w, b.reshape(1, n))


KCAT = 256
ONES_ROWS = 16


def _with_ones_rows(vt):
    lead = vt.shape[:-2]
    t = vt.shape[-1]
    ones = jnp.ones(lead + (1, t), vt.dtype)
    zeros = jnp.zeros(lead + (ONES_ROWS - 1, t), vt.dtype)
    return jnp.concatenate([vt, ones, zeros], axis=-2)


def _lanes_from(lane, pieces):
    out = jnp.zeros(lane.shape, f32)
    for end, src, start in reversed(pieces):
        begin = max([e for e, _, _ in pieces if e < end], default=0)
        out = jnp.where(lane < end, pltpu.roll(src, (begin - start) % LANES, 1), out)
    return out


TRIG = 32


def _trig_expand():
    e = np.zeros((TRIG, 4 * LANES), np.float32)
    cos_a, cos_i, one, sin_a, sin_i = 0, 8, 12, 16, 24
    for lane in range(LANES):
        l, sign = lane % 64, (-1.0 if lane < 64 else 1.0)
        e[cos_a + l % 8, lane] = 1.0
        e[sin_a + l % 8, LANES + lane] = sign
        if l < 32:
            e[cos_i + l // IDX_HEADS, 2 * LANES + lane] = 1.0
            e[sin_i + l // IDX_HEADS, 3 * LANES + lane] = sign
        elif l < 40:
            e[cos_a + l - 32, 2 * LANES + lane] = 1.0
            e[sin_a + l - 32, 3 * LANES + lane] = sign
        elif l < 44:
            e[cos_i + l - 40, 2 * LANES + lane] = 1.0
            e[sin_i + l - 40, 3 * LANES + lane] = sign
        else:
            e[one, 2 * LANES + lane] = 1.0
    return e


def _inproj_kernel(x_ref, sc_ref, sh_ref, g_ref, w_ref, trig_ref, exp_ref,
                   gkv_ref, o_ref, kcat_ref, ik_ref, latt_ref, vbt_ref):
    x = x_ref[...]
    ms = jnp.mean(x * x, axis=-1, keepdims=True)
    h = x * lax.rsqrt(ms + EPS) * g_ref[...]
    h = h * (1.0 + sc_ref[0]) + sh_ref[0]
    p = jnp.dot(h.astype(bf16), w_ref[...], preferred_element_type=f32)
    tab = jnp.dot(trig_ref[...], exp_ref[...], preferred_element_type=f32,
                  precision=lax.Precision.HIGHEST)
    r1 = p[:, C_R1:C_R1 + LANES]
    o_ref[:, C_R1:C_R1 + LANES] = (r1 * tab[:, 0:LANES]
                                   + pltpu.roll(r1, 64, 1) * tab[:, LANES:2 * LANES])
    r2 = p[:, C_R2:C_R2 + LANES]
    r2 = r2 * tab[:, 2 * LANES:3 * LANES] + pltpu.roll(r2, 64, 1) * tab[:, 3 * LANES:]
    o_ref[:, C_R2:C_R2 + LANES] = r2
    lat = p[:, C_LAT:C_LAT + KV_RANK]
    lms = jnp.mean(lat * lat, axis=-1, keepdims=True)
    lat = lat * lax.rsqrt(lms + EPS) * gkv_ref[...]
    o_ref[:, C_LAT:C_LAT + KV_RANK] = lat
    o_ref[:, C_QN:] = p[:, C_QN:]

    lane = lax.broadcasted_iota(i32, r2.shape, 1)
    kr = _lanes_from(lane, [(8, r2, 32), (16, r2, 96)])
    kcat_ref[...] = jnp.concatenate([lat, kr], axis=1).astype(bf16)
    misc_hi = p[:, C_MISC + LANES:C_MISC + 2 * LANES]
    ik_ref[...] = _lanes_from(lane, [(4, r2, 40), (8, r2, 104), (32, misc_hi, 64)]).astype(bf16)
    latt_ref[...] = _with_ones_rows(lat.T.astype(bf16))
    vt = p[:, C_VB:C_VB + WIDTH_B].T.astype(bf16)
    vbt_ref[...] = _with_ones_rows(vt.reshape(N_HEADS_B, HEAD_DIM_B, vt.shape[1]))


def _inproj_call(x2, sc, sh, g, w, trig, gkv, seq, tm):
    n, d = x2.shape
    per_b = seq // tm
    bsz = n // seq
    tok = lambda i: (i, 0)
    bat = lambda i: (i // per_b, 0, 0)
    cst = lambda i: (0, 0)
    return pl.pallas_call(
        _inproj_kernel,
        grid=(n // tm,),
        in_specs=[pl.BlockSpec((tm, d), tok),
                  pl.BlockSpec((1, 1, d), bat),
                  pl.BlockSpec((1, 1, d), bat),
                  pl.BlockSpec((1, d), cst),
                  pl.BlockSpec((d, C_TOTAL), cst),
                  pl.BlockSpec((tm, TRIG), tok),
                  pl.BlockSpec((TRIG, 4 * LANES), cst),
                  pl.BlockSpec((1, KV_RANK), cst)],
        out_specs=[pl.BlockSpec((tm, C_TOTAL), tok),
                   pl.BlockSpec((tm, KCAT), tok),
                   pl.BlockSpec((tm, LANES), tok),
                   pl.BlockSpec((None, None, KV_RANK + ONES_ROWS, tm),
                                lambda i: (i // per_b, i % per_b, 0, 0)),
                   pl.BlockSpec((None, N_HEADS_B, None, HEAD_DIM_B + ONES_ROWS, tm),
                                lambda i: (i // per_b, 0, i % per_b, 0, 0))],
        out_shape=[jax.ShapeDtypeStruct((n, C_TOTAL), f32),
                   jax.ShapeDtypeStruct((n, KCAT), bf16),
                   jax.ShapeDtypeStruct((n, LANES), bf16),
                   jax.ShapeDtypeStruct((bsz, per_b, KV_RANK + ONES_ROWS, tm), bf16),
                   jax.ShapeDtypeStruct((bsz, N_HEADS_B, per_b, HEAD_DIM_B + ONES_ROWS, tm), bf16)],
        compiler_params=_params(("parallel",)),
        name="inproj",
    )(x2, sc, sh, g, w, trig, jnp.asarray(_trig_expand()), gkv)


LOG2E = 1.4426950408889634


def _sortable(x):
    bits = lax.bitcast_convert_type(x, i32)
    return bits ^ ((bits >> 31) & 0x7FFFFFFF)


def _dsa_kernel(r1_ref, r2_ref, qn_ref, misc_ref, kcat_ref, latt_ref, ik_ref, wuk_ref, wuvt_ref,
                o_ref, key_scr, iq_scr, m_scr, acc_scr, *, tq, tk, topk, nbits, seq):
    nh = N_HEADS_A
    qi = pl.program_id(1)
    nc = ((qi + 1) * tq + tk - 1) // tk
    qpos = qi * tq + lax.broadcasted_iota(i32, (tk, tq), 1)
    krow = lax.broadcasted_iota(i32, (tk, tq), 0)

    r2t = r2_ref[...].T
    mt = misc_ref[...].T
    iq_scr[...] = jnp.concatenate([r2t[0:32], r2t[64:96], mt[0:192]], axis=0)
    iqt = jnp.concatenate([iq_scr[pl.ds(h, IDX_DIM, stride=IDX_HEADS), :] for h in range(IDX_HEADS)],
                          axis=1)
    iqt = jnp.concatenate([iqt, jnp.zeros((LANES - IDX_DIM, IDX_HEADS * tq), f32)], axis=0).astype(bf16)
    iwt = mt[216:224] * (IDX_DIM ** -0.5 * IDX_HEADS ** -0.5)

    def score_chunk(c, carry):
        start = pl.multiple_of(c * tk, tk)
        d = jnp.dot(ik_ref[pl.ds(start, tk), :], iqt, preferred_element_type=f32)
        sc = jnp.zeros((tk, tq), f32)
        for h in range(IDX_HEADS):
            sc = sc + jnp.maximum(d[:, h * tq:(h + 1) * tq], 0.0) * iwt[h:h + 1, :]
        sc = jnp.where(start + krow <= qpos, sc, -jnp.inf)
        key_scr[c] = _sortable(sc)
        return carry

    _loop_pairs(0, nc, score_chunk, 0)

    def count(pred):
        def body(c, acc):
            m = pred(key_scr[c], c * tk + krow)
            return acc + jnp.sum(m.reshape(tk // 8, 8, tq), axis=0)
        acc = _loop_pairs(0, nc, body, jnp.zeros((8, tq), i32))
        return jnp.sum(acc, axis=0, keepdims=True)

    def bit_body(i, thr):
        cand = thr + lax.shift_left(jnp.int32(1), 31 - i)
        cnt = count(lambda k, col: jnp.where(k >= cand, 1, 0))
        return jnp.where(cnt >= topk, cand, thr)

    thr = lax.fori_loop(0, 32, bit_body, jnp.full((1, tq), INT_MIN, i32))
    need = topk - count(lambda k, col: jnp.where(k > thr, 1, 0))
    n_eq = count(lambda k, col: jnp.where(k == thr, 1, 0))

    def tie_search():
        def tie_body(i, last):
            cand = last + lax.shift_left(jnp.int32(1), nbits - 1 - i)
            below = count(lambda k, col: jnp.where(k == thr, jnp.where(col < cand, 1, 0), 0))
            return jnp.where(below < need, cand, last)
        return lax.fori_loop(0, nbits, tie_body, jnp.zeros((1, tq), i32))

    ambiguous = jnp.max(jnp.where(n_eq > need, 1, 0)) > 0
    last = lax.cond(ambiguous, tie_search, lambda: jnp.full((1, tq), seq, i32))

    scale = HEAD_DIM_A ** -0.5 * LOG2E
    r1t = r1_ref[...].T
    qnt = qn_ref[...].T.astype(bf16)
    pad = jnp.zeros((KCAT - KV_RANK - ROPE_DIM_A, tq), f32)
    cols = []
    for h in range(nh):
        ql = jnp.dot(wuk_ref[h], qnt[h * NOPE_DIM_A:(h + 1) * NOPE_DIM_A], preferred_element_type=f32)
        cols.append(jnp.concatenate([ql, r1t[h * 8:(h + 1) * 8], r1t[64 + h * 8:64 + (h + 1) * 8], pad],
                                    axis=0))
    qcat = (jnp.concatenate(cols, axis=1) * scale).astype(bf16)

    m_scr[...] = jnp.full(m_scr.shape, NEG_BIG, f32)
    acc_scr[...] = jnp.zeros(acc_scr.shape, f32)

    def attend_chunk(c, carry):
        start = pl.multiple_of(c * tk, tk)
        s = jnp.dot(kcat_ref[pl.ds(start, tk), :], qcat, preferred_element_type=f32)
        k = key_scr[c]
        col = start + krow
        tie = jnp.where(k == thr, jnp.where(col <= last, 0.0, NEG_BIG), NEG_BIG)
        bias = jnp.where(col <= qpos, jnp.where(k > thr, 0.0, tie), NEG_BIG)
        s = s + jnp.concatenate([bias] * nh, axis=1)
        m_old = m_scr[...]
        m_new = jnp.maximum(m_old, jnp.max(s, axis=0, keepdims=True))
        p = jnp.exp2(s - m_new)
        alpha = jnp.exp2(m_old - m_new)
        acc_scr[...] = alpha * acc_scr[...] + jnp.dot(latt_ref[c], p.astype(bf16),
                                                      preferred_element_type=f32)
        m_scr[...] = m_new
        return carry

    _loop_pairs(0, nc, attend_chunk, 0)

    ot = (acc_scr[0:KV_RANK, :] / acc_scr[KV_RANK:KV_RANK + 1, :]).astype(bf16)
    outs = [jnp.dot(wuvt_ref[h], ot[:, h * tq:(h + 1) * tq], preferred_element_type=f32)
            for h in range(nh)]
    o_ref[...] = jnp.concatenate(outs, axis=0).T


def _dsa_call(proj, kcat, latt, ik, wuk, wuvt, bsz, seq, tq, tk):
    nh = N_HEADS_A
    topk = min(TOPK_MAX, seq // 4)
    nbits = max(1, (seq - 1).bit_length())
    nq = seq // tq
    kern = functools.partial(_dsa_kernel, tq=tq, tk=tk, topk=topk, nbits=nbits, seq=seq)
    qcol = lambda width, blk: pl.BlockSpec((tq, width), lambda b, i: (b * nq + i, blk))
    return pl.pallas_call(
        kern,
        grid=(bsz, nq),
        in_specs=[qcol(LANES, C_R1 // LANES), qcol(LANES, C_R2 // LANES),
                  qcol(C_MISC - C_QN, C_QN // (C_MISC - C_QN)), qcol(256, C_MISC // 256),
                  pl.BlockSpec((None, seq, KCAT), lambda b, i: (b, 0, 0)),
                  pl.BlockSpec((None, seq // tk, KV_RANK + ONES_ROWS, tk), lambda b, i: (b, 0, 0, 0)),
                  pl.BlockSpec((None, seq, LANES), lambda b, i: (b, 0, 0)),
                  pl.BlockSpec(wuk.shape, lambda b, i: (0, 0, 0)),
                  pl.BlockSpec(wuvt.shape, lambda b, i: (0, 0, 0))],
        out_specs=pl.BlockSpec((tq, WIDTH_A), lambda b, i: (b * nq + i, 0)),
        out_shape=jax.ShapeDtypeStruct((bsz * seq, WIDTH_A), f32),
        scratch_shapes=[pltpu.VMEM((seq // tk, tk, tq), i32),
                        pltpu.VMEM((IDX_HEADS * IDX_DIM, tq), f32),
                        pltpu.VMEM((1, nh * tq), f32),
                        pltpu.VMEM((KV_RANK + ONES_ROWS, nh * tq), f32)],
        compiler_params=_params(("parallel", "arbitrary")),
        name="dsa",
    )(proj, proj, proj, proj, kcat, latt, ik, wuk, wuvt)


C_FB = C_MISC + 224
CUM_CHUNK = 512


def _cum_kernel(p_ref, b_ref, o_ref):
    seq = p_ref.shape[0]
    lane0 = C_FB % LANES
    x = p_ref[...].T[lane0:lane0 + N_HEADS_B] + b_ref[...]
    ls = jnp.minimum(x, 0.0) - jnp.log(1.0 + jnp.exp(-jnp.abs(x)))
    row = lax.broadcasted_iota(i32, (CUM_CHUNK, CUM_CHUNK), 0)
    col = lax.broadcasted_iota(i32, (CUM_CHUNK, CUM_CHUNK), 1)
    upper = jnp.where(row <= col, 1.0, 0.0)
    carry = jnp.zeros((N_HEADS_B, 1), f32)
    for c in range(seq // CUM_CHUNK):
        part = ls[:, c * CUM_CHUNK:(c + 1) * CUM_CHUNK]
        cs = jnp.dot(part, upper, preferred_element_type=f32, precision=lax.Precision.HIGHEST) + carry
        o_ref[:, c * CUM_CHUNK:(c + 1) * CUM_CHUNK] = cs * LOG2E
        carry = cs[:, CUM_CHUNK - 1:CUM_CHUNK]


def _cum_call(proj, b_forget, bsz, seq):
    assert seq % CUM_CHUNK == 0
    return pl.pallas_call(
        _cum_kernel,
        grid=(bsz,),
        in_specs=[pl.BlockSpec((seq, LANES), lambda b: (b, C_FB // LANES)),
                  pl.BlockSpec((N_HEADS_B, 1), lambda b: (0, 0))],
        out_specs=pl.BlockSpec((None, N_HEADS_B, seq), lambda b: (b, 0, 0)),
        out_shape=jax.ShapeDtypeStruct((bsz, N_HEADS_B, seq), f32),
        compiler_params=_params(("parallel",)),
        name="cum",
    )(proj, b_forget.reshape(N_HEADS_B, 1))


def _fox_kernel(q_ref, k_ref, vt_ref, cq_ref, ck_ref, o_ref, s_scr, *, tq, ts):
    qi = pl.program_id(2)
    nc = ((qi + 1) * tq + ts - 1) // ts
    mine = lax.broadcasted_iota(i32, (LANES, tq), 0) // HEAD_DIM_B == pl.program_id(1) % 2
    qt = jnp.where(mine, q_ref[...].T * (HEAD_DIM_B ** -0.5 * LOG2E), 0.0).astype(bf16)
    cq = cq_ref[...]
    qpos = qi * tq + lax.broadcasted_iota(i32, (ts, tq), 1)
    krow = lax.broadcasted_iota(i32, (ts, tq), 0)

    def score(c, m, masked):
        start = pl.multiple_of(c * ts, ts)
        s = jnp.dot(k_ref[pl.ds(start, ts), :].astype(bf16), qt, preferred_element_type=f32)
        ck = jnp.broadcast_to(ck_ref[c], (LANES, ts)).T
        s = s + cq - jnp.concatenate([ck] * (tq // LANES), axis=1)
        if masked:
            s = jnp.where(start + krow <= qpos, s, NEG_BIG)
        s_scr[c] = s
        return jnp.maximum(m, jnp.max(s, axis=0, keepdims=True))

    nfull = (qi * tq) // ts
    m = _loop_pairs(0, nfull, lambda c, m: score(c, m, False), jnp.full((1, tq), NEG_BIG, f32))
    m = lax.fori_loop(nfull, nc, lambda c, m: score(c, m, True), m)

    def attend(c, acc):
        p = jnp.exp2(s_scr[c] - m)
        return acc + jnp.dot(vt_ref[c], p.astype(bf16), preferred_element_type=f32)

    acc = _loop_pairs(0, nc, attend, jnp.zeros((HEAD_DIM_B + ONES_ROWS, tq), f32))
    o_ref[...] = acc[0:HEAD_DIM_B] / acc[HEAD_DIM_B:HEAD_DIM_B + 1]


def _fox_call(proj, vt, cq, ck, tq, ts):
    bsz, nh, _, _, _ = vt.shape
    hd = HEAD_DIM_B
    seq = proj.shape[0] // bsz
    nq = seq // tq
    kern = functools.partial(_fox_kernel, tq=tq, ts=ts)
    return pl.pallas_call(
        kern,
        grid=(bsz, nh, nq),
        in_specs=[pl.BlockSpec((tq, LANES), lambda b, h, i: (b * nq + i, C_QB // LANES + h // 2)),
                  pl.BlockSpec((seq, LANES), lambda b, h, i: (b, C_KB // LANES + h // 2)),
                  pl.BlockSpec((None, None, seq // ts, hd + ONES_ROWS, ts), lambda b, h, i: (b, h, 0, 0, 0)),
                  pl.BlockSpec((None, None, 1, tq), lambda b, h, i: (b, h, 0, i)),
                  pl.BlockSpec((None, None, seq // ts, 1, ts), lambda b, h, i: (b, h, 0, 0, 0))],
        out_specs=pl.BlockSpec((None, None, hd, tq), lambda b, h, i: (b, h, 0, i)),
        out_shape=jax.ShapeDtypeStruct((bsz, nh, hd, seq), f32),
        scratch_shapes=[pltpu.VMEM((seq // ts, ts, tq), f32)],
        compiler_params=_params(("parallel", "parallel", "arbitrary")),
        name="fox",
    )(proj, proj, vt, cq, ck)


def _outproj_kernel(oa_ref, ob_ref, x_ref, g1_ref, ga_ref, gb_ref, wa_ref, wb_ref,
                    gf_ref, sc_ref, sh_ref, wq_ref, x1_ref, h2_ref, qp_ref):
    oa = oa_ref[...]
    ob = ob_ref[...].reshape(WIDTH_B, oa.shape[0]).T
    na = oa * lax.rsqrt(jnp.mean(oa * oa, axis=-1, keepdims=True) + EPS) * ga_ref[...]
    nb = ob * lax.rsqrt(jnp.mean(ob * ob, axis=-1, keepdims=True) + EPS) * gb_ref[...]
    res = (jnp.dot(na.astype(bf16), wa_ref[...], preferred_element_type=f32)
           + jnp.dot(nb.astype(bf16), wb_ref[...], preferred_element_type=f32))
    x1 = x_ref[...] + g1_ref[0] * res
    x1_ref[...] = x1
    h2 = x1 * lax.rsqrt(jnp.mean(x1 * x1, axis=-1, keepdims=True) + EPS) * gf_ref[...]
    h2 = h2 * (1.0 + sc_ref[0]) + sh_ref[0]
    h2_ref[...] = h2
    qp_ref[...] = jnp.dot(h2.astype(bf16), wq_ref[...], preferred_element_type=f32)


def _outproj_call(oa, ob, x2, g1, ga, gb, wa, wb, gf, sc, sh, wq, seq, tm):
    n, d = x2.shape
    per_b = seq // tm
    tok = lambda i: (i, 0)
    bat = lambda i: (i // per_b, 0, 0)
    cst = lambda i: (0, 0)
    nq = wq.shape[1]
    return pl.pallas_call(
        _outproj_kernel,
        grid=(n // tm,),
        in_specs=[pl.BlockSpec((tm, WIDTH_A), tok),
                  pl.BlockSpec((None, N_HEADS_B, HEAD_DIM_B, tm), lambda i: (i // per_b, 0, 0, i % per_b)),
                  pl.BlockSpec((tm, d), tok), pl.BlockSpec((1, 1, d), bat),
                  pl.BlockSpec((1, WIDTH_A), cst), pl.BlockSpec((1, WIDTH_B), cst),
                  pl.BlockSpec((WIDTH_A, d), cst), pl.BlockSpec((WIDTH_B, d), cst),
                  pl.BlockSpec((1, d), cst), pl.BlockSpec((1, 1, d), bat),
                  pl.BlockSpec((1, 1, d), bat), pl.BlockSpec((d, nq), cst)],
        out_specs=[pl.BlockSpec((tm, d), tok), pl.BlockSpec((tm, d), tok),
                   pl.BlockSpec((tm, nq), tok)],
        out_shape=[jax.ShapeDtypeStruct((n, d), f32), jax.ShapeDtypeStruct((n, d), f32),
                   jax.ShapeDtypeStruct((n, nq), f32)],
        compiler_params=_params(("parallel",)),
        name="outproj",
    )(oa, ob, x2, g1, ga, gb, wa, wb, gf, sc, sh, wq)


def _argmax_rows(x, iota):
    vals = [x[j:j + 8] for j in range(0, x.shape[0], 8)]
    idxs = [iota[j:j + 8] for j in range(0, x.shape[0], 8)]
    while len(vals) > 1:
        nv, ni = [], []
        for a in range(0, len(vals) - 1, 2):
            keep = vals[a] >= vals[a + 1]
            nv.append(jnp.where(keep, vals[a], vals[a + 1]))
            ni.append(jnp.where(keep, idxs[a], idxs[a + 1]))
        if len(vals) % 2:
            nv.append(vals[-1])
            ni.append(idxs[-1])
        vals, idxs = nv, ni
    m = jnp.max(vals[0], axis=0, keepdims=True)
    pos = jnp.min(jnp.where(vals[0] == m, idxs[0], x.shape[0]), axis=0, keepdims=True)
    return m, pos


def _topk_rows(x, kk):
    iota = lax.broadcasted_iota(i32, x.shape, 0)
    vals, idxs = [], []
    for _ in range(kk):
        m, pos = _argmax_rows(x, iota)
        vals.append(m)
        idxs.append(pos)
        x = jnp.where(iota == pos, -jnp.inf, x)
    return jnp.concatenate(vals, axis=0), jnp.concatenate(idxs, axis=0)


CAND_PAIRS = [(a, b) for a in range(PEER_TOPK) for b in range(PEER_TOPK) if (a + 1) * (b + 1) <= PEER_TOPK]
CAND_ROWS = -(-len(CAND_PAIRS) // 8) * 8


def _cand_select():
    sel = np.zeros((2, CAND_ROWS, PEER_TOPK), np.float32)
    for r, (a, b) in enumerate(CAND_PAIRS):
        sel[0, r, a] = 1.0
        sel[1, r, b] = 1.0
    return sel


def _route_kernel(qp_ref, k1_ref, k2_ref, sel_ref, dup_ref, rows_ref, g_ref, *, nl):
    half = PEER_KEY_DIM // 2
    hp = lax.Precision.HIGHEST
    pick = lambda j, v: jnp.dot(sel_ref[j], v, preferred_element_type=f32, precision=hp)
    rows = lax.broadcasted_iota(i32, (CAND_ROWS, LANES), 0)

    def head(q):
        s1 = lax.dot_general(k1_ref[...], q[:, :half], NT_DIMS, preferred_element_type=f32, precision=hp)
        s2 = lax.dot_general(k2_ref[...], q[:, half:], NT_DIMS, preferred_element_type=f32, precision=hp)
        v1, i1 = _topk_rows(s1, PEER_TOPK)
        v2, i2 = _topk_rows(s2, PEER_TOPK)
        cand = jnp.where(rows < len(CAND_PAIRS), pick(0, v1) + pick(1, v2), -jnp.inf)
        cand_e = pick(0, i1.astype(f32)) * N_KEYS + pick(1, i2.astype(f32))
        best, experts = [], []
        for _ in range(PEER_TOPK):
            m, pos = _argmax_rows(cand, rows)
            hit = rows == pos
            experts.append(jnp.sum(jnp.where(hit, cand_e, 0.0), axis=0, keepdims=True))
            best.append(m)
            cand = jnp.where(hit, -jnp.inf, cand)
        best = jnp.concatenate(best, axis=0)
        e = jnp.exp(best - best[0:1])
        return jnp.concatenate(experts, axis=0), e / jnp.sum(e, axis=0, keepdims=True)

    def lane_tile(l, carry):
        start = pl.multiple_of(l * LANES, LANES)
        outs = [head(qp_ref[pl.ds(start, LANES), h * PEER_KEY_DIM:(h + 1) * PEER_KEY_DIM])
                for h in range(PEER_HEADS)]
        ids = jnp.concatenate([o[0] for o in outs], axis=0)
        gates = jnp.concatenate([o[1] for o in outs], axis=0)
        rows_ref[pl.ds(start, LANES), :] = (ids.T * ROWS_PER_EXPERT).astype(i32)
        g_ref[pl.ds(start, LANES), :] = jnp.dot(dup_ref[...], gates, preferred_element_type=f32,
                                                precision=hp).T
        return carry

    lax.fori_loop(0, nl, lane_tile, 0)


def _route_call(qp, k1, k2, tr):
    n = qp.shape[0]
    sel = jnp.asarray(_cand_select())
    dup = jnp.asarray(np.repeat(np.eye(PEER_SLOTS, dtype=np.float32), 2, axis=0))
    kern = functools.partial(_route_kernel, nl=tr // LANES)
    cst = lambda i: (0, 0)
    return pl.pallas_call(
        kern,
        grid=(n // tr,),
        in_specs=[pl.BlockSpec((tr, PEER_HEADS * PEER_KEY_DIM), lambda i: (i, 0)),
                  pl.BlockSpec(k1.shape, cst),
                  pl.BlockSpec(k2.shape, cst),
                  pl.BlockSpec(sel.shape, lambda i: (0, 0, 0)),
                  pl.BlockSpec(dup.shape, cst)],
        out_specs=[pl.BlockSpec((tr, PEER_SLOTS), lambda i: (i, 0)),
                   pl.BlockSpec((tr, 2 * PEER_SLOTS), lambda i: (i, 0))],
        out_shape=[jax.ShapeDtypeStruct((n, PEER_SLOTS), i32),
                   jax.ShapeDtypeStruct((n, 2 * PEER_SLOTS), f32)],
        compiler_params=_params(("parallel",)),
        name="route",
    )(qp, k1, k2, sel, dup)


ROWS_PER_EXPERT = 4
TILE_STRIDE = 136
TOK_UNROLL = 64
HALF = 512


def _pack_kernel(t_ref, o_ref):
    te = t_ref.shape[0]
    bits = lax.bitcast_convert_type(t_ref[...].astype(bf16).astype(f32), i32)
    word = (bits[:, HALF:] & jnp.int32(-65536)) | lax.shift_right_logical(bits[:, :HALF], 16)
    for r in range(ROWS_PER_EXPERT):
        o_ref[pl.ds(r, te, stride=ROWS_PER_EXPERT), :] = word[:, r * LANES:(r + 1) * LANES]


def _pack_table(tab, te=256):
    e, d = tab.shape
    return pl.pallas_call(
        _pack_kernel,
        grid=(e // te,),
        in_specs=[pl.BlockSpec((te, d), lambda i: (i, 0))],
        out_specs=pl.BlockSpec((te * ROWS_PER_EXPERT, LANES), lambda i: (i, 0)),
        out_shape=jax.ShapeDtypeStruct((e * ROWS_PER_EXPERT, LANES), i32),
        compiler_params=_params(("parallel",)),
        name="pack",
    )(tab)


def _gather_tile(tab_ref, rows_ref, t, tile_ref):
    tok_rows = rows_ref.at[t]
    for k in range(PEER_SLOTS):
        row = pl.multiple_of(tok_rows[k], ROWS_PER_EXPERT)
        tile_ref[pl.ds(k, ROWS_PER_EXPERT, stride=TILE_STRIDE), :] = tab_ref[pl.ds(row, ROWS_PER_EXPERT), :]
    chunks = [pltpu.bitcast(tile_ref[j * TILE_STRIDE:j * TILE_STRIDE + PEER_SLOTS, :], bf16)
              for j in range(ROWS_PER_EXPERT)]
    return jnp.concatenate(chunks, axis=1)


def _for_tokens(tab_ref, rows_ref, tile_scr, base, per_token):
    for tl in range(TOK_UNROLL):
        per_token(tl, base + tl, _gather_tile(tab_ref, rows_ref, base + tl, tile_scr.at[tl % 2]))


def _peer_u_kernel(ids_ref, h_ref, g_ref, u_ref, o_ref, tile_scr, act_scr, *, tt):
    even = lax.broadcasted_iota(i32, (1, 2 * PEER_SLOTS), 1) % 2 == 0

    def group(gi, carry):
        base = pl.multiple_of(gi * TOK_UNROLL, TOK_UNROLL)
        rows = []

        def per_token(tl, t, r):
            row = h_ref[pl.ds(t, 1), :]
            h = jnp.concatenate([row[:, :HALF], row[:, HALF:]], axis=0)
            major = h.astype(bf16)
            minor = (h - major.astype(f32)).astype(bf16)
            out = lax.dot_general(jnp.concatenate([major, minor], axis=0), r, NT_DIMS,
                                  preferred_element_type=f32)
            rows.append(jnp.where(even, out[0:1] + out[2:3], out[1:2] + out[3:4]))

        _for_tokens(u_ref, ids_ref, tile_scr, base, per_token)
        act_scr[pl.ds(base, TOK_UNROLL), :] = jnp.concatenate(rows, axis=0)
        return carry

    lax.fori_loop(0, tt // TOK_UNROLL, group, 0)
    part = act_scr[...]
    act = part + jnp.where(even, pltpu.roll(part, 2 * PEER_SLOTS - 1, 1), pltpu.roll(part, 1, 1))
    gelu = 0.5 * act * (1.0 + lax.erf(act * (2.0 ** -0.5)))
    o_ref[...] = g_ref[...] * gelu


def _peer_u_call(ids, h3, g2, u_packed, tt):
    n = ids.shape[0]
    kern = functools.partial(_peer_u_kernel, tt=tt)
    return pl.pallas_call(
        kern,
        grid=(n // tt,),
        in_specs=[pl.BlockSpec((tt, PEER_SLOTS), lambda i: (i, 0), memory_space=pltpu.SMEM),
                  pl.BlockSpec((tt, 2 * HALF), lambda i: (i, 0)),
                  pl.BlockSpec((tt, 2 * PEER_SLOTS), lambda i: (i, 0)),
                  pl.BlockSpec(u_packed.shape, lambda i: (0, 0), pipeline_mode=pl.Buffered(1))],
        out_specs=pl.BlockSpec((tt, 2 * PEER_SLOTS), lambda i: (i, 0)),
        out_shape=jax.ShapeDtypeStruct((n, 2 * PEER_SLOTS), f32),
        scratch_shapes=[pltpu.VMEM((2, ROWS_PER_EXPERT * TILE_STRIDE, LANES), i32),
                        pltpu.VMEM((tt, 2 * PEER_SLOTS), f32)],
        compiler_params=_params(("arbitrary",)),
        name="peer_u",
    )(ids, h3, g2, u_packed)


def _peer_v_kernel(ids_ref, w_ref, v_ref, o_ref, tile_scr, *, tt):
    even = lax.broadcasted_iota(i32, (1, 2 * PEER_SLOTS), 1) % 2 == 0

    def group(gi, carry):
        base = pl.multiple_of(gi * TOK_UNROLL, TOK_UNROLL)
        wts = w_ref[pl.ds(base, TOK_UNROLL), :]
        rows = []

        def per_token(tl, t, r):
            w = wts[tl:tl + 1]
            lhs = jnp.concatenate([jnp.where(even, w, 0.0), jnp.where(even, 0.0, w)], axis=0)
            out = jnp.dot(lhs.astype(bf16), r, preferred_element_type=f32)
            rows.append(jnp.concatenate([out[0:1], out[1:2]], axis=1))
            if len(rows) == 8:
                o_ref[pl.ds(pl.multiple_of(base + tl - 7, 8), 8), :] = jnp.concatenate(rows, axis=0)
                rows.clear()

        _for_tokens(v_ref, ids_ref, tile_scr, base, per_token)
        return carry

    lax.fori_loop(0, tt // TOK_UNROLL, group, 0)


def _peer_v_call(ids, wgt2, v_packed, tt):
    n = ids.shape[0]
    kern = functools.partial(_peer_v_kernel, tt=tt)
    return pl.pallas_call(
        kern,
        grid=(n // tt,),
        in_specs=[pl.BlockSpec((tt, PEER_SLOTS), lambda i: (i, 0), memory_space=pltpu.SMEM),
                  pl.BlockSpec((tt, 2 * PEER_SLOTS), lambda i: (i, 0)),
                  pl.BlockSpec(v_packed.shape, lambda i: (0, 0), pipeline_mode=pl.Buffered(1))],
        out_specs=pl.BlockSpec((tt, 2 * HALF), lambda i: (i, 0)),
        out_shape=jax.ShapeDtypeStruct((n, 2 * HALF), f32),
        scratch_shapes=[pltpu.VMEM((2, ROWS_PER_EXPERT * TILE_STRIDE, LANES), i32)],
        compiler_params=_params(("arbitrary",)),
        name="peer_v",
    )(ids, wgt2, v_packed)


def _final_kernel(x1_ref, p_ref, g2_ref, gf_ref, o_ref):
    x2 = x1_ref[...] + g2_ref[0] * p_ref[...]
    o_ref[...] = x2 * lax.rsqrt(jnp.mean(x2 * x2, axis=-1, keepdims=True) + EPS) * gf_ref[...]


def _final_call(x1, peer, g2, gf, seq, tm):
    n, d = x1.shape
    per_b = seq // tm
    tok = lambda i: (i, 0)
    return pl.pallas_call(
        _final_kernel,
        grid=(n // tm,),
        in_specs=[pl.BlockSpec((tm, d), tok), pl.BlockSpec((tm, d), tok),
                  pl.BlockSpec((1, 1, d), lambda i: (i // per_b, 0, 0)),
                  pl.BlockSpec((1, d), lambda i: (0, 0))],
        out_specs=pl.BlockSpec((tm, d), tok),
        out_shape=jax.ShapeDtypeStruct((n, d), f32),
        compiler_params=_params(("parallel",)),
        name="final",
    )(x1, peer, g2, gf)


def _rope_tables(positions):
    def tab(rot):
        inv = ROPE_THETA ** (-jnp.arange(0, rot, 2, dtype=f32) / rot)
        ang = positions.astype(f32)[..., None] * inv
        return jnp.cos(ang), jnp.sin(ang)

    cos_a, sin_a = tab(ROPE_DIM_A)
    cos_i, sin_i = tab(IDX_ROPE_DIM)
    b, s = positions.shape
    one = jnp.ones((b, s, 1), f32)
    zero = jnp.zeros((b, s, 4), f32)
    trig = jnp.concatenate([cos_a, cos_i, one, zero[..., :3], sin_a, sin_i, zero], axis=-1)
    return trig.reshape(b * s, TRIG)


def kernel(x, c, positions, w_ada, b_ada, g_mix, w_in, g_kv, w_uk, w_uv, b_forget,
           g_out_a, g_out_b, w_out, g_ffn, w_peer_q, peer_keys1, peer_keys2, peer_u,
           peer_v, g_final):
    b, s, d = x.shape
    n = b * s
    assert w_ada.shape[0] == 1, "single layer supported"
    tm = min(TOKEN_TILE, s)
    x2 = x.reshape(n, d)

    mod = _mod_call(c, w_ada[0], b_ada[0])
    mod = mod.reshape(b, 6, 1, d)
    shift1, scale1, gate1, shift2, scale2, gate2 = [mod[:, j] for j in range(6)]

    perm = _in_perm()
    w_in_r = jnp.where((perm >= 0)[None, :], w_in[0][:, np.maximum(perm, 0)], 0.0).astype(bf16)
    proj, kcat, ik, latt, vbt = _inproj_call(x2, scale1, shift1, g_mix[0].reshape(1, d), w_in_r,
                                             _rope_tables(positions), g_kv[0].reshape(1, KV_RANK), s, tm)

    wuk = w_uk[0].transpose(1, 0, 2).astype(bf16)
    wuvt = w_uv[0].transpose(1, 2, 0).astype(bf16)
    o_a = _dsa_call(proj, kcat.reshape(b, s, KCAT), latt, ik.reshape(b, s, LANES), wuk, wuvt,
                    b, s, tq=DSA_QUERY_TILE, tk=tm)

    cum = _cum_call(proj, b_forget[0], b, s)
    o_bt = _fox_call(proj, vbt, cum[:, :, None, :], cum.reshape(b, N_HEADS_B, s // tm, 1, tm),
                     tq=min(FOX_QUERY_TILE, s), ts=tm)

    wo = w_out[0].astype(bf16)
    x1, h2, qp = _outproj_call(o_a, o_bt, x2, gate1, g_out_a[0].reshape(1, WIDTH_A),
                               g_out_b[0].reshape(1, WIDTH_B), wo[:WIDTH_A], wo[WIDTH_A:],
                               g_ffn[0].reshape(1, d), scale2, shift2,
                               w_peer_q[0].astype(bf16), s, tm)

    ids, gates2 = _route_call(qp, peer_keys1[0], peer_keys2[0], tr=min(ROUTE_TILE, n))
    wgt2 = _peer_u_call(ids, h2, gates2, _pack_table(peer_u[0]), PEER_TILE)
    peer = _peer_v_call(ids, wgt2, _pack_table(peer_v[0]), PEER_TILE)

    out = _final_call(x1, peer, gate2, g_final.reshape(1, d), s, tm)
    return out.reshape(b, s, d)
```
